```python
import jax, jax.numpy as jnp
from jax import lax
import numpy as np

D_MODEL = 1024
BATCH = 16
SEQ = 2048
DEPTH = 1

MEM_LEN = 256
EPS = 1e-6

HG_HEADS = 4
HG_DK = 128
HG_DV = 128
HG_WIDTH = HG_HEADS * HG_DV
HG_KEY_WIDTH = HG_HEADS * HG_DK
HG_CHUNK = 64

SW_HEADS = 8
SW_KV_HEADS = 2
SW_GROUP = SW_HEADS // SW_KV_HEADS
SW_HEAD_DIM = 64
SW_WIDTH = SW_HEADS * SW_HEAD_DIM
WINDOW = 128
SW_BLOCK = 128
ROPE_THETA = 500000.0
ROT_DIM = SW_HEAD_DIM // 4

MIX_WIDTH = HG_WIDTH + SW_WIDTH
IN_SPLITS = (HG_KEY_WIDTH, HG_KEY_WIDTH, HG_WIDTH, HG_WIDTH,
             SW_WIDTH, SW_KV_HEADS * SW_HEAD_DIM, SW_KV_HEADS * SW_HEAD_DIM)
IN_WIDTH = sum(IN_SPLITS)

XA_HEADS = 4
XA_HEAD_DIM = 128
XA_WIDTH = XA_HEADS * XA_HEAD_DIM

D_FF = 4 * D_MODEL

kernel_name = "hymba_style_hgrn2_swa_sink_hybrid"


def rms_norm(x, g):
    xf = x.astype(jnp.float32)
    y = xf * lax.rsqrt(jnp.mean(xf * xf, axis=-1, keepdims=True) + EPS)
    return (y * g.astype(jnp.float32)).astype(x.dtype)


def head_rms(t, g):
    return t * lax.rsqrt(jnp.mean(t * t, axis=-1, keepdims=True) + EPS) * g.astype(jnp.float32)


def partial_rope(t, positions):
    half = ROT_DIM // 2
    inv_freq = ROPE_THETA ** (-(jnp.arange(half, dtype=jnp.float32) * 2.0 / ROT_DIM))
    ang = positions.astype(jnp.float32)[..., None] * inv_freq
    cos = jnp.cos(ang)[:, :, None, :]
    sin = jnp.sin(ang)[:, :, None, :]
    x1 = t[..., :half]
    x2 = t[..., half:ROT_DIM]
    return jnp.concatenate([x1 * cos - x2 * sin, x2 * cos + x1 * sin, t[..., ROT_DIM:]], axis=-1)


def hgrn2_group(q, f_logit, i, g, lb, norm_g):
    B, S, _ = q.shape
    H, C = HG_HEADS, HG_CHUNK
    nc = S // C
    qf = q.astype(jnp.float32).reshape(B, S, H, HG_DK)
    vf = i.astype(jnp.float32).reshape(B, S, H, HG_DV)
    lbh = lb.astype(jnp.float32).reshape(H, HG_DK)
    f = lbh + (1.0 - lbh) * jax.nn.sigmoid(f_logit.astype(jnp.float32).reshape(B, S, H, HG_DK))
    kf = 1.0 - f
    log_f = jnp.log(f)

    def to_chunks(t):
        return t.reshape(B, nc, C, H, t.shape[-1]).transpose(1, 0, 3, 2, 4)

    causal = jnp.tril(jnp.ones((C, C), dtype=bool))[:, :, None]

    def step(state, inp):
        qc, kc, vc, lfc = inp
        b = jnp.cumsum(lfc, axis=2)
        o_inter = jnp.einsum('bhik,bhkv->bhiv', qc * jnp.exp(b), state)
        diff = b[:, :, :, None, :] - b[:, :, None, :, :]
        decay = jnp.exp(jnp.where(causal, diff, -jnp.inf))
        scores = jnp.einsum('bhik,bhjk,bhijk->bhij', qc, kc, decay)
        o_intra = jnp.einsum('bhij,bhjv->bhiv', scores, vc)
        b_last = b[:, :, -1:, :]
        new_state = (jnp.exp(b_last[:, :, 0, :])[..., None] * state
                     + jnp.einsum('bhjk,bhjv->bhkv', kc * jnp.exp(b_last - b), vc))
        return new_state, o_inter + o_intra

    s0 = jnp.zeros((B, H, HG_DK, HG_DV), jnp.float32)
    _, o = lax.scan(step, s0, (to_chunks(qf), to_chunks(kf), to_chunks(vf), to_chunks(log_f)))
    o = o.transpose(1, 0, 3, 2, 4).reshape(B, S, H, HG_DV)
    gate = jax.nn.silu(g.astype(jnp.float32)).reshape(B, S, H, HG_DV)
    return (head_rms(o, norm_g) * gate).reshape(B, S, HG_WIDTH)


def swa_group(q, k, v, positions, q_norm_g, k_norm_g, sinks):
    B, S, _ = q.shape
    nb = S // SW_BLOCK
    qf = partial_rope(head_rms(q.astype(jnp.float32).reshape(B, S, SW_HEADS, SW_HEAD_DIM), q_norm_g), positions)
    kf = partial_rope(head_rms(k.astype(jnp.float32).reshape(B, S, SW_KV_HEADS, SW_HEAD_DIM), k_norm_g), positions)
    vf = v.astype(jnp.float32).reshape(B, S, SW_KV_HEADS, SW_HEAD_DIM)

    qb = qf.reshape(B, nb, SW_BLOCK, SW_KV_HEADS, SW_GROUP, SW_HEAD_DIM)

    def band(t):
        tp = jnp.pad(t, ((0, 0), (SW_BLOCK, 0), (0, 0), (0, 0))).reshape(B, nb + 1, SW_BLOCK, SW_KV_HEADS, SW_HEAD_DIM)
        return jnp.concatenate([tp[:, :-1], tp[:, 1:]], axis=2)

    kw, vw = band(kf), band(vf)
    s = jnp.einsum('bnqhgd,bnkhd->bnhgqk', qb, kw) * (SW_HEAD_DIM ** -0.5)

    qi = jnp.arange(SW_BLOCK)[:, None]
    kj = jnp.arange(2 * SW_BLOCK)[None, :]
    dist = qi + SW_BLOCK - kj
    in_band = (dist >= 0) & (dist < WINDOW)
    blk = jnp.arange(nb)[:, None, None]
    valid = in_band[None] & (blk * SW_BLOCK + kj[None] - SW_BLOCK >= 0)
    s = jnp.where(valid[None, :, None, None], s, -jnp.inf)

    sink = sinks.astype(jnp.float32).reshape(SW_KV_HEADS, SW_GROUP)[None, None, :, :, None, None]
    m = jnp.maximum(jnp.max(s, axis=-1, keepdims=True), sink)
    p = jnp.exp(s - m)
    denom = jnp.sum(p, axis=-1, keepdims=True) + jnp.exp(sink - m)
    o = jnp.einsum('bnhgqk,bnkhd->bnqhgd', p / denom, vw)
    return o.reshape(B, S, SW_WIDTH)


def memory_cross_attention(hn, mn, wq, wkv, q_norm_g, k_norm_g, wo):
    B, S, _ = hn.shape
    M = mn.shape[1]
    q = head_rms((hn @ wq).astype(jnp.float32).reshape(B, S, XA_HEADS, XA_HEAD_DIM), q_norm_g)
    kv = (mn @ wkv).astype(jnp.float32)
    k = head_rms(kv[..., :XA_WIDTH].reshape(B, M, XA_HEADS, XA_HEAD_DIM), k_norm_g)
    v = kv[..., XA_WIDTH:].reshape(B, M, XA_HEADS, XA_HEAD_DIM)
    s = jnp.einsum('bshd,bmhd->bhsm', q, k) * (XA_HEAD_DIM ** -0.5)
    p = jax.nn.softmax(s, axis=-1)
    o = jnp.einsum('bhsm,bmhd->bshd', p, v).reshape(B, S, XA_WIDTH)
    return o.astype(hn.dtype) @ wo


def setup_inputs(seed: int = 0) -> dict:
    key = jax.random.key(seed)
    ks = jax.random.split(key, 24)
    f32 = jnp.float32

    def nrm(k, shape, scale):
        return jax.random.normal(k, shape, f32) * scale

    def gain(k, shape):
        return 1.0 + 0.02 * jax.random.normal(k, shape, f32)

    offs = jax.random.randint(ks[2], (BATCH, 1), 0, 4096, dtype=jnp.int32)
    positions = (offs + jnp.arange(SEQ, dtype=jnp.int32)[None, :]).astype(jnp.int32)
    return {
        "x": nrm(ks[0], (BATCH, SEQ, D_MODEL), 1.0),
        "mem": nrm(ks[1], (BATCH, MEM_LEN, D_MODEL), 1.0),
        "positions": positions,
        "norm1_g": gain(ks[3], (DEPTH, D_MODEL)),
        "w_in": nrm(ks[4], (DEPTH, D_MODEL, IN_WIDTH), D_MODEL ** -0.5),
        "hg_lower_bounds": nrm(ks[5], (DEPTH + 1, HG_KEY_WIDTH), 0.1),
        "hg_norm_g": gain(ks[6], (DEPTH, HG_DV)),
        "sw_q_norm_g": gain(ks[7], (DEPTH, SW_HEAD_DIM)),
        "sw_k_norm_g": gain(ks[8], (DEPTH, SW_HEAD_DIM)),
        "sw_sinks": nrm(ks[9], (DEPTH, SW_HEADS), 0.5),
        "w_out": nrm(ks[10], (DEPTH, MIX_WIDTH, D_MODEL), MIX_WIDTH ** -0.5),
        "norm2_g": gain(ks[11], (DEPTH, D_MODEL)),
        "mem_norm_g": gain(ks[12], (DEPTH, D_MODEL)),
        "xa_wq": nrm(ks[13], (DEPTH, D_MODEL, XA_WIDTH), D_MODEL ** -0.5),
        "xa_wkv": nrm(ks[14], (DEPTH, D_MODEL, 2 * XA_WIDTH), D_MODEL ** -0.5),
        "xa_q_norm_g": gain(ks[15], (DEPTH, XA_HEAD_DIM)),
        "xa_k_norm_g": gain(ks[16], (DEPTH, XA_HEAD_DIM)),
        "xa_wo": nrm(ks[17], (DEPTH, XA_WIDTH, D_MODEL), XA_WIDTH ** -0.5),
        "norm3_g": gain(ks[18], (DEPTH, D_MODEL)),
        "mlp_up": nrm(ks[19], (DEPTH, D_MODEL, D_FF), D_MODEL ** -0.5),
        "mlp_down": nrm(ks[20], (DEPTH, D_FF, D_MODEL), D_FF ** -0.5),
    }


def reference(x, mem, positions, norm1_g, w_in, hg_lower_bounds, hg_norm_g, sw_q_norm_g,
              sw_k_norm_g, sw_sinks, w_out, norm2_g, mem_norm_g, xa_wq, xa_wkv, xa_q_norm_g,
              xa_k_norm_g, xa_wo, norm3_g, mlp_up, mlp_down):
    lb_all = jnp.cumsum(jax.nn.softmax(hg_lower_bounds.astype(jnp.float32), axis=0), axis=0)
    split_points = [int(p) for p in np.cumsum(IN_SPLITS)[:-1]]
    h = x
    for l in range(DEPTH):
        hn = rms_norm(h, norm1_g[l])
        proj = hn @ w_in[l]
        hq, hf, hi, hg, sq, sk, sv = jnp.split(proj, split_points, axis=-1)
        y_hg = hgrn2_group(hq, hf, hi, hg, lb_all[l], hg_norm_g[l])
        y_sw = swa_group(sq, sk, sv, positions, sw_q_norm_g[l], sw_k_norm_g[l], sw_sinks[l])
        mix = jnp.concatenate([y_hg, y_sw], axis=-1).astype(h.dtype) @ w_out[l]
        h = h + mix
        hn = rms_norm(h, norm2_g[l])
        mn = rms_norm(mem, mem_norm_g[l])
        h = h + memory_cross_attention(hn, mn, xa_wq[l], xa_wkv[l], xa_q_norm_g[l], xa_k_norm_g[l], xa_wo[l])
        hn = rms_norm(h, norm3_g[l])
        a = jax.nn.relu(hn @ mlp_up[l])
        h = h + (a * a) @ mlp_down[l]
    return h
```

```python
import functools

import numpy as np
import jax
import jax.numpy as jnp
from jax import lax
from jax.experimental import pallas as pl
from jax.experimental.pallas import tpu as pltpu

F32 = jnp.float32
BF16 = jnp.bfloat16

LANES = 128
SUBLANES = 8
VMEM_LIMIT_BYTES = 56 * 1024 * 1024

D_MODEL = 1024
EPS = 1e-6

HG_HEADS = 4
HG_DK = 128
HG_DV = 128
HG_WIDTH = HG_HEADS * HG_DV
HG_KEY_WIDTH = HG_HEADS * HG_DK

SW_HEADS = 8
SW_KV_HEADS = 2
SW_HEAD_DIM = 64
SW_WIDTH = SW_HEADS * SW_HEAD_DIM
SW_KV_WIDTH = SW_KV_HEADS * SW_HEAD_DIM
WINDOW = 128
SW_BLOCK = 128
ROPE_THETA = 500000.0
ROT_DIM = SW_HEAD_DIM // 4
ROT_HALF = ROT_DIM // 2

MIX_WIDTH = HG_WIDTH + SW_WIDTH
IN_WIDTH = 2 * HG_KEY_WIDTH + 2 * HG_WIDTH + SW_WIDTH + 2 * SW_KV_WIDTH

XA_HEADS = 4
XA_HEAD_DIM = 128
XA_WIDTH = XA_HEADS * XA_HEAD_DIM

N_SLABS = IN_WIDTH // LANES
SLAB_Q, SLAB_F, SLAB_I, SLAB_G = 0, 4, 8, 12
SLAB_SQ, SLAB_SK, SLAB_SV = 16, 20, 21

HG_CHUNK = 64
N_PLANES = 8
PLANE_ROWS = HG_CHUNK // N_PLANES
N_LEVELS = 7

MIX_TILE = 256
TAIL_TILE = 512
FF_CHUNK = 1024


def _mul(a, b):
    if a is None:
        return b
    if b is None:
        return a
    return a * b


def _rms_norm(x, g):
    return x * lax.rsqrt(jnp.mean(x * x, axis=-1, keepdims=True) + EPS) * g


def _dot(a, b):
    return jnp.dot(a, b, preferred_element_type=F32)


def _dot_nt(a, b):
    return lax.dot_general(a, b, (((1,), (1,)), ((), ())), preferred_element_type=F32)


def _dot_tn(a, b):
    return lax.dot_general(a, b, (((0,), (0,)), ((), ())), preferred_element_type=F32)


def _rope_table_kernel(pos_ref, invf_ref, cos_ref, sin_ref):
    ang = pos_ref[0].astype(F32) * invf_ref[...]
    cos_ref[0] = jnp.cos(ang)
    sin_ref[0] = jnp.sin(ang)


def _rope_tables(positions):
    B, S = positions.shape
    inv_freq = ROPE_THETA ** (-(jnp.arange(ROT_HALF, dtype=F32) * 2.0 / ROT_DIM))
    cos, sin = pl.pallas_call(
        _rope_table_kernel,
        grid=(B,),
        in_specs=[pl.BlockSpec((1, 1, S), lambda b: (b, 0, 0)),
                  pl.BlockSpec((ROT_HALF, 1), lambda b: (0, 0))],
        out_specs=[pl.BlockSpec((1, ROT_HALF, S), lambda b: (b, 0, 0))] * 2,
        out_shape=[jax.ShapeDtypeStruct((B, ROT_HALF, S), F32)] * 2,
        name="rope_tables",
    )(positions.reshape(B, 1, S), inv_freq.reshape(ROT_HALF, 1))
    cos = cos.transpose(0, 2, 1)
    sin = sin.transpose(0, 2, 1)
    rest = SW_HEAD_DIM - ROT_DIM
    ctab = jnp.concatenate([cos, cos, jnp.ones((B, S, rest), F32)], axis=-1)
    stab = jnp.concatenate([-sin, sin, jnp.zeros((B, S, rest), F32)], axis=-1)
    reps = LANES // SW_HEAD_DIM
    return jnp.tile(ctab, (1, 1, reps)), jnp.tile(stab, (1, 1, reps))


def _rope(t, ctab, stab):
    width = t.shape[1]
    reps = width // LANES
    c = jnp.concatenate([ctab] * reps, axis=1) if reps > 1 else ctab
    s = jnp.concatenate([stab] * reps, axis=1) if reps > 1 else stab
    lane = lax.broadcasted_iota(jnp.int32, t.shape, 1)
    first_half = (lane % SW_HEAD_DIM) < ROT_HALF
    partner = jnp.where(first_half,
                        pltpu.roll(t, width - ROT_HALF, 1),
                        pltpu.roll(t, ROT_HALF, 1))
    return t * c + partner * s


def _mem_kv_kernel(mem_ref, g_ref, wkv_ref, gk_ref, k_ref, v_ref):
    mn = _rms_norm(mem_ref[0], g_ref[...]).astype(BF16)
    kv = _dot(mn, wkv_ref[...])
    for h in range(XA_HEADS):
        sl = slice(h * XA_HEAD_DIM, (h + 1) * XA_HEAD_DIM)
        k_ref[0, :, sl] = _rms_norm(kv[:, sl], gk_ref[...]).astype(BF16)
    v_ref[0] = kv[:, XA_WIDTH:].astype(BF16)


def _mem_kv(mem, g, wkv, gk):
    B, M, D = mem.shape
    return pl.pallas_call(
        _mem_kv_kernel,
        grid=(B,),
        in_specs=[pl.BlockSpec((1, M, D), lambda b: (b, 0, 0)),
                  pl.BlockSpec((1, D), lambda b: (0, 0)),
                  pl.BlockSpec((D, 2 * XA_WIDTH), lambda b: (0, 0)),
                  pl.BlockSpec((1, XA_HEAD_DIM), lambda b: (0, 0))],
        out_specs=[pl.BlockSpec((1, M, XA_WIDTH), lambda b: (b, 0, 0))] * 2,
        out_shape=[jax.ShapeDtypeStruct((B, M, XA_WIDTH), BF16)] * 2,
        name="mem_kv",
    )(mem, g, wkv, gk)


def _hgrn2_level_masks():
    p = np.arange(HG_CHUNK)
    tok = N_PLANES * (p % PLANE_ROWS) + p // PLANE_ROWS
    ti, tj = tok[:, None], tok[None, :]
    masks = []
    h = 1
    while h < HG_CHUNK:
        masks.append((ti // (2 * h) == tj // (2 * h)) & ((ti // h) % 2 == 1) & ((tj // h) % 2 == 0))
        h *= 2
    masks.append(ti == tj)
    masks = np.stack(masks).astype(np.float32)
    assert masks.shape[0] == N_LEVELS
    assert (masks.sum(0) == (tj <= ti)).all()
    return masks


def _plane_block_products(f):
    pre = {1: list(f)}
    suf = {1: [None] * N_PLANES}
    h = 1
    while h < N_PLANES:
        p, s = pre[h], suf[h]
        new_p, new_s = [], []
        for r in range(N_PLANES):
            blk = r // h
            if blk % 2 == 1:
                new_p.append(_mul(p[r], p[blk * h - 1]))
                new_s.append(s[r])
            else:
                new_p.append(p[r])
                new_s.append(_mul(s[r], p[(blk + 2) * h - 1]))
        pre[2 * h], suf[2 * h] = new_p, new_s
        h *= 2
    return pre, suf


def _group_block_products(total):
    m_idx = lax.broadcasted_iota(jnp.int32, total.shape, 0)
    wpre, wsuf = [None], [None]
    for d in range(1, PLANE_ROWS):
        wpre.append(_mul(wpre[-1], pltpu.roll(total, d, 0)))
        wsuf.append(_mul(wsuf[-1], pltpu.roll(total, PLANE_ROWS - d, 0)))
    cpre, csuf = {1: None}, {1: None}
    u = 2
    while u <= PLANE_ROWS:
        off = m_idx % u
        a = jnp.ones_like(total)
        b = jnp.ones_like(total)
        for d in range(1, u):
            a = jnp.where(off == d, wpre[d], a)
            b = jnp.where(off == u - 1 - d, wsuf[d], b)
        cpre[u], csuf[u] = a, b
        u *= 2
    return cpre, csuf


def _hgrn2_chunk(slab_ref, row0, lb, norm_g, masks_ref, state_ref, out_ref):
    def planes(slab):
        return [slab_ref[slab, pl.ds(row0 + r, PLANE_ROWS, stride=N_PLANES), :]
                for r in range(N_PLANES)]

    def stack(ps):
        return jnp.concatenate(ps, axis=0)

    for h in range(HG_HEADS):
        lb_h = lb[:, h * HG_DK:(h + 1) * HG_DK]
        f = [lb_h + (1.0 - lb_h) * jax.nn.sigmoid(x) for x in planes(SLAB_F + h)]
        q = planes(SLAB_Q + h)
        k = [1.0 - x for x in f]
        v = stack(planes(SLAB_I + h)).astype(BF16)
        gate = stack(planes(SLAB_G + h))

        pre, suf = _plane_block_products(f)
        pre8, suf8 = pre[N_PLANES], suf[N_PLANES]
        cpre, csuf = _group_block_products(pre8[N_PLANES - 1])

        levels = []
        hsz = 1
        while hsz < N_PLANES:
            levels.append((pre[hsz], suf[hsz]))
            hsz *= 2
        u = 1
        while u < PLANE_ROWS:
            levels.append(([_mul(x, cpre[u]) for x in pre8], [_mul(x, csuf[u]) for x in suf8]))
            u *= 2
        levels.append(([None] * N_PLANES, [None] * N_PLANES))

        scores = None
        for lvl, (pq, sk) in enumerate(levels):
            ql = stack([_mul(a, b) for a, b in zip(q, pq)]).astype(BF16)
            kl = stack([_mul(a, b) for a, b in zip(k, sk)]).astype(BF16)
            s = _dot_nt(ql, kl) * masks_ref[lvl]
            scores = s if scores is None else scores + s

        q_full = stack([_mul(x, cpre[PLANE_ROWS]) * a for x, a in zip(pre8, q)]).astype(BF16)
        k_full = stack([_mul(_mul(x, csuf[PLANE_ROWS]), a) for x, a in zip(suf8, k)]).astype(BF16)
        chunk_decay = (pre8[N_PLANES - 1] * cpre[PLANE_ROWS])[PLANE_ROWS - 1:PLANE_ROWS, :]

        state_t = state_ref[h]
        o = _dot_nt(q_full, state_t.astype(BF16)) + _dot(scores.astype(BF16), v)
        state_ref[h] = state_t * chunk_decay + _dot_tn(v, k_full)

        y = _rms_norm(o, norm_g) * (gate * jax.nn.sigmoid(gate))
        for r in range(N_PLANES):
            out_ref[h, pl.ds(row0 + r, PLANE_ROWS, stride=N_PLANES), :] = (
                y[r * PLANE_ROWS:(r + 1) * PLANE_ROWS])


def _swa_block(slab_ref, row0, first_key, ctab, stab, gq, gk, seg_mean, sinks_ref,
               kprev_ref, vprev_ref):
    rows = pl.ds(row0, SW_BLOCK)
    q = jnp.concatenate([slab_ref[SLAB_SQ + s, rows, :] for s in range(SW_WIDTH // LANES)], axis=1)
    k = slab_ref[SLAB_SK, rows, :]
    v = slab_ref[SLAB_SV, rows, :]

    def head_norm(t, g, mean_mat):
        ms = _dot((t * t).astype(BF16), mean_mat)
        return t * lax.rsqrt(ms + EPS) * g

    qn = _rope(head_norm(q, gq, seg_mean[...]), ctab, stab) * (SW_HEAD_DIM ** -0.5)
    kn = _rope(head_norm(k, gk, seg_mean[:LANES, :LANES]), ctab, stab)

    kcat = jnp.concatenate([kprev_ref[...], kn], axis=0)
    vcat = jnp.concatenate([vprev_ref[...], v], axis=0)
    kprev_ref[...] = kn
    vprev_ref[...] = v

    lane = lax.broadcasted_iota(jnp.int32, kcat.shape, 1)
    low = lane < SW_HEAD_DIM
    kroll = pltpu.roll(kcat, SW_HEAD_DIM, 1)
    vroll = pltpu.roll(vcat, SW_HEAD_DIM, 1)
    zero = jnp.zeros_like(kcat)

    def paired(own, rolled, kv_head):
        src_even, src_odd = (own, rolled) if kv_head == 0 else (rolled, own)
        return jnp.concatenate([jnp.where(low, src_even, zero),
                                jnp.where(low, zero, src_odd)], axis=0).astype(BF16)

    qi = lax.broadcasted_iota(jnp.int32, (SW_BLOCK, 2 * SW_BLOCK), 0)
    kj = lax.broadcasted_iota(jnp.int32, (SW_BLOCK, 2 * SW_BLOCK), 1)
    dist = qi + SW_BLOCK - kj
    valid = (dist >= 0) & (dist < WINDOW) & (kj >= first_key)
    lane_o = lax.broadcasted_iota(jnp.int32, (SW_BLOCK, LANES), 1)
    low_o = lane_o < SW_HEAD_DIM

    outs = []
    for pair in range(SW_HEADS // 2):
        kv_head = pair // (SW_HEADS // SW_KV_HEADS // 2)
        kp = paired(kcat, kroll, kv_head)
        vp = paired(vcat, vroll, kv_head)
        s = _dot_nt(qn[:, pair * LANES:(pair + 1) * LANES].astype(BF16), kp)
        ps, inv = [], []
        for half in range(2):
            sink = sinks_ref[2 * pair + half]
            sh = jnp.where(valid, s[:, half * 2 * SW_BLOCK:(half + 1) * 2 * SW_BLOCK], -jnp.inf)
            m = jnp.maximum(jnp.max(sh, axis=-1, keepdims=True), sink)
            p = jnp.exp(sh - m)
            denom = jnp.sum(p, axis=-1, keepdims=True) + jnp.exp(sink - m)
            ps.append(p.astype(BF16))
            inv.append(1.0 / denom)
        o = _dot(jnp.concatenate(ps, axis=1), vp)
        outs.append(o * jnp.where(low_o, inv[0], inv[1]))
    return jnp.concatenate(outs, axis=1)


def _mixer_kernel(sinks_ref, x_ref, ctab_ref, stab_ref, g1_ref, win_ref, hlb_ref, hgn_ref,
                  gq_ref, gk_ref, segm_ref, masks_ref, mix_ref,
                  slab_ref, hg_ref, state_ref, kprev_ref, vprev_ref):
    t = pl.program_id(1)
    tile = x_ref.shape[1]

    @pl.when(t == 0)
    def _():
        state_ref[...] = jnp.zeros_like(state_ref)
        kprev_ref[...] = jnp.zeros_like(kprev_ref)
        vprev_ref[...] = jnp.zeros_like(vprev_ref)

    xn = _rms_norm(x_ref[0], g1_ref[...]).astype(BF16)
    col_step = 4 * LANES
    for c0 in range(0, IN_WIDTH, col_step):
        c1 = min(c0 + col_step, IN_WIDTH)
        proj = _dot(xn, win_ref[:, c0:c1])
        for s in range((c1 - c0) // LANES):
            slab_ref[c0 // LANES + s] = proj[:, s * LANES:(s + 1) * LANES]

    hlb = hlb_ref[...]
    e = jnp.exp(hlb - jnp.max(hlb, axis=0, keepdims=True))
    lb = e[0:1, :] / jnp.sum(e, axis=0, keepdims=True)

    def chunk_body(c, carry):
        row0 = pl.multiple_of(c * HG_CHUNK, HG_CHUNK)
        _hgrn2_chunk(slab_ref, row0, lb, hgn_ref[...], masks_ref, state_ref, hg_ref)
        return carry

    lax.fori_loop(0, tile // HG_CHUNK, chunk_body, 0)

    for h in range(HG_HEADS):
        mix_ref[0, :, h * HG_DV:(h + 1) * HG_DV] = hg_ref[h].astype(BF16)

    for blk in range(tile // SW_BLOCK):
        row0 = blk * SW_BLOCK
        first_key = jnp.where(t > 0, 0, SW_BLOCK) if blk == 0 else 0
        rows = slice(row0, row0 + SW_BLOCK)
        y = _swa_block(slab_ref, row0, first_key, ctab_ref[0, rows, :], stab_ref[0, rows, :],
                       gq_ref[...], gk_ref[...], segm_ref, sinks_ref, kprev_ref, vprev_ref)
        mix_ref[0, rows, HG_WIDTH:] = y.astype(BF16)


def _const_spec(shape):
    zeros = (0,) * len(shape)
    return pl.BlockSpec(shape, lambda b, t: zeros)


def _mixer(x, ctab, stab, g1, w_in, hlb, hgn, gq, gk, sinks):
    B, S, D = x.shape
    tile = MIX_TILE
    seg_mean = jnp.asarray(np.kron(np.eye(SW_HEADS), np.full((SW_HEAD_DIM, SW_HEAD_DIM),
                                                              1.0 / SW_HEAD_DIM)), BF16)
    masks = jnp.asarray(_hgrn2_level_masks())
    tok = lambda width: pl.BlockSpec((1, tile, width), lambda b, t: (b, t, 0))
    return pl.pallas_call(
        _mixer_kernel,
        grid=(B, S // tile),
        in_specs=[pl.BlockSpec(memory_space=pltpu.SMEM),
                  tok(D), tok(LANES), tok(LANES),
                  _const_spec((1, D)), _const_spec((D, IN_WIDTH)),
                  _const_spec(hlb.shape), _const_spec((1, HG_DV)),
                  _const_spec((1, SW_WIDTH)), _const_spec((1, LANES)),
                  _const_spec((SW_WIDTH, SW_WIDTH)),
                  _const_spec((N_LEVELS, HG_CHUNK, HG_CHUNK))],
        out_specs=tok(MIX_WIDTH),
        out_shape=jax.ShapeDtypeStruct((B, S, MIX_WIDTH), BF16),
        scratch_shapes=[pltpu.VMEM((N_SLABS, tile, LANES), F32),
                        pltpu.VMEM((HG_HEADS, tile, LANES), F32),
                        pltpu.VMEM((HG_HEADS, HG_DV, HG_DK), F32),
                        pltpu.VMEM((SW_BLOCK, LANES), F32),
                        pltpu.VMEM((SW_BLOCK, LANES), F32)],
        compiler_params=pltpu.CompilerParams(
            dimension_semantics=("arbitrary", "arbitrary"),
            vmem_limit_bytes=VMEM_LIMIT_BYTES),
        name="mixer",
    )(sinks, x, ctab, stab, g1, w_in, hlb, hgn, gq, gk, seg_mean, masks)


def _tail_kernel(x_ref, mix_ref, wout_ref, g2_ref, wq_ref, gq_ref, k_ref, v_ref, wo_ref,
                 g3_ref, up_ref, down_ref, out_ref):
    h1 = x_ref[0] + _dot(mix_ref[0], wout_ref[...])

    q = _dot(_rms_norm(h1, g2_ref[...]).astype(BF16), wq_ref[...])
    heads = []
    for h in range(XA_HEADS):
        sl = slice(h * XA_HEAD_DIM, (h + 1) * XA_HEAD_DIM)
        qn = _rms_norm(q[:, sl], gq_ref[...]) * (XA_HEAD_DIM ** -0.5)
        s = _dot_nt(qn.astype(BF16), k_ref[0, :, sl])
        p = jnp.exp(s - jnp.max(s, axis=-1, keepdims=True))
        inv = 1.0 / jnp.sum(p, axis=-1, keepdims=True)
        heads.append(_dot(p.astype(BF16), v_ref[0, :, sl]) * inv)
    h2 = h1 + _dot(jnp.concatenate(heads, axis=1).astype(BF16), wo_ref[...])

    hn = _rms_norm(h2, g3_ref[...]).astype(BF16)
    acc = h2
    for c0 in range(0, up_ref.shape[1], FF_CHUNK):
        a = jnp.maximum(_dot(hn, up_ref[:, c0:c0 + FF_CHUNK]), 0.0)
        acc = acc + _dot((a * a).astype(BF16), down_ref[c0:c0 + FF_CHUNK, :])
    out_ref[0] = acc


def _tail(x, mix, w_out, g2, wq, gq, kmem, vmem, wo, g3, up, down):
    B, S, D = x.shape
    M = kmem.shape[1]
    tile = TAIL_TILE
    tok = lambda width: pl.BlockSpec((1, tile, width), lambda b, t: (b, t, 0))
    once = pl.Buffered(1)
    weight = lambda shape: pl.BlockSpec(shape, lambda b, t: (0,) * len(shape), pipeline_mode=once)
    per_batch = pl.BlockSpec((1, M, XA_WIDTH), lambda b, t: (b, 0, 0))
    return pl.pallas_call(
        _tail_kernel,
        grid=(B, S // tile),
        in_specs=[tok(D), tok(MIX_WIDTH),
                  weight(w_out.shape), weight((1, D)), weight(wq.shape), weight((1, XA_HEAD_DIM)),
                  per_batch, per_batch, weight(wo.shape), weight((1, D)),
                  weight(up.shape), weight(down.shape)],
        out_specs=tok(D),
        out_shape=jax.ShapeDtypeStruct((B, S, D), F32),
        compiler_params=pltpu.CompilerParams(
            dimension_semantics=("arbitrary", "arbitrary"),
            vmem_limit_bytes=VMEM_LIMIT_BYTES),
        name="tail",
    )(x, mix, w_out, g2, wq, gq, kmem, vmem, wo, g3, up, down)


def kernel(x, mem, positions, norm1_g, w_in, hg_lower_bounds, hg_norm_g, sw_q_norm_g, sw_k_norm_g,
           sw_sinks, w_out, norm2_g, mem_norm_g, xa_wq, xa_wkv, xa_q_norm_g, xa_k_norm_g, xa_wo,
           norm3_g, mlp_up, mlp_down):
    depth = norm1_g.shape[0]
    assert depth == 1 and x.shape[1] % TAIL_TILE == 0 and x.shape[1] % MIX_TILE == 0
    ctab, stab = _rope_tables(positions)
    h = x
    for l in range(depth):
        kmem, vmem = _mem_kv(mem, mem_norm_g[l][None], xa_wkv[l].astype(BF16), xa_k_norm_g[l][None])
        mix = _mixer(h, ctab, stab, norm1_g[l][None], w_in[l].astype(BF16), hg_lower_bounds,
                     hg_norm_g[l][None], jnp.tile(sw_q_norm_g[l], SW_HEADS)[None],
                     jnp.tile(sw_k_norm_g[l], SW_KV_HEADS)[None], sw_sinks[l])
        h = _tail(h, mix, w_out[l].astype(BF16), norm2_g[l][None], xa_wq[l].astype(BF16),
                  xa_q_norm_g[l][None], kmem, vmem, xa_wo[l].astype(BF16), norm3_g[l][None],
                  mlp_up[l].astype(BF16), mlp_down[l].astype(BF16))
    return h
```

```python
import functools

import numpy as np
import jax
import jax.numpy as jnp
from jax import lax
from jax.experimental import pallas as pl
from jax.experimental.pallas import tpu as pltpu

F32 = jnp.float32
BF16 = jnp.bfloat16

LANES = 128
SUBLANES = 8
VMEM_LIMIT_BYTES = 56 * 1024 * 1024

D_MODEL = 1024
EPS = 1e-6

HG_HEADS = 4
HG_DK = 128
HG_DV = 128
HG_WIDTH = HG_HEADS * HG_DV
HG_KEY_WIDTH = HG_HEADS * HG_DK

SW_HEADS = 8
SW_KV_HEADS = 2
SW_HEAD_DIM = 64
SW_WIDTH = SW_HEADS * SW_HEAD_DIM
SW_KV_WIDTH = SW_KV_HEADS * SW_HEAD_DIM
WINDOW = 128
SW_BLOCK = 128
ROPE_THETA = 500000.0
ROT_DIM = SW_HEAD_DIM // 4
ROT_HALF = ROT_DIM // 2

MIX_WIDTH = HG_WIDTH + SW_WIDTH
IN_WIDTH = 2 * HG_KEY_WIDTH + 2 * HG_WIDTH + SW_WIDTH + 2 * SW_KV_WIDTH

XA_HEADS = 4
XA_HEAD_DIM = 128
XA_WIDTH = XA_HEADS * XA_HEAD_DIM

N_SLABS = IN_WIDTH // LANES
SLAB_Q, SLAB_F, SLAB_I, SLAB_G = 0, 4, 8, 12
SLAB_SQ, SLAB_SK, SLAB_SV = 16, 20, 21

HG_CHUNK = 64
N_PLANES = 8
PLANE_ROWS = HG_CHUNK // N_PLANES
N_LEVELS = 7

MIX_TILE = 256
TAIL_TILE = 512
FF_CHUNK = 1024


def _mul(a, b):
    if a is None:
        return b
    if b is None:
        return a
    return a * b


def _rms_norm(x, g):
    return x * lax.rsqrt(jnp.mean(x * x, axis=-1, keepdims=True) + EPS) * g


def _dot(a, b):
    return jnp.dot(a, b, preferred_element_type=F32)


def _dot_nt(a, b):
    return lax.dot_general(a, b, (((1,), (1,)), ((), ())), preferred_element_type=F32)


def _dot_tn(a, b):
    return lax.dot_general(a, b, (((0,), (0,)), ((), ())), preferred_element_type=F32)


def _rope_table_kernel(pos_ref, invf_ref, cos_ref, sin_ref):
    ang = pos_ref[0].astype(F32) * invf_ref[...]
    cos_ref[0] = jnp.cos(ang)
    sin_ref[0] = jnp.sin(ang)


def _rope_tables(positions):
    B, S = positions.shape
    inv_freq = ROPE_THETA ** (-(jnp.arange(ROT_HALF, dtype=F32) * 2.0 / ROT_DIM))
    cos, sin = pl.pallas_call(
        _rope_table_kernel,
        grid=(B,),
        in_specs=[pl.BlockSpec((1, 1, S), lambda b: (b, 0, 0)),
                  pl.BlockSpec((ROT_HALF, 1), lambda b: (0, 0))],
        out_specs=[pl.BlockSpec((1, ROT_HALF, S), lambda b: (b, 0, 0))] * 2,
        out_shape=[jax.ShapeDtypeStruct((B, ROT_HALF, S), F32)] * 2,
        name="rope_tables",
    )(positions.reshape(B, 1, S), inv_freq.reshape(ROT_HALF, 1))
    cos = cos.transpose(0, 2, 1)
    sin = sin.transpose(0, 2, 1)
    rest = SW_HEAD_DIM - ROT_DIM
    ctab = jnp.concatenate([cos, cos, jnp.ones((B, S, rest), F32)], axis=-1)
    stab = jnp.concatenate([-sin, sin, jnp.zeros((B, S, rest), F32)], axis=-1)
    reps = LANES // SW_HEAD_DIM
    return jnp.tile(ctab, (1, 1, reps)), jnp.tile(stab, (1, 1, reps))


def _rope(t, ctab, stab):
    width = t.shape[1]
    reps = width // LANES
    c = jnp.concatenate([ctab] * reps, axis=1) if reps > 1 else ctab
    s = jnp.concatenate([stab] * reps, axis=1) if reps > 1 else stab
    lane = lax.broadcasted_iota(jnp.int32, t.shape, 1)
    first_half = (lane % SW_HEAD_DIM) < ROT_HALF
    partner = jnp.where(first_half,
                        pltpu.roll(t, width - ROT_HALF, 1),
                        pltpu.roll(t, ROT_HALF, 1))
    return t * c + partner * s


def _mem_kv_kernel(mem_ref, g_ref, wkv_ref, gk_ref, k_ref, v_ref):
    mn = _rms_norm(mem_ref[0], g_ref[...]).astype(BF16)
    kv = _dot(mn, wkv_ref[...])
    for h in range(XA_HEADS):
        sl = slice(h * XA_HEAD_DIM, (h + 1) * XA_HEAD_DIM)
        k_ref[0, :, sl] = _rms_norm(kv[:, sl], gk_ref[...]).astype(BF16)
    v_ref[0] = kv[:, XA_WIDTH:].astype(BF16)


def _mem_kv(mem, g, wkv, gk):
    B, M, D = mem.shape
    return pl.pallas_call(
        _mem_kv_kernel,
        grid=(B,),
        in_specs=[pl.BlockSpec((1, M, D), lambda b: (b, 0, 0)),
                  pl.BlockSpec((1, D), lambda b: (0, 0)),
                  pl.BlockSpec((D, 2 * XA_WIDTH), lambda b: (0, 0)),
                  pl.BlockSpec((1, XA_HEAD_DIM), lambda b: (0, 0))],
        out_specs=[pl.BlockSpec((1, M, XA_WIDTH), lambda b: (b, 0, 0))] * 2,
        out_shape=[jax.ShapeDtypeStruct((B, M, XA_WIDTH), BF16)] * 2,
        name="mem_kv",
    )(mem, g, wkv, gk)


def _hgrn2_level_masks():
    p = np.arange(HG_CHUNK)
    tok = N_PLANES * (p % PLANE_ROWS) + p // PLANE_ROWS
    ti, tj = tok[:, None], tok[None, :]
    masks = []
    h = 1
    while h < HG_CHUNK:
        masks.append((ti // (2 * h) == tj // (2 * h)) & ((ti // h) % 2 == 1) & ((tj // h) % 2 == 0))
        h *= 2
    masks.append(ti == tj)
    masks = np.stack(masks).astype(np.float32)
    assert masks.shape[0] == N_LEVELS
    assert (masks.sum(0) == (tj <= ti)).all()
    return np.tile(masks, (1, 1, 2))


def _plane_block_products(f):
    pre = {1: list(f)}
    suf = {1: [None] * N_PLANES}
    h = 1
    while h < N_PLANES:
        p, s = pre[h], suf[h]
        new_p, new_s = [], []
        for r in range(N_PLANES):
            blk = r // h
            if blk % 2 == 1:
                new_p.append(_mul(p[r], p[blk * h - 1]))
                new_s.append(s[r])
            else:
                new_p.append(p[r])
                new_s.append(_mul(s[r], p[(blk + 2) * h - 1]))
        pre[2 * h], suf[2 * h] = new_p, new_s
        h *= 2
    return pre, suf


def _group_block_products(total):
    rows = total.shape[0]
    m_idx = lax.broadcasted_iota(jnp.int32, total.shape, 0) % PLANE_ROWS
    wpre, wsuf = [None], [None]
    for d in range(1, PLANE_ROWS):
        wpre.append(_mul(wpre[-1], pltpu.roll(total, d, 0)))
        wsuf.append(_mul(wsuf[-1], pltpu.roll(total, rows - d, 0)))
    cpre, csuf = {1: None}, {1: None}
    u = 2
    while u <= PLANE_ROWS:
        off = m_idx % u
        a = jnp.ones_like(total)
        b = jnp.ones_like(total)
        for d in range(1, u):
            a = jnp.where(off == d, wpre[d], a)
            b = jnp.where(off == u - 1 - d, wsuf[d], b)
        cpre[u], csuf[u] = a, b
        u *= 2
    return cpre, csuf


def _block_diag(a, b):
    za, zb = jnp.zeros_like(a), jnp.zeros_like(b)
    return jnp.concatenate([jnp.concatenate([a, zb], axis=1),
                            jnp.concatenate([za, b], axis=1)], axis=0)


def _hgrn2_tile(slab_ref, lb, norm_g, masks_ref, state_ref, out_ref):
    tile = slab_ref.shape[1]
    n_chunks = tile // HG_CHUNK
    plane_len = tile // N_PLANES
    pair_w = 2 * HG_DK
    n_pairs = HG_HEADS // 2

    def planes(slab0):
        return [jnp.concatenate(
            [slab_ref[slab0 + h, pl.ds(r, plane_len, stride=N_PLANES), :] for h in range(HG_HEADS)],
            axis=1) for r in range(N_PLANES)]

    def chunk_rows(ps, c, lanes=slice(None)):
        return jnp.concatenate([p[c * PLANE_ROWS:(c + 1) * PLANE_ROWS, lanes] for p in ps], axis=0)

    def pair_lanes(p):
        return slice(p * pair_w, (p + 1) * pair_w)

    def heads_block_diag(x):
        return _block_diag(x[:, :HG_DK], x[:, HG_DK:])

    f = [lb + (1.0 - lb) * jax.nn.sigmoid(x) for x in planes(SLAB_F)]
    q = planes(SLAB_Q)
    k = [1.0 - x for x in f]
    v = planes(SLAB_I)

    pre, suf = _plane_block_products(f)
    pre8, suf8 = pre[N_PLANES], suf[N_PLANES]
    cpre, csuf = _group_block_products(pre8[N_PLANES - 1])

    levels = []
    hsz = 1
    while hsz < N_PLANES:
        levels.append((pre[hsz], suf[hsz]))
        hsz *= 2
    u = 1
    while u < PLANE_ROWS:
        levels.append(([_mul(x, cpre[u]) for x in pre8], [_mul(x, csuf[u]) for x in suf8]))
        u *= 2
    levels.append(([None] * N_PLANES, [None] * N_PLANES))

    scores = [[None] * n_pairs for _ in range(n_chunks)]
    for lvl, (pq, sk) in enumerate(levels):
        ql = [_mul(a, b) for a, b in zip(q, pq)]
        kl = [_mul(a, b) for a, b in zip(k, sk)]
        for c in range(n_chunks):
            for p in range(n_pairs):
                lhs = chunk_rows(ql, c, pair_lanes(p)).astype(BF16)
                rhs = heads_block_diag(chunk_rows(kl, c, pair_lanes(p)).astype(BF16))
                s = _dot_nt(lhs, rhs) * masks_ref[lvl]
                scores[c][p] = s if scores[c][p] is None else scores[c][p] + s

    q_full = [_mul(x, cpre[PLANE_ROWS]) * a for x, a in zip(pre8, q)]
    k_full = [_mul(_mul(x, csuf[PLANE_ROWS]), a) for x, a in zip(suf8, k)]
    decay = pre8[N_PLANES - 1] * cpre[PLANE_ROWS]

    states = [state_ref[h] for h in range(HG_HEADS)]
    outs = []
    for c in range(n_chunks):
        last = (c + 1) * PLANE_ROWS - 1
        o_pairs = []
        for p in range(n_pairs):
            lanes = pair_lanes(p)
            vc = chunk_rows(v, c, lanes).astype(BF16)
            st = _block_diag(states[2 * p], states[2 * p + 1]).astype(BF16)
            o = (_dot_nt(chunk_rows(q_full, c, lanes).astype(BF16), st)
                 + _dot(scores[c][p].astype(BF16), heads_block_diag(vc)))
            upd = _dot_tn(vc, chunk_rows(k_full, c, lanes).astype(BF16))
            for i in range(2):
                h = 2 * p + i
                blk = slice(i * HG_DK, (i + 1) * HG_DK)
                states[h] = (states[h] * decay[last:last + 1, h * HG_DK:(h + 1) * HG_DK]
                             + upd[blk, blk])
            o_pairs.append(o)
        outs.append(jnp.concatenate(o_pairs, axis=1))
    for h in range(HG_HEADS):
        state_ref[h] = states[h]

    gate = planes(SLAB_G)
    for r in range(N_PLANES):
        o_r = jnp.concatenate([o[r * PLANE_ROWS:(r + 1) * PLANE_ROWS] for o in outs], axis=0)
        g_r = gate[r]
        for h in range(HG_HEADS):
            lanes = slice(h * HG_DV, (h + 1) * HG_DV)
            y = _rms_norm(o_r[:, lanes], norm_g) * (g_r[:, lanes] * jax.nn.sigmoid(g_r[:, lanes]))
            out_ref[h, pl.ds(r, plane_len, stride=N_PLANES), :] = y


def _swa_block(slab_ref, row0, first_key, ctab, stab, gq, gk, seg_mean, sinks_ref,
               kprev_ref, vprev_ref):
    rows = pl.ds(row0, SW_BLOCK)
    q = jnp.concatenate([slab_ref[SLAB_SQ + s, rows, :] for s in range(SW_WIDTH // LANES)], axis=1)
    k = slab_ref[SLAB_SK, rows, :]
    v = slab_ref[SLAB_SV, rows, :]

    def head_norm(t, g, mean_mat):
        ms = _dot((t * t).astype(BF16), mean_mat)
        return t * lax.rsqrt(ms + EPS) * g

    qn = _rope(head_norm(q, gq, seg_mean[...]), ctab, stab) * (SW_HEAD_DIM ** -0.5)
    kn = _rope(head_norm(k, gk, seg_mean[:LANES, :LANES]), ctab, stab)

    kcat = jnp.concatenate([kprev_ref[...], kn], axis=0)
    vcat = jnp.concatenate([vprev_ref[...], v], axis=0)
    kprev_ref[...] = kn
    vprev_ref[...] = v

    lane = lax.broadcasted_iota(jnp.int32, kcat.shape, 1)
    low = lane < SW_HEAD_DIM
    kroll = pltpu.roll(kcat, SW_HEAD_DIM, 1)
    vroll = pltpu.roll(vcat, SW_HEAD_DIM, 1)
    zero = jnp.zeros_like(kcat)

    def paired(own, rolled, kv_head):
        src_even, src_odd = (own, rolled) if kv_head == 0 else (rolled, own)
        return jnp.concatenate([jnp.where(low, src_even, zero),
                                jnp.where(low, zero, src_odd)], axis=0).astype(BF16)

    qi = lax.broadcasted_iota(jnp.int32, (SW_BLOCK, 2 * SW_BLOCK), 0)
    kj = lax.broadcasted_iota(jnp.int32, (SW_BLOCK, 2 * SW_BLOCK), 1)
    dist = qi + SW_BLOCK - kj
    valid = (dist >= 0) & (dist < WINDOW) & (kj >= first_key)
    lane_o = lax.broadcasted_iota(jnp.int32, (SW_BLOCK, LANES), 1)
    low_o = lane_o < SW_HEAD_DIM

    outs = []
    for pair in range(SW_HEADS // 2):
        kv_head = pair // (SW_HEADS // SW_KV_HEADS // 2)
        kp = paired(kcat, kroll, kv_head)
        vp = paired(vcat, vroll, kv_head)
        s = _dot_nt(qn[:, pair * LANES:(pair + 1) * LANES].astype(BF16), kp)
        ps, inv = [], []
        for half in range(2):
            sink = sinks_ref[2 * pair + half]
            sh = jnp.where(valid, s[:, half * 2 * SW_BLOCK:(half + 1) * 2 * SW_BLOCK], -jnp.inf)
            m = jnp.maximum(jnp.max(sh, axis=-1, keepdims=True), sink)
            p = jnp.exp(sh - m)
            denom = jnp.sum(p, axis=-1, keepdims=True) + jnp.exp(sink - m)
            ps.append(p.astype(BF16))
            inv.append(1.0 / denom)
        o = _dot(jnp.concatenate(ps, axis=1), vp)
        outs.append(o * jnp.where(low_o, inv[0], inv[1]))
    return jnp.concatenate(outs, axis=1)


def _project_pieces(x_ref, g1_ref, win_ref, slab_ref):
    xn = _rms_norm(x_ref[...], g1_ref[...]).astype(BF16)
    col_step = 4 * LANES

    def piece(c0):
        c1 = min(c0 + col_step, IN_WIDTH)
        proj = _dot(xn, win_ref[:, c0:c1])
        for s in range((c1 - c0) // LANES):
            slab_ref[c0 // LANES + s] = proj[:, s * LANES:(s + 1) * LANES]

    return [functools.partial(piece, c0) for c0 in range(0, IN_WIDTH, col_step)]


def _mix_pieces(slab_ref, seq_start, lb, sinks_ref, ctab_ref, stab_ref, hgn_ref, gq_ref, gk_ref,
                segm_ref, masks_ref, mix_ref, hg_ref, state_ref, kprev_ref, vprev_ref):
    tile = mix_ref.shape[0]

    def hgrn2():
        _hgrn2_tile(slab_ref, lb, hgn_ref[...], masks_ref, state_ref, hg_ref)
        for h in range(HG_HEADS):
            mix_ref[:, h * HG_DV:(h + 1) * HG_DV] = hg_ref[h].astype(BF16)

    def swa(blk):
        row0 = blk * SW_BLOCK
        first_key = jnp.where(seq_start, SW_BLOCK, 0) if blk == 0 else 0
        rows = slice(row0, row0 + SW_BLOCK)
        y = _swa_block(slab_ref, row0, first_key, ctab_ref[rows, :], stab_ref[rows, :],
                       gq_ref[...], gk_ref[...], segm_ref, sinks_ref, kprev_ref, vprev_ref)
        mix_ref[rows, HG_WIDTH:] = y.astype(BF16)

    return [hgrn2] + [functools.partial(swa, blk) for blk in range(tile // SW_BLOCK)]


def _mixer_kernel(sinks_ref, x_ref, ctab_ref, stab_ref, g1_ref, win_ref, hlb_ref, hgn_ref,
                  gq_ref, gk_ref, segm_ref, masks_ref, mix_ref,
                  slab_a, slab_b, hg_ref, state_ref, kprev_ref, vprev_ref, *, tiles_per_seq):
    s = pl.program_id(0)
    seq_start = lax.rem(s + tiles_per_seq - 1, tiles_per_seq) == 0

    @pl.when(s == 0)
    def _():
        slab_b[...] = jnp.zeros_like(slab_b)

    @pl.when(seq_start | (s == 0))
    def _():
        state_ref[...] = jnp.zeros_like(state_ref)
        kprev_ref[...] = jnp.zeros_like(kprev_ref)
        vprev_ref[...] = jnp.zeros_like(vprev_ref)

    hlb = hlb_ref[...]
    e = jnp.exp(hlb - jnp.max(hlb, axis=0, keepdims=True))
    lb = e[0:1, :] / jnp.sum(e, axis=0, keepdims=True)

    def step(write_slab, read_slab):
        project = _project_pieces(x_ref, g1_ref, win_ref, write_slab)
        mix = _mix_pieces(read_slab, seq_start, lb, sinks_ref, ctab_ref, stab_ref, hgn_ref,
                          gq_ref, gk_ref, segm_ref, masks_ref, mix_ref, hg_ref, state_ref,
                          kprev_ref, vprev_ref)
        for i in range(max(len(project), len(mix))):
            for pieces in (project, mix):
                if i < len(pieces):
                    pieces[i]()

    @pl.when(s % 2 == 0)
    def _():
        step(slab_a, slab_b)

    @pl.when(s % 2 == 1)
    def _():
        step(slab_b, slab_a)


def _const_spec(shape):
    zeros = (0,) * len(shape)
    return pl.BlockSpec(shape, lambda s: zeros)


def _mixer(x, ctab, stab, g1, w_in, hlb, hgn, gq, gk, sinks):
    B, S, D = x.shape
    tile = MIX_TILE
    n_tiles = B * S // tile
    seg_mean = jnp.asarray(np.kron(np.eye(SW_HEADS), np.full((SW_HEAD_DIM, SW_HEAD_DIM),
                                                              1.0 / SW_HEAD_DIM)), BF16)
    masks = jnp.asarray(_hgrn2_level_masks())
    projected = lambda width: pl.BlockSpec((tile, width), lambda s: (jnp.minimum(s, n_tiles - 1), 0))
    mixed = lambda width: pl.BlockSpec((tile, width), lambda s: (jnp.maximum(s - 1, 0), 0))
    slab = pltpu.VMEM((N_SLABS, tile, LANES), F32)
    out = pl.pallas_call(
        functools.partial(_mixer_kernel, tiles_per_seq=S // tile),
        grid=(n_tiles + 1,),
        in_specs=[pl.BlockSpec(memory_space=pltpu.SMEM),
                  projected(D), mixed(LANES), mixed(LANES),
                  _const_spec((1, D)), _const_spec((D, IN_WIDTH)),
                  _const_spec(hlb.shape), _const_spec((1, HG_DV)),
                  _const_spec((1, SW_WIDTH)), _const_spec((1, LANES)),
                  _const_spec((SW_WIDTH, SW_WIDTH)),
                  _const_spec((N_LEVELS, HG_CHUNK, 2 * HG_CHUNK))],
        out_specs=mixed(MIX_WIDTH),
        out_shape=jax.ShapeDtypeStruct((B * S, MIX_WIDTH), BF16),
        scratch_shapes=[slab, slab,
                        pltpu.VMEM((HG_HEADS, tile, LANES), F32),
                        pltpu.VMEM((HG_HEADS, HG_DV, HG_DK), F32),
                        pltpu.VMEM((SW_BLOCK, LANES), F32),
                        pltpu.VMEM((SW_BLOCK, LANES), F32)],
        compiler_params=pltpu.CompilerParams(
            dimension_semantics=("arbitrary",),
            vmem_limit_bytes=VMEM_LIMIT_BYTES),
        name="mixer",
    )(sinks, x.reshape(B * S, D), ctab.reshape(B * S, LANES), stab.reshape(B * S, LANES),
      g1, w_in, hlb, hgn, gq, gk, seg_mean, masks)
    return out.reshape(B, S, MIX_WIDTH)


def _tail_kernel(x_ref, mix_ref, wout_ref, g2_ref, wq_ref, gq_ref, k_ref, v_ref, wo_ref,
                 g3_ref, up_ref, down_ref, out_ref):
    h1 = x_ref[0] + _dot(mix_ref[0], wout_ref[...])

    q = _dot(_rms_norm(h1, g2_ref[...]).astype(BF16), wq_ref[...])
    heads = []
    for h in range(XA_HEADS):
        sl = slice(h * XA_HEAD_DIM, (h + 1) * XA_HEAD_DIM)
        qn = _rms_norm(q[:, sl], gq_ref[...]) * (XA_HEAD_DIM ** -0.5)
        s = _dot_nt(qn.astype(BF16), k_ref[0, :, sl])
        p = jnp.exp(s - jnp.max(s, axis=-1, keepdims=True))
        inv = 1.0 / jnp.sum(p, axis=-1, keepdims=True)
        heads.append(_dot(p.astype(BF16), v_ref[0, :, sl]) * inv)
    h2 = h1 + _dot(jnp.concatenate(heads, axis=1).astype(BF16), wo_ref[...])

    hn = _rms_norm(h2, g3_ref[...]).astype(BF16)
    acc = h2
    for c0 in range(0, up_ref.shape[1], FF_CHUNK):
        a = jnp.maximum(_dot(hn, up_ref[:, c0:c0 + FF_CHUNK]), 0.0)
        acc = acc + _dot((a * a).astype(BF16), down_ref[c0:c0 + FF_CHUNK, :])
    out_ref[0] = acc


def _tail(x, mix, w_out, g2, wq, gq, kmem, vmem, wo, g3, up, down):
    B, S, D = x.shape
    M = kmem.shape[1]
    tile = TAIL_TILE
    tok = lambda width: pl.BlockSpec((1, tile, width), lambda b, t: (b, t, 0))
    once = pl.Buffered(1)
    weight = lambda shape: pl.BlockSpec(shape, lambda b, t: (0,) * len(shape), pipeline_mode=once)
    per_batch = pl.BlockSpec((1, M, XA_WIDTH), lambda b, t: (b, 0, 0))
    return pl.pallas_call(
        _tail_kernel,
        grid=(B, S // tile),
        in_specs=[tok(D), tok(MIX_WIDTH),
                  weight(w_out.shape), weight((1, D)), weight(wq.shape), weight((1, XA_HEAD_DIM)),
                  per_batch, per_batch, weight(wo.shape), weight((1, D)),
                  weight(up.shape), weight(down.shape)],
        out_specs=tok(D),
        out_shape=jax.ShapeDtypeStruct((B, S, D), F32),
        compiler_params=pltpu.CompilerParams(
            dimension_semantics=("arbitrary", "arbitrary"),
            vmem_limit_bytes=VMEM_LIMIT_BYTES),
        name="tail",
    )(x, mix, w_out, g2, wq, gq, kmem, vmem, wo, g3, up, down)


def kernel(x, mem, positions, norm1_g, w_in, hg_lower_bounds, hg_norm_g, sw_q_norm_g, sw_k_norm_g,
           sw_sinks, w_out, norm2_g, mem_norm_g, xa_wq, xa_wkv, xa_q_norm_g, xa_k_norm_g, xa_wo,
           norm3_g, mlp_up, mlp_down):
    depth = norm1_g.shape[0]
    assert depth == 1 and x.shape[1] % TAIL_TILE == 0 and x.shape[1] % MIX_TILE == 0
    ctab, stab = _rope_tables(positions)
    h = x
    for l in range(depth):
        kmem, vmem = _mem_kv(mem, mem_norm_g[l][None], xa_wkv[l].astype(BF16), xa_k_norm_g[l][None])
        mix = _mixer(h, ctab, stab, norm1_g[l][None], w_in[l].astype(BF16), hg_lower_bounds,
                     hg_norm_g[l][None], jnp.tile(sw_q_norm_g[l], SW_HEADS)[None],
                     jnp.tile(sw_k_norm_g[l], SW_KV_HEADS)[None], sw_sinks[l])
        h = _tail(h, mix, w_out[l].astype(BF16), norm2_g[l][None], xa_wq[l].astype(BF16),
                  xa_q_norm_g[l][None], kmem, vmem, xa_wo[l].astype(BF16), norm3_g[l][None],
                  mlp_up[l].astype(BF16), mlp_down[l].astype(BF16))
    return h
```

```python
import functools

import numpy as np
import jax
import jax.numpy as jnp
from jax import lax
from jax.experimental import pallas as pl
from jax.experimental.pallas import tpu as pltpu

F32 = jnp.float32
BF16 = jnp.bfloat16

LANES = 128
SUBLANES = 8
VMEM_LIMIT_BYTES = 56 * 1024 * 1024

D_MODEL = 1024
EPS = 1e-6

HG_HEADS = 4
HG_DK = 128
HG_DV = 128
HG_WIDTH = HG_HEADS * HG_DV
HG_KEY_WIDTH = HG_HEADS * HG_DK

SW_HEADS = 8
SW_KV_HEADS = 2
SW_HEAD_DIM = 64
SW_WIDTH = SW_HEADS * SW_HEAD_DIM
SW_KV_WIDTH = SW_KV_HEADS * SW_HEAD_DIM
WINDOW = 128
SW_BLOCK = 128
ROPE_THETA = 500000.0
ROT_DIM = SW_HEAD_DIM // 4
ROT_HALF = ROT_DIM // 2

MIX_WIDTH = HG_WIDTH + SW_WIDTH
IN_WIDTH = 2 * HG_KEY_WIDTH + 2 * HG_WIDTH + SW_WIDTH + 2 * SW_KV_WIDTH

XA_HEADS = 4
XA_HEAD_DIM = 128
XA_WIDTH = XA_HEADS * XA_HEAD_DIM

N_SLABS = IN_WIDTH // LANES
SLAB_Q, SLAB_F, SLAB_I, SLAB_G = 0, 4, 8, 12
SLAB_SQ, SLAB_SK, SLAB_SV = 16, 20, 21

HG_CHUNK = 64
N_PLANES = 8
PLANE_ROWS = HG_CHUNK // N_PLANES
N_LEVELS = 7

MIX_TILE = 256
PROJECT_AFTER_MIX_STAGE = (-1, 0, 4, 5, 7, 8)
TAIL_TILE = 512
FF_CHUNK = 1024


def _mul(a, b):
    if a is None:
        return b
    if b is None:
        return a
    return a * b


def _rms_norm(x, g):
    return x * lax.rsqrt(jnp.mean(x * x, axis=-1, keepdims=True) + EPS) * g


def _dot(a, b):
    return jnp.dot(a, b, preferred_element_type=F32)


def _dot_nt(a, b):
    return lax.dot_general(a, b, (((1,), (1,)), ((), ())), preferred_element_type=F32)


def _dot_tn(a, b):
    return lax.dot_general(a, b, (((0,), (0,)), ((), ())), preferred_element_type=F32)


def _rope_table_kernel(pos_ref, invf_ref, cos_ref, sin_ref):
    ang = pos_ref[0].astype(F32) * invf_ref[...]
    cos_ref[0] = jnp.cos(ang)
    sin_ref[0] = jnp.sin(ang)


def _rope_tables(positions):
    B, S = positions.shape
    inv_freq = ROPE_THETA ** (-(jnp.arange(ROT_HALF, dtype=F32) * 2.0 / ROT_DIM))
    cos, sin = pl.pallas_call(
        _rope_table_kernel,
        grid=(B,),
        in_specs=[pl.BlockSpec((1, 1, S), lambda b: (b, 0, 0)),
                  pl.BlockSpec((ROT_HALF, 1), lambda b: (0, 0))],
        out_specs=[pl.BlockSpec((1, ROT_HALF, S), lambda b: (b, 0, 0))] * 2,
        out_shape=[jax.ShapeDtypeStruct((B, ROT_HALF, S), F32)] * 2,
        name="rope_tables",
    )(positions.reshape(B, 1, S), inv_freq.reshape(ROT_HALF, 1))
    cos = cos.transpose(0, 2, 1)
    sin = sin.transpose(0, 2, 1)
    rest = SW_HEAD_DIM - ROT_DIM
    ctab = jnp.concatenate([cos, cos, jnp.ones((B, S, rest), F32)], axis=-1)
    stab = jnp.concatenate([-sin, sin, jnp.zeros((B, S, rest), F32)], axis=-1)
    reps = LANES // SW_HEAD_DIM
    return jnp.tile(ctab, (1, 1, reps)), jnp.tile(stab, (1, 1, reps))


def _rope(t, ctab, stab):
    width = t.shape[1]
    reps = width // LANES
    c = jnp.concatenate([ctab] * reps, axis=1) if reps > 1 else ctab
    s = jnp.concatenate([stab] * reps, axis=1) if reps > 1 else stab
    lane = lax.broadcasted_iota(jnp.int32, t.shape, 1)
    first_half = (lane % SW_HEAD_DIM) < ROT_HALF
    partner = jnp.where(first_half,
                        pltpu.roll(t, width - ROT_HALF, 1),
                        pltpu.roll(t, ROT_HALF, 1))
    return t * c + partner * s


def _mem_kv_kernel(mem_ref, g_ref, wkv_ref, gk_ref, k_ref, v_ref):
    mn = _rms_norm(mem_ref[0], g_ref[...]).astype(BF16)
    kv = _dot(mn, wkv_ref[...])
    for h in range(XA_HEADS):
        sl = slice(h * XA_HEAD_DIM, (h + 1) * XA_HEAD_DIM)
        k_ref[0, :, sl] = _rms_norm(kv[:, sl], gk_ref[...]).astype(BF16)
    v_ref[0] = kv[:, XA_WIDTH:].astype(BF16)


def _mem_kv(mem, g, wkv, gk):
    B, M, D = mem.shape
    return pl.pallas_call(
        _mem_kv_kernel,
        grid=(B,),
        in_specs=[pl.BlockSpec((1, M, D), lambda b: (b, 0, 0)),
                  pl.BlockSpec((1, D), lambda b: (0, 0)),
                  pl.BlockSpec((D, 2 * XA_WIDTH), lambda b: (0, 0)),
                  pl.BlockSpec((1, XA_HEAD_DIM), lambda b: (0, 0))],
        out_specs=[pl.BlockSpec((1, M, XA_WIDTH), lambda b: (b, 0, 0))] * 2,
        out_shape=[jax.ShapeDtypeStruct((B, M, XA_WIDTH), BF16)] * 2,
        name="mem_kv",
    )(mem, g, wkv, gk)


def _hgrn2_level_masks():
    p = np.arange(HG_CHUNK)
    tok = N_PLANES * (p % PLANE_ROWS) + p // PLANE_ROWS
    ti, tj = tok[:, None], tok[None, :]
    masks = []
    h = 1
    while h < HG_CHUNK:
        masks.append((ti // (2 * h) == tj // (2 * h)) & ((ti // h) % 2 == 1) & ((tj // h) % 2 == 0))
        h *= 2
    masks.append(ti == tj)
    masks = np.stack(masks).astype(np.float32)
    assert masks.shape[0] == N_LEVELS
    assert (masks.sum(0) == (tj <= ti)).all()
    return np.tile(masks, (1, 1, 2))


def _plane_block_products(f):
    pre = {1: list(f)}
    suf = {1: [None] * N_PLANES}
    h = 1
    while h < N_PLANES:
        p, s = pre[h], suf[h]
        new_p, new_s = [], []
        for r in range(N_PLANES):
            blk = r // h
            if blk % 2 == 1:
                new_p.append(_mul(p[r], p[blk * h - 1]))
                new_s.append(s[r])
            else:
                new_p.append(p[r])
                new_s.append(_mul(s[r], p[(blk + 2) * h - 1]))
        pre[2 * h], suf[2 * h] = new_p, new_s
        h *= 2
    return pre, suf


def _group_block_products(total):
    rows = total.shape[0]
    m_idx = lax.broadcasted_iota(jnp.int32, total.shape, 0) % PLANE_ROWS
    wpre, wsuf = [None], [None]
    for d in range(1, PLANE_ROWS):
        wpre.append(_mul(wpre[-1], pltpu.roll(total, d, 0)))
        wsuf.append(_mul(wsuf[-1], pltpu.roll(total, rows - d, 0)))
    cpre, csuf = {1: None}, {1: None}
    u = 2
    while u <= PLANE_ROWS:
        off = m_idx % u
        a = jnp.ones_like(total)
        b = jnp.ones_like(total)
        for d in range(1, u):
            a = jnp.where(off == d, wpre[d], a)
            b = jnp.where(off == u - 1 - d, wsuf[d], b)
        cpre[u], csuf[u] = a, b
        u *= 2
    return cpre, csuf


def _block_diag(a, b):
    za, zb = jnp.zeros_like(a), jnp.zeros_like(b)
    return jnp.concatenate([jnp.concatenate([a, zb], axis=1),
                            jnp.concatenate([za, b], axis=1)], axis=0)


def _hgrn2_tile(slab_ref, lb, norm_g, masks_ref, state_ref, out_ref):
    tile = slab_ref.shape[1]
    n_chunks = tile // HG_CHUNK
    plane_len = tile // N_PLANES
    pair_w = 2 * HG_DK
    n_pairs = HG_HEADS // 2

    def planes(slab0):
        return [jnp.concatenate(
            [slab_ref[slab0 + h, pl.ds(r, plane_len, stride=N_PLANES), :] for h in range(HG_HEADS)],
            axis=1) for r in range(N_PLANES)]

    def chunk_rows(ps, c, lanes=slice(None)):
        return jnp.concatenate([p[c * PLANE_ROWS:(c + 1) * PLANE_ROWS, lanes] for p in ps], axis=0)

    def pair_lanes(p):
        return slice(p * pair_w, (p + 1) * pair_w)

    def heads_block_diag(x):
        return _block_diag(x[:, :HG_DK], x[:, HG_DK:])

    f = [lb + (1.0 - lb) * jax.nn.sigmoid(x) for x in planes(SLAB_F)]
    q = planes(SLAB_Q)
    k = [1.0 - x for x in f]
    v = planes(SLAB_I)

    pre, suf = _plane_block_products(f)
    pre8, suf8 = pre[N_PLANES], suf[N_PLANES]
    cpre, csuf = _group_block_products(pre8[N_PLANES - 1])

    levels = []
    hsz = 1
    while hsz < N_PLANES:
        levels.append((pre[hsz], suf[hsz]))
        hsz *= 2
    u = 1
    while u < PLANE_ROWS:
        levels.append(([_mul(x, cpre[u]) for x in pre8], [_mul(x, csuf[u]) for x in suf8]))
        u *= 2
    levels.append(([None] * N_PLANES, [None] * N_PLANES))

    scores = [[None] * n_pairs for _ in range(n_chunks)]
    for lvl, (pq, sk) in enumerate(levels):
        ql = [_mul(a, b) for a, b in zip(q, pq)]
        kl = [_mul(a, b) for a, b in zip(k, sk)]
        for c in range(n_chunks):
            for p in range(n_pairs):
                lhs = chunk_rows(ql, c, pair_lanes(p)).astype(BF16)
                rhs = heads_block_diag(chunk_rows(kl, c, pair_lanes(p)).astype(BF16))
                s = _dot_nt(lhs, rhs) * masks_ref[lvl]
                scores[c][p] = s if scores[c][p] is None else scores[c][p] + s
        if lvl % 2 == 0:
            yield

    q_full = [_mul(x, cpre[PLANE_ROWS]) * a for x, a in zip(pre8, q)]
    k_full = [_mul(_mul(x, csuf[PLANE_ROWS]), a) for x, a in zip(suf8, k)]
    decay = pre8[N_PLANES - 1] * cpre[PLANE_ROWS]

    states = [state_ref[h] for h in range(HG_HEADS)]
    outs = []
    for c in range(n_chunks):
        last = (c + 1) * PLANE_ROWS - 1
        o_pairs = []
        for p in range(n_pairs):
            lanes = pair_lanes(p)
            vc = chunk_rows(v, c, lanes).astype(BF16)
            st = _block_diag(states[2 * p], states[2 * p + 1]).astype(BF16)
            o = (_dot_nt(chunk_rows(q_full, c, lanes).astype(BF16), st)
                 + _dot(scores[c][p].astype(BF16), heads_block_diag(vc)))
            upd = _dot_tn(vc, chunk_rows(k_full, c, lanes).astype(BF16))
            for i in range(2):
                h = 2 * p + i
                blk = slice(i * HG_DK, (i + 1) * HG_DK)
                states[h] = (states[h] * decay[last:last + 1, h * HG_DK:(h + 1) * HG_DK]
                             + upd[blk, blk])
            o_pairs.append(o)
        outs.append(jnp.concatenate(o_pairs, axis=1))
    for h in range(HG_HEADS):
        state_ref[h] = states[h]
    yield

    gate = planes(SLAB_G)
    for r in range(N_PLANES):
        o_r = jnp.concatenate([o[r * PLANE_ROWS:(r + 1) * PLANE_ROWS] for o in outs], axis=0)
        g_r = gate[r]
        for h in range(HG_HEADS):
            lanes = slice(h * HG_DV, (h + 1) * HG_DV)
            y = _rms_norm(o_r[:, lanes], norm_g) * (g_r[:, lanes] * jax.nn.sigmoid(g_r[:, lanes]))
            out_ref[h, pl.ds(r, plane_len, stride=N_PLANES), :] = y


def _swa_tile(slab_ref, seq_start, ctab, stab, gq, gk, seg_mean, sinks_ref, kpad_ref, vpad_ref,
              out_ref):
    tile = slab_ref.shape[1]
    n_blocks = tile // SW_BLOCK
    n_pairs = SW_HEADS // 2
    q = jnp.concatenate([slab_ref[SLAB_SQ + s] for s in range(SW_WIDTH // LANES)], axis=1)
    k = slab_ref[SLAB_SK]
    v = slab_ref[SLAB_SV]

    def head_norm(t, g, mean_mat):
        ms = _dot((t * t).astype(BF16), mean_mat)
        return t * lax.rsqrt(ms + EPS) * g

    qn = _rope(head_norm(q, gq, seg_mean[...]), ctab, stab) * (SW_HEAD_DIM ** -0.5)
    kn = _rope(head_norm(k, gk, seg_mean[:LANES, :LANES]), ctab, stab)

    lane = lax.broadcasted_iota(jnp.int32, (tile, LANES), 1)
    low = lane < SW_HEAD_DIM

    def padded(t):
        rolled = pltpu.roll(t, SW_HEAD_DIM, 1)
        zero = jnp.zeros_like(t)
        out = []
        for kv_head in range(SW_KV_HEADS):
            src_even, src_odd = (t, rolled) if kv_head == 0 else (rolled, t)
            out.append(jnp.where(low, src_even, zero).astype(BF16))
            out.append(jnp.where(low, zero, src_odd).astype(BF16))
        return out

    kpads, vpads = padded(kn), padded(v)
    qb = qn.astype(BF16)
    yield

    def block_operand(pads, carry_ref, blk, kv):
        parts = []
        for i in (kv, kv + 1):
            prev = carry_ref[i] if blk == 0 else pads[i][(blk - 1) * SW_BLOCK:blk * SW_BLOCK]
            parts += [prev, pads[i][blk * SW_BLOCK:(blk + 1) * SW_BLOCK]]
        return jnp.concatenate(parts, axis=0)

    units = [(blk, pair) for blk in range(n_blocks) for pair in range(n_pairs)]
    heads = [(blk, pair, half) for blk, pair in units for half in range(2)]
    kv_of = lambda pair: 2 * (pair // (n_pairs // SW_KV_HEADS))

    scores = {}
    for blk, pair in units:
        rows = slice(blk * SW_BLOCK, (blk + 1) * SW_BLOCK)
        scores[blk, pair] = _dot_nt(qb[rows, pair * LANES:(pair + 1) * LANES],
                                    block_operand(kpads, kpad_ref, blk, kv_of(pair)))

    yield
    qi = lax.broadcasted_iota(jnp.int32, (SW_BLOCK, SW_BLOCK), 0)
    kj = lax.broadcasted_iota(jnp.int32, (SW_BLOCK, SW_BLOCK), 1)
    from_prev = kj > qi
    start_bias = jnp.where(seq_start, -jnp.inf, 0.0)

    merged, sink = {}, {}
    for blk, pair, half in heads:
        s = scores[blk, pair]
        s_prev = s[:, (2 * half) * SW_BLOCK:(2 * half + 1) * SW_BLOCK]
        s_cur = s[:, (2 * half + 1) * SW_BLOCK:(2 * half + 2) * SW_BLOCK]
        if blk == 0:
            s_prev = s_prev + start_bias
        merged[blk, pair, half] = jnp.where(from_prev, s_prev, s_cur)
        sink[blk, pair, half] = sinks_ref[2 * pair + half]
    top = {u: jnp.maximum(jnp.max(merged[u], axis=-1, keepdims=True), sink[u]) for u in heads}
    prob = {u: jnp.exp(merged[u] - top[u]) for u in heads}
    inv = {u: 1.0 / (jnp.sum(prob[u], axis=-1, keepdims=True) + jnp.exp(sink[u] - top[u]))
           for u in heads}

    yield
    low_o = lax.broadcasted_iota(jnp.int32, (SW_BLOCK, LANES), 1) < SW_HEAD_DIM
    for blk, pair in units:
        ps = []
        for half in range(2):
            p = prob[blk, pair, half]
            zero = jnp.zeros_like(p)
            ps += [jnp.where(from_prev, p, zero).astype(BF16),
                   jnp.where(from_prev, zero, p).astype(BF16)]
        o = _dot(jnp.concatenate(ps, axis=1), block_operand(vpads, vpad_ref, blk, kv_of(pair)))
        o = o * jnp.where(low_o, inv[blk, pair, 0], inv[blk, pair, 1])
        out_ref[blk * SW_BLOCK:(blk + 1) * SW_BLOCK, pair * LANES:(pair + 1) * LANES] = (
            o.astype(out_ref.dtype))

    last = slice((n_blocks - 1) * SW_BLOCK, n_blocks * SW_BLOCK)
    for i in range(2 * SW_KV_HEADS):
        kpad_ref[i] = kpads[i][last]
        vpad_ref[i] = vpads[i][last]


def _project_pieces(x_ref, g1_ref, win_ref, slab_ref):
    xn = _rms_norm(x_ref[...], g1_ref[...]).astype(BF16)
    col_step = 4 * LANES

    def piece(c0):
        c1 = min(c0 + col_step, IN_WIDTH)
        proj = _dot(xn, win_ref[:, c0:c1])
        for s in range((c1 - c0) // LANES):
            slab_ref[c0 // LANES + s] = proj[:, s * LANES:(s + 1) * LANES]

    return [functools.partial(piece, c0) for c0 in range(0, IN_WIDTH, col_step)]


def _mix_pieces(slab_ref, seq_start, lb, sinks_ref, ctab_ref, stab_ref, hgn_ref, gq_ref, gk_ref,
                segm_ref, masks_ref, mix_ref, hg_ref, state_ref, kpad_ref, vpad_ref):
    yield from _hgrn2_tile(slab_ref, lb, hgn_ref[...], masks_ref, state_ref, hg_ref)
    for h in range(HG_HEADS):
        mix_ref[:, h * HG_DV:(h + 1) * HG_DV] = hg_ref[h].astype(BF16)
    yield
    yield from _swa_tile(slab_ref, seq_start, ctab_ref[...], stab_ref[...], gq_ref[...],
                         gk_ref[...], segm_ref, sinks_ref, kpad_ref, vpad_ref,
                         mix_ref.at[:, pl.ds(HG_WIDTH, SW_WIDTH)])


def _mixer_kernel(sinks_ref, x_ref, ctab_ref, stab_ref, g1_ref, win_ref, hlb_ref, hgn_ref,
                  gq_ref, gk_ref, segm_ref, masks_ref, mix_ref,
                  slab_a, slab_b, hg_ref, state_ref, kpad_ref, vpad_ref, *, tiles_per_seq):
    s = pl.program_id(0)
    seq_start = lax.rem(s + tiles_per_seq - 1, tiles_per_seq) == 0

    @pl.when(s == 0)
    def _():
        slab_b[...] = jnp.zeros_like(slab_b)

    @pl.when(seq_start | (s == 0))
    def _():
        state_ref[...] = jnp.zeros_like(state_ref)
        kpad_ref[...] = jnp.zeros_like(kpad_ref)
        vpad_ref[...] = jnp.zeros_like(vpad_ref)

    hlb = hlb_ref[...]
    e = jnp.exp(hlb - jnp.max(hlb, axis=0, keepdims=True))
    lb = e[0:1, :] / jnp.sum(e, axis=0, keepdims=True)

    def step(write_slab, read_slab):
        project = _project_pieces(x_ref, g1_ref, win_ref, write_slab)
        mix = _mix_pieces(read_slab, seq_start, lb, sinks_ref, ctab_ref, stab_ref, hgn_ref,
                          gq_ref, gk_ref, segm_ref, masks_ref, mix_ref, hg_ref, state_ref,
                          kpad_ref, vpad_ref)
        assert len(project) == len(PROJECT_AFTER_MIX_STAGE)

        def project_after(stage):
            for piece, after in zip(project, PROJECT_AFTER_MIX_STAGE):
                if after == stage:
                    piece()

        project_after(-1)
        for stage, _ in enumerate(mix):
            project_after(stage)

    @pl.when(s % 2 == 0)
    def _():
        step(slab_a, slab_b)

    @pl.when(s % 2 == 1)
    def _():
        step(slab_b, slab_a)


def _const_spec(shape):
    zeros = (0,) * len(shape)
    return pl.BlockSpec(shape, lambda s: zeros)


def _mixer(x, ctab, stab, g1, w_in, hlb, hgn, gq, gk, sinks):
    B, S, D = x.shape
    tile = MIX_TILE
    n_tiles = B * S // tile
    seg_mean = jnp.asarray(np.kron(np.eye(SW_HEADS), np.full((SW_HEAD_DIM, SW_HEAD_DIM),
                                                              1.0 / SW_HEAD_DIM)), BF16)
    masks = jnp.asarray(_hgrn2_level_masks())
    projected = lambda width: pl.BlockSpec((tile, width), lambda s: (jnp.minimum(s, n_tiles - 1), 0))
    mixed = lambda width: pl.BlockSpec((tile, width), lambda s: (jnp.maximum(s - 1, 0), 0))
    slab = pltpu.VMEM((N_SLABS, tile, LANES), F32)
    out = pl.pallas_call(
        functools.partial(_mixer_kernel, tiles_per_seq=S // tile),
        grid=(n_tiles + 1,),
        in_specs=[pl.BlockSpec(memory_space=pltpu.SMEM),
                  projected(D), mixed(LANES), mixed(LANES),
                  _const_spec((1, D)), _const_spec((D, IN_WIDTH)),
                  _const_spec(hlb.shape), _const_spec((1, HG_DV)),
                  _const_spec((1, SW_WIDTH)), _const_spec((1, LANES)),
                  _const_spec((SW_WIDTH, SW_WIDTH)),
                  _const_spec((N_LEVELS, HG_CHUNK, 2 * HG_CHUNK))],
        out_specs=mixed(MIX_WIDTH),
        out_shape=jax.ShapeDtypeStruct((B * S, MIX_WIDTH), BF16),
        scratch_shapes=[slab, slab,
                        pltpu.VMEM((HG_HEADS, tile, LANES), F32),
                        pltpu.VMEM((HG_HEADS, HG_DV, HG_DK), F32),
                        pltpu.VMEM((2 * SW_KV_HEADS, SW_BLOCK, LANES), BF16),
                        pltpu.VMEM((2 * SW_KV_HEADS, SW_BLOCK, LANES), BF16)],
        compiler_params=pltpu.CompilerParams(
            dimension_semantics=("arbitrary",),
            vmem_limit_bytes=VMEM_LIMIT_BYTES),
        name="mixer",
    )(sinks, x.reshape(B * S, D), ctab.reshape(B * S, LANES), stab.reshape(B * S, LANES),
      g1, w_in, hlb, hgn, gq, gk, seg_mean, masks)
    return out.reshape(B, S, MIX_WIDTH)


def _tail_kernel(x_ref, mix_ref, wout_ref, g2_ref, wq_ref, gq_ref, k_ref, v_ref, wo_ref,
                 g3_ref, up_ref, down_ref, out_ref):
    h1 = x_ref[0] + _dot(mix_ref[0], wout_ref[...])

    q = _dot(_rms_norm(h1, g2_ref[...]).astype(BF16), wq_ref[...])
    heads = []
    for h in range(XA_HEADS):
        sl = slice(h * XA_HEAD_DIM, (h + 1) * XA_HEAD_DIM)
        qn = _rms_norm(q[:, sl], gq_ref[...]) * (XA_HEAD_DIM ** -0.5)
        s = _dot_nt(qn.astype(BF16), k_ref[0, :, sl])
        p = jnp.exp(s - jnp.max(s, axis=-1, keepdims=True))
        inv = 1.0 / jnp.sum(p, axis=-1, keepdims=True)
        heads.append(_dot(p.astype(BF16), v_ref[0, :, sl]) * inv)
    h2 = h1 + _dot(jnp.concatenate(heads, axis=1).astype(BF16), wo_ref[...])

    hn = _rms_norm(h2, g3_ref[...]).astype(BF16)
    acc = h2
    for c0 in range(0, up_ref.shape[1], FF_CHUNK):
        a = jnp.maximum(_dot(hn, up_ref[:, c0:c0 + FF_CHUNK]), 0.0)
        acc = acc + _dot((a * a).astype(BF16), down_ref[c0:c0 + FF_CHUNK, :])
    out_ref[0] = acc


def _tail(x, mix, w_out, g2, wq, gq, kmem, vmem, wo, g3, up, down):
    B, S, D = x.shape
    M = kmem.shape[1]
    tile = TAIL_TILE
    tok = lambda width: pl.BlockSpec((1, tile, width), lambda b, t: (b, t, 0))
    once = pl.Buffered(1)
    weight = lambda shape: pl.BlockSpec(shape, lambda b, t: (0,) * len(shape), pipeline_mode=once)
    per_batch = pl.BlockSpec((1, M, XA_WIDTH), lambda b, t: (b, 0, 0))
    return pl.pallas_call(
        _tail_kernel,
        grid=(B, S // tile),
        in_specs=[tok(D), tok(MIX_WIDTH),
                  weight(w_out.shape), weight((1, D)), weight(wq.shape), weight((1, XA_HEAD_DIM)),
                  per_batch, per_batch, weight(wo.shape), weight((1, D)),
                  weight(up.shape), weight(down.shape)],
        out_specs=tok(D),
        out_shape=jax.ShapeDtypeStruct((B, S, D), F32),
        compiler_params=pltpu.CompilerParams(
            dimension_semantics=("arbitrary", "arbitrary"),
            vmem_limit_bytes=VMEM_LIMIT_BYTES),
        name="tail",
    )(x, mix, w_out, g2, wq, gq, kmem, vmem, wo, g3, up, down)


def kernel(x, mem, positions, norm1_g, w_in, hg_lower_bounds, hg_norm_g, sw_q_norm_g, sw_k_norm_g,
           sw_sinks, w_out, norm2_g, mem_norm_g, xa_wq, xa_wkv, xa_q_norm_g, xa_k_norm_g, xa_wo,
           norm3_g, mlp_up, mlp_down):
    depth = norm1_g.shape[0]
    assert depth == 1 and x.shape[1] % TAIL_TILE == 0 and x.shape[1] % MIX_TILE == 0
    ctab, stab = _rope_tables(positions)
    h = x
    for l in range(depth):
        kmem, vmem = _mem_kv(mem, mem_norm_g[l][None], xa_wkv[l].astype(BF16), xa_k_norm_g[l][None])
        mix = _mixer(h, ctab, stab, norm1_g[l][None], w_in[l].astype(BF16), hg_lower_bounds,
                     hg_norm_g[l][None], jnp.tile(sw_q_norm_g[l], SW_HEADS)[None],
                     jnp.tile(sw_k_norm_g[l], SW_KV_HEADS)[None], sw_sinks[l])
        h = _tail(h, mix, w_out[l].astype(BF16), norm2_g[l][None], xa_wq[l].astype(BF16),
                  xa_q_norm_g[l][None], kmem, vmem, xa_wo[l].astype(BF16), norm3_g[l][None],
                  mlp_up[l].astype(BF16), mlp_down[l].astype(BF16))
    return h
```

```python
import functools

import numpy as np
import jax
import jax.numpy as jnp
from jax import lax
from jax.experimental import pallas as pl
from jax.experimental.pallas import tpu as pltpu

F32 = jnp.float32
BF16 = jnp.bfloat16

LANES = 128
SUBLANES = 8
VMEM_LIMIT_BYTES = 56 * 1024 * 1024

D_MODEL = 1024
EPS = 1e-6

HG_HEADS = 4
HG_DK = 128
HG_DV = 128
HG_WIDTH = HG_HEADS * HG_DV
HG_KEY_WIDTH = HG_HEADS * HG_DK

SW_HEADS = 8
SW_KV_HEADS = 2
SW_HEAD_DIM = 64
SW_WIDTH = SW_HEADS * SW_HEAD_DIM
SW_KV_WIDTH = SW_KV_HEADS * SW_HEAD_DIM
WINDOW = 128
SW_BLOCK = 128
ROPE_THETA = 500000.0
ROT_DIM = SW_HEAD_DIM // 4
ROT_HALF = ROT_DIM // 2

MIX_WIDTH = HG_WIDTH + SW_WIDTH
IN_WIDTH = 2 * HG_KEY_WIDTH + 2 * HG_WIDTH + SW_WIDTH + 2 * SW_KV_WIDTH

XA_HEADS = 4
XA_HEAD_DIM = 128
XA_WIDTH = XA_HEADS * XA_HEAD_DIM

N_SLABS = IN_WIDTH // LANES
SLAB_Q, SLAB_F, SLAB_I, SLAB_G = 0, 4, 8, 12
SLAB_SQ, SLAB_SK, SLAB_SV = 16, 20, 21

HG_CHUNK = 64
N_PLANES = 8
PLANE_ROWS = HG_CHUNK // N_PLANES
N_LEVELS = 7

MIX_TILE = 256
PROJECT_AFTER_MIX_STAGE = (-1, 0, 4, 5, 7, 8)
TAIL_TILE = 512
FF_CHUNK = 1024


def _mul(a, b):
    if a is None:
        return b
    if b is None:
        return a
    return a * b


def _rms_norm(x, g):
    return x * lax.rsqrt(jnp.mean(x * x, axis=-1, keepdims=True) + EPS) * g


def _dot(a, b):
    return jnp.dot(a, b, preferred_element_type=F32)


def _dot_nt(a, b):
    return lax.dot_general(a, b, (((1,), (1,)), ((), ())), preferred_element_type=F32)


def _dot_tn(a, b):
    return lax.dot_general(a, b, (((0,), (0,)), ((), ())), preferred_element_type=F32)


BF16_SPLIT_PARTS = 3


def _rope_table_kernel(pos_ref, invf_ref, ecos_ref, esin_ref, base_ref, ctab_ref, stab_ref):
    ang = pos_ref[0].astype(F32) * invf_ref[...]

    def spread(t, e_ref):
        parts, rest = [], t
        for _ in range(BF16_SPLIT_PARTS):
            piece = rest.astype(BF16).astype(F32)
            parts.append(piece)
            rest = rest - piece
        parts.append(jnp.zeros_like(t))
        return _dot_tn(jnp.concatenate(parts, axis=0).astype(BF16), e_ref[...])

    ctab_ref[0] = spread(jnp.cos(ang), ecos_ref) + base_ref[...]
    stab_ref[0] = spread(jnp.sin(ang), esin_ref)


def _rope_tables(positions):
    B, S = positions.shape
    inv_freq = ROPE_THETA ** (-(jnp.arange(ROT_HALF, dtype=F32) * 2.0 / ROT_DIM))
    dim = np.arange(LANES) % SW_HEAD_DIM
    freq = np.arange(ROT_HALF)[:, None]
    first, second = dim[None, :] == freq, dim[None, :] == freq + ROT_HALF
    pad = np.zeros((ROT_HALF, LANES))
    stack = lambda e: jnp.asarray(np.concatenate([e] * BF16_SPLIT_PARTS + [pad]), BF16)
    ecos = stack(first * 1.0 + second * 1.0)
    esin = stack(second * 1.0 - first * 1.0)
    base = jnp.asarray((dim >= ROT_DIM)[None, :], F32)
    rows = (BF16_SPLIT_PARTS + 1) * ROT_HALF
    const = lambda shape: pl.BlockSpec(shape, lambda b: (0, 0))
    return pl.pallas_call(
        _rope_table_kernel,
        grid=(B,),
        in_specs=[pl.BlockSpec((1, 1, S), lambda b: (b, 0, 0)), const((ROT_HALF, 1)),
                  const((rows, LANES)), const((rows, LANES)), const((1, LANES))],
        out_specs=[pl.BlockSpec((1, S, LANES), lambda b: (b, 0, 0))] * 2,
        out_shape=[jax.ShapeDtypeStruct((B, S, LANES), F32)] * 2,
        name="rope_tables",
    )(positions.reshape(B, 1, S), inv_freq.reshape(ROT_HALF, 1), ecos, esin, base)


def _rope(t, ctab, stab):
    width = t.shape[1]
    reps = width // LANES
    c = jnp.concatenate([ctab] * reps, axis=1) if reps > 1 else ctab
    s = jnp.concatenate([stab] * reps, axis=1) if reps > 1 else stab
    lane = lax.broadcasted_iota(jnp.int32, t.shape, 1)
    first_half = (lane % SW_HEAD_DIM) < ROT_HALF
    partner = jnp.where(first_half,
                        pltpu.roll(t, width - ROT_HALF, 1),
                        pltpu.roll(t, ROT_HALF, 1))
    return t * c + partner * s


def _mem_kv_kernel(mem_ref, g_ref, wkv_ref, gk_ref, k_ref, v_ref):
    mn = _rms_norm(mem_ref[0], g_ref[...]).astype(BF16)
    kv = _dot(mn, wkv_ref[...])
    for h in range(XA_HEADS):
        sl = slice(h * XA_HEAD_DIM, (h + 1) * XA_HEAD_DIM)
        k_ref[0, :, sl] = _rms_norm(kv[:, sl], gk_ref[...]).astype(BF16)
    v_ref[0] = kv[:, XA_WIDTH:].astype(BF16)


def _mem_kv(mem, g, wkv, gk):
    B, M, D = mem.shape
    return pl.pallas_call(
        _mem_kv_kernel,
        grid=(B,),
        in_specs=[pl.BlockSpec((1, M, D), lambda b: (b, 0, 0)),
                  pl.BlockSpec((1, D), lambda b: (0, 0)),
                  pl.BlockSpec((D, 2 * XA_WIDTH), lambda b: (0, 0)),
                  pl.BlockSpec((1, XA_HEAD_DIM), lambda b: (0, 0))],
        out_specs=[pl.BlockSpec((1, M, XA_WIDTH), lambda b: (b, 0, 0))] * 2,
        out_shape=[jax.ShapeDtypeStruct((B, M, XA_WIDTH), BF16)] * 2,
        name="mem_kv",
    )(mem, g, wkv, gk)


def _hgrn2_level_masks():
    p = np.arange(HG_CHUNK)
    tok = N_PLANES * (p % PLANE_ROWS) + p // PLANE_ROWS
    ti, tj = tok[:, None], tok[None, :]
    masks = []
    h = 1
    while h < HG_CHUNK:
        masks.append((ti // (2 * h) == tj // (2 * h)) & ((ti // h) % 2 == 1) & ((tj // h) % 2 == 0))
        h *= 2
    masks.append(ti == tj)
    masks = np.stack(masks).astype(np.float32)
    assert masks.shape[0] == N_LEVELS
    assert (masks.sum(0) == (tj <= ti)).all()
    return np.tile(masks, (1, 1, 2))


def _plane_block_products(f):
    pre = {1: list(f)}
    suf = {1: [None] * N_PLANES}
    h = 1
    while h < N_PLANES:
        p, s = pre[h], suf[h]
        new_p, new_s = [], []
        for r in range(N_PLANES):
            blk = r // h
            if blk % 2 == 1:
                new_p.append(_mul(p[r], p[blk * h - 1]))
                new_s.append(s[r])
            else:
                new_p.append(p[r])
                new_s.append(_mul(s[r], p[(blk + 2) * h - 1]))
        pre[2 * h], suf[2 * h] = new_p, new_s
        h *= 2
    return pre, suf


def _group_block_products(total):
    rows = total.shape[0]
    m_idx = lax.broadcasted_iota(jnp.int32, total.shape, 0) % PLANE_ROWS
    wpre, wsuf = [None], [None]
    for d in range(1, PLANE_ROWS):
        wpre.append(_mul(wpre[-1], pltpu.roll(total, d, 0)))
        wsuf.append(_mul(wsuf[-1], pltpu.roll(total, rows - d, 0)))
    cpre, csuf = {1: None}, {1: None}
    u = 2
    while u <= PLANE_ROWS:
        off = m_idx % u
        a = jnp.ones_like(total)
        b = jnp.ones_like(total)
        for d in range(1, u):
            a = jnp.where(off == d, wpre[d], a)
            b = jnp.where(off == u - 1 - d, wsuf[d], b)
        cpre[u], csuf[u] = a, b
        u *= 2
    return cpre, csuf


def _block_diag(a, b):
    za, zb = jnp.zeros_like(a), jnp.zeros_like(b)
    return jnp.concatenate([jnp.concatenate([a, zb], axis=1),
                            jnp.concatenate([za, b], axis=1)], axis=0)


def _hgrn2_tile(slab_ref, lb, norm_g, masks_ref, state_ref, out_ref):
    tile = slab_ref.shape[1]
    n_chunks = tile // HG_CHUNK
    plane_len = tile // N_PLANES
    pair_w = 2 * HG_DK
    n_pairs = HG_HEADS // 2

    def planes(slab0):
        return [jnp.concatenate(
            [slab_ref[slab0 + h, pl.ds(r, plane_len, stride=N_PLANES), :] for h in range(HG_HEADS)],
            axis=1) for r in range(N_PLANES)]

    def chunk_rows(ps, c, lanes=slice(None)):
        return jnp.concatenate([p[c * PLANE_ROWS:(c + 1) * PLANE_ROWS, lanes] for p in ps], axis=0)

    def pair_lanes(p):
        return slice(p * pair_w, (p + 1) * pair_w)

    def heads_block_diag(x):
        return _block_diag(x[:, :HG_DK], x[:, HG_DK:])

    f = [lb + (1.0 - lb) * jax.nn.sigmoid(x) for x in planes(SLAB_F)]
    q = planes(SLAB_Q)
    k = [1.0 - x for x in f]
    v = planes(SLAB_I)

    pre, suf = _plane_block_products(f)
    pre8, suf8 = pre[N_PLANES], suf[N_PLANES]
    cpre, csuf = _group_block_products(pre8[N_PLANES - 1])

    levels = []
    hsz = 1
    while hsz < N_PLANES:
        levels.append((pre[hsz], suf[hsz]))
        hsz *= 2
    u = 1
    while u < PLANE_ROWS:
        levels.append(([_mul(x, cpre[u]) for x in pre8], [_mul(x, csuf[u]) for x in suf8]))
        u *= 2
    levels.append(([None] * N_PLANES, [None] * N_PLANES))

    scores = [[None] * n_pairs for _ in range(n_chunks)]
    for lvl, (pq, sk) in enumerate(levels):
        ql = [_mul(a, b) for a, b in zip(q, pq)]
        kl = [_mul(a, b) for a, b in zip(k, sk)]
        for c in range(n_chunks):
            for p in range(n_pairs):
                lhs = chunk_rows(ql, c, pair_lanes(p)).astype(BF16)
                rhs = heads_block_diag(chunk_rows(kl, c, pair_lanes(p)).astype(BF16))
                s = _dot_nt(lhs, rhs) * masks_ref[lvl]
                scores[c][p] = s if scores[c][p] is None else scores[c][p] + s
        if lvl % 2 == 0:
            yield

    q_full = [_mul(x, cpre[PLANE_ROWS]) * a for x, a in zip(pre8, q)]
    k_full = [_mul(_mul(x, csuf[PLANE_ROWS]), a) for x, a in zip(suf8, k)]
    decay = pre8[N_PLANES - 1] * cpre[PLANE_ROWS]

    states = [state_ref[h] for h in range(HG_HEADS)]
    outs = []
    for c in range(n_chunks):
        last = (c + 1) * PLANE_ROWS - 1
        o_pairs = []
        for p in range(n_pairs):
            lanes = pair_lanes(p)
            vc = chunk_rows(v, c, lanes).astype(BF16)
            st = _block_diag(states[2 * p], states[2 * p + 1]).astype(BF16)
            o = (_dot_nt(chunk_rows(q_full, c, lanes).astype(BF16), st)
                 + _dot(scores[c][p].astype(BF16), heads_block_diag(vc)))
            upd = _dot_tn(vc, chunk_rows(k_full, c, lanes).astype(BF16))
            for i in range(2):
                h = 2 * p + i
                blk = slice(i * HG_DK, (i + 1) * HG_DK)
                states[h] = (states[h] * decay[last:last + 1, h * HG_DK:(h + 1) * HG_DK]
                             + upd[blk, blk])
            o_pairs.append(o)
        outs.append(jnp.concatenate(o_pairs, axis=1))
    for h in range(HG_HEADS):
        state_ref[h] = states[h]
    yield

    gate = planes(SLAB_G)
    for r in range(N_PLANES):
        o_r = jnp.concatenate([o[r * PLANE_ROWS:(r + 1) * PLANE_ROWS] for o in outs], axis=0)
        g_r = gate[r]
        for h in range(HG_HEADS):
            lanes = slice(h * HG_DV, (h + 1) * HG_DV)
            y = _rms_norm(o_r[:, lanes], norm_g) * (g_r[:, lanes] * jax.nn.sigmoid(g_r[:, lanes]))
            out_ref[h, pl.ds(r, plane_len, stride=N_PLANES), :] = y


def _swa_tile(slab_ref, seq_start, ctab, stab, gq, gk, seg_mean, sinks_ref, kpad_ref, vpad_ref,
              out_ref):
    tile = slab_ref.shape[1]
    n_blocks = tile // SW_BLOCK
    n_pairs = SW_HEADS // 2
    q = jnp.concatenate([slab_ref[SLAB_SQ + s] for s in range(SW_WIDTH // LANES)], axis=1)
    k = slab_ref[SLAB_SK]
    v = slab_ref[SLAB_SV]

    def head_norm(t, g, mean_mat):
        ms = _dot((t * t).astype(BF16), mean_mat)
        return t * lax.rsqrt(ms + EPS) * g

    qn = _rope(head_norm(q, gq, seg_mean[...]), ctab, stab) * (SW_HEAD_DIM ** -0.5)
    kn = _rope(head_norm(k, gk, seg_mean[:LANES, :LANES]), ctab, stab)

    lane = lax.broadcasted_iota(jnp.int32, (tile, LANES), 1)
    low = lane < SW_HEAD_DIM

    def padded(t):
        rolled = pltpu.roll(t, SW_HEAD_DIM, 1)
        zero = jnp.zeros_like(t)
        out = []
        for kv_head in range(SW_KV_HEADS):
            src_even, src_odd = (t, rolled) if kv_head == 0 else (rolled, t)
            out.append(jnp.where(low, src_even, zero).astype(BF16))
            out.append(jnp.where(low, zero, src_odd).astype(BF16))
        return out

    kpads, vpads = padded(kn), padded(v)
    qb = qn.astype(BF16)
    yield

    def block_operand(pads, carry_ref, blk, kv):
        parts = []
        for i in (kv, kv + 1):
            prev = carry_ref[i] if blk == 0 else pads[i][(blk - 1) * SW_BLOCK:blk * SW_BLOCK]
            parts += [prev, pads[i][blk * SW_BLOCK:(blk + 1) * SW_BLOCK]]
        return jnp.concatenate(parts, axis=0)

    units = [(blk, pair) for blk in range(n_blocks) for pair in range(n_pairs)]
    heads = [(blk, pair, half) for blk, pair in units for half in range(2)]
    kv_of = lambda pair: 2 * (pair // (n_pairs // SW_KV_HEADS))

    scores = {}
    for blk, pair in units:
        rows = slice(blk * SW_BLOCK, (blk + 1) * SW_BLOCK)
        scores[blk, pair] = _dot_nt(qb[rows, pair * LANES:(pair + 1) * LANES],
                                    block_operand(kpads, kpad_ref, blk, kv_of(pair)))

    yield
    qi = lax.broadcasted_iota(jnp.int32, (SW_BLOCK, SW_BLOCK), 0)
    kj = lax.broadcasted_iota(jnp.int32, (SW_BLOCK, SW_BLOCK), 1)
    from_prev = kj > qi
    start_bias = jnp.where(seq_start, -jnp.inf, 0.0)

    merged, sink = {}, {}
    for blk, pair, half in heads:
        s = scores[blk, pair]
        s_prev = s[:, (2 * half) * SW_BLOCK:(2 * half + 1) * SW_BLOCK]
        s_cur = s[:, (2 * half + 1) * SW_BLOCK:(2 * half + 2) * SW_BLOCK]
        if blk == 0:
            s_prev = s_prev + start_bias
        merged[blk, pair, half] = jnp.where(from_prev, s_prev, s_cur)
        sink[blk, pair, half] = sinks_ref[2 * pair + half]
    top = {u: jnp.maximum(jnp.max(merged[u], axis=-1, keepdims=True), sink[u]) for u in heads}
    prob = {u: jnp.exp(merged[u] - top[u]) for u in heads}
    inv = {u: 1.0 / (jnp.sum(prob[u], axis=-1, keepdims=True) + jnp.exp(sink[u] - top[u]))
           for u in heads}

    yield
    low_o = lax.broadcasted_iota(jnp.int32, (SW_BLOCK, LANES), 1) < SW_HEAD_DIM
    for blk, pair in units:
        ps = []
        for half in range(2):
            p = prob[blk, pair, half]
            zero = jnp.zeros_like(p)
            ps += [jnp.where(from_prev, p, zero).astype(BF16),
                   jnp.where(from_prev, zero, p).astype(BF16)]
        o = _dot(jnp.concatenate(ps, axis=1), block_operand(vpads, vpad_ref, blk, kv_of(pair)))
        o = o * jnp.where(low_o, inv[blk, pair, 0], inv[blk, pair, 1])
        out_ref[blk * SW_BLOCK:(blk + 1) * SW_BLOCK, pair * LANES:(pair + 1) * LANES] = (
            o.astype(out_ref.dtype))

    last = slice((n_blocks - 1) * SW_BLOCK, n_blocks * SW_BLOCK)
    for i in range(2 * SW_KV_HEADS):
        kpad_ref[i] = kpads[i][last]
        vpad_ref[i] = vpads[i][last]


def _project_pieces(x_ref, g1_ref, win_ref, slab_ref):
    xn = _rms_norm(x_ref[...], g1_ref[...]).astype(BF16)
    col_step = 4 * LANES

    def piece(c0):
        c1 = min(c0 + col_step, IN_WIDTH)
        proj = _dot(xn, win_ref[:, c0:c1])
        for s in range((c1 - c0) // LANES):
            slab_ref[c0 // LANES + s] = proj[:, s * LANES:(s + 1) * LANES]

    return [functools.partial(piece, c0) for c0 in range(0, IN_WIDTH, col_step)]


def _mix_pieces(slab_ref, seq_start, lb, sinks_ref, ctab_ref, stab_ref, hgn_ref, gq_ref, gk_ref,
                segm_ref, masks_ref, mix_ref, hg_ref, state_ref, kpad_ref, vpad_ref):
    yield from _hgrn2_tile(slab_ref, lb, hgn_ref[...], masks_ref, state_ref, hg_ref)
    for h in range(HG_HEADS):
        mix_ref[:, h * HG_DV:(h + 1) * HG_DV] = hg_ref[h].astype(BF16)
    yield
    yield from _swa_tile(slab_ref, seq_start, ctab_ref[...], stab_ref[...], gq_ref[...],
                         gk_ref[...], segm_ref, sinks_ref, kpad_ref, vpad_ref,
                         mix_ref.at[:, pl.ds(HG_WIDTH, SW_WIDTH)])


def _mixer_kernel(sinks_ref, x_ref, ctab_ref, stab_ref, g1_ref, win_ref, hlb_ref, hgn_ref,
                  gq_ref, gk_ref, segm_ref, masks_ref, mix_ref,
                  slab_a, slab_b, hg_ref, state_ref, kpad_ref, vpad_ref, *, tiles_per_seq):
    s = pl.program_id(0)
    seq_start = lax.rem(s + tiles_per_seq - 1, tiles_per_seq) == 0

    @pl.when(s == 0)
    def _():
        slab_b[...] = jnp.zeros_like(slab_b)

    @pl.when(seq_start | (s == 0))
    def _():
        state_ref[...] = jnp.zeros_like(state_ref)
        kpad_ref[...] = jnp.zeros_like(kpad_ref)
        vpad_ref[...] = jnp.zeros_like(vpad_ref)

    hlb = hlb_ref[...]
    e = jnp.exp(hlb - jnp.max(hlb, axis=0, keepdims=True))
    lb = e[0:1, :] / jnp.sum(e, axis=0, keepdims=True)

    def step(write_slab, read_slab):
        project = _project_pieces(x_ref, g1_ref, win_ref, write_slab)
        mix = _mix_pieces(read_slab, seq_start, lb, sinks_ref, ctab_ref, stab_ref, hgn_ref,
                          gq_ref, gk_ref, segm_ref, masks_ref, mix_ref, hg_ref, state_ref,
                          kpad_ref, vpad_ref)
        assert len(project) == len(PROJECT_AFTER_MIX_STAGE)

        def project_after(stage):
            for piece, after in zip(project, PROJECT_AFTER_MIX_STAGE):
                if after == stage:
                    piece()

        project_after(-1)
        for stage, _ in enumerate(mix):
            project_after(stage)

    @pl.when(s % 2 == 0)
    def _():
        step(slab_a, slab_b)

    @pl.when(s % 2 == 1)
    def _():
        step(slab_b, slab_a)


def _const_spec(shape):
    zeros = (0,) * len(shape)
    return pl.BlockSpec(shape, lambda s: zeros)


def _mixer(x, ctab, stab, g1, w_in, hlb, hgn, gq, gk, sinks):
    B, S, D = x.shape
    tile = MIX_TILE
    n_tiles = B * S // tile
    seg_mean = jnp.asarray(np.kron(np.eye(SW_HEADS), np.full((SW_HEAD_DIM, SW_HEAD_DIM),
                                                              1.0 / SW_HEAD_DIM)), BF16)
    masks = jnp.asarray(_hgrn2_level_masks())
    projected = lambda width: pl.BlockSpec((tile, width), lambda s: (jnp.minimum(s, n_tiles - 1), 0))
    mixed = lambda width: pl.BlockSpec((tile, width), lambda s: (jnp.maximum(s - 1, 0), 0))
    slab = pltpu.VMEM((N_SLABS, tile, LANES), F32)
    out = pl.pallas_call(
        functools.partial(_mixer_kernel, tiles_per_seq=S // tile),
        grid=(n_tiles + 1,),
        in_specs=[pl.BlockSpec(memory_space=pltpu.SMEM),
                  projected(D), mixed(LANES), mixed(LANES),
                  _const_spec((1, D)), _const_spec((D, IN_WIDTH)),
                  _const_spec(hlb.shape), _const_spec((1, HG_DV)),
                  _const_spec((1, SW_WIDTH)), _const_spec((1, LANES)),
                  _const_spec((SW_WIDTH, SW_WIDTH)),
                  _const_spec((N_LEVELS, HG_CHUNK, 2 * HG_CHUNK))],
        out_specs=mixed(MIX_WIDTH),
        out_shape=jax.ShapeDtypeStruct((B * S, MIX_WIDTH), BF16),
        scratch_shapes=[slab, slab,
                        pltpu.VMEM((HG_HEADS, tile, LANES), F32),
                        pltpu.VMEM((HG_HEADS, HG_DV, HG_DK), F32),
                        pltpu.VMEM((2 * SW_KV_HEADS, SW_BLOCK, LANES), BF16),
                        pltpu.VMEM((2 * SW_KV_HEADS, SW_BLOCK, LANES), BF16)],
        compiler_params=pltpu.CompilerParams(
            dimension_semantics=("arbitrary",),
            vmem_limit_bytes=VMEM_LIMIT_BYTES),
        name="mixer",
    )(sinks, x.reshape(B * S, D), ctab.reshape(B * S, LANES), stab.reshape(B * S, LANES),
      g1, w_in, hlb, hgn, gq, gk, seg_mean, masks)
    return out.reshape(B, S, MIX_WIDTH)


def _tail_kernel(x_ref, mix_ref, wout_ref, g2_ref, wq_ref, gq_ref, k_ref, v_ref, wo_ref,
                 g3_ref, up_ref, down_ref, out_ref):
    h1 = x_ref[0] + _dot(mix_ref[0], wout_ref[...])

    q = _dot(_rms_norm(h1, g2_ref[...]).astype(BF16), wq_ref[...])
    heads = []
    for h in range(XA_HEADS):
        sl = slice(h * XA_HEAD_DIM, (h + 1) * XA_HEAD_DIM)
        qn = _rms_norm(q[:, sl], gq_ref[...]) * (XA_HEAD_DIM ** -0.5)
        s = _dot_nt(qn.astype(BF16), k_ref[0, :, sl])
        p = jnp.exp(s - jnp.max(s, axis=-1, keepdims=True))
        inv = 1.0 / jnp.sum(p, axis=-1, keepdims=True)
        heads.append(_dot(p.astype(BF16), v_ref[0, :, sl]) * inv)
    h2 = h1 + _dot(jnp.concatenate(heads, axis=1).astype(BF16), wo_ref[...])

    hn = _rms_norm(h2, g3_ref[...]).astype(BF16)
    acc = h2
    for c0 in range(0, up_ref.shape[1], FF_CHUNK):
        a = jnp.maximum(_dot(hn, up_ref[:, c0:c0 + FF_CHUNK]), 0.0)
        acc = acc + _dot((a * a).astype(BF16), down_ref[c0:c0 + FF_CHUNK, :])
    out_ref[0] = acc


def _tail(x, mix, w_out, g2, wq, gq, kmem, vmem, wo, g3, up, down):
    B, S, D = x.shape
    M = kmem.shape[1]
    tile = TAIL_TILE
    tok = lambda width: pl.BlockSpec((1, tile, width), lambda b, t: (b, t, 0))
    once = pl.Buffered(1)
    weight = lambda shape: pl.BlockSpec(shape, lambda b, t: (0,) * len(shape), pipeline_mode=once)
    per_batch = pl.BlockSpec((1, M, XA_WIDTH), lambda b, t: (b, 0, 0))
    return pl.pallas_call(
        _tail_kernel,
        grid=(B, S // tile),
        in_specs=[tok(D), tok(MIX_WIDTH),
                  weight(w_out.shape), weight((1, D)), weight(wq.shape), weight((1, XA_HEAD_DIM)),
                  per_batch, per_batch, weight(wo.shape), weight((1, D)),
                  weight(up.shape), weight(down.shape)],
        out_specs=tok(D),
        out_shape=jax.ShapeDtypeStruct((B, S, D), F32),
        compiler_params=pltpu.CompilerParams(
            dimension_semantics=("arbitrary", "arbitrary"),
            vmem_limit_bytes=VMEM_LIMIT_BYTES),
        name="tail",
    )(x, mix, w_out, g2, wq, gq, kmem, vmem, wo, g3, up, down)


def kernel(x, mem, positions, norm1_g, w_in, hg_lower_bounds, hg_norm_g, sw_q_norm_g, sw_k_norm_g,
           sw_sinks, w_out, norm2_g, mem_norm_g, xa_wq, xa_wkv, xa_q_norm_g, xa_k_norm_g, xa_wo,
           norm3_g, mlp_up, mlp_down):
    depth = norm1_g.shape[0]
    assert depth == 1 and x.shape[1] % TAIL_TILE == 0 and x.shape[1] % MIX_TILE == 0
    ctab, stab = _rope_tables(positions)
    h = x
    for l in range(depth):
        kmem, vmem = _mem_kv(mem, mem_norm_g[l][None], xa_wkv[l].astype(BF16), xa_k_norm_g[l][None])
        mix = _mixer(h, ctab, stab, norm1_g[l][None], w_in[l].astype(BF16), hg_lower_bounds,
                     hg_norm_g[l][None], jnp.tile(sw_q_norm_g[l], SW_HEADS)[None],
                     jnp.tile(sw_k_norm_g[l], SW_KV_HEADS)[None], sw_sinks[l])
        h = _tail(h, mix, w_out[l].astype(BF16), norm2_g[l][None], xa_wq[l].astype(BF16),
                  xa_q_norm_g[l][None], kmem, vmem, xa_wo[l].astype(BF16), norm3_g[l][None],
                  mlp_up[l].astype(BF16), mlp_down[l].astype(BF16))
    return h
```

```python
import functools

import numpy as np
import jax
import jax.numpy as jnp
from jax import lax
from jax.experimental import pallas as pl
from jax.experimental.pallas import tpu as pltpu

F32 = jnp.float32
BF16 = jnp.bfloat16

LANES = 128
SUBLANES = 8
VMEM_LIMIT_BYTES = 56 * 1024 * 1024

D_MODEL = 1024
EPS = 1e-6

HG_HEADS = 4
HG_DK = 128
HG_DV = 128
HG_WIDTH = HG_HEADS * HG_DV
HG_KEY_WIDTH = HG_HEADS * HG_DK

SW_HEADS = 8
SW_KV_HEADS = 2
SW_HEAD_DIM = 64
SW_WIDTH = SW_HEADS * SW_HEAD_DIM
SW_KV_WIDTH = SW_KV_HEADS * SW_HEAD_DIM
WINDOW = 128
SW_BLOCK = 128
ROPE_THETA = 500000.0
ROT_DIM = SW_HEAD_DIM // 4
ROT_HALF = ROT_DIM // 2

MIX_WIDTH = HG_WIDTH + SW_WIDTH
IN_WIDTH = 2 * HG_KEY_WIDTH + 2 * HG_WIDTH + SW_WIDTH + 2 * SW_KV_WIDTH

XA_HEADS = 4
XA_HEAD_DIM = 128
XA_WIDTH = XA_HEADS * XA_HEAD_DIM

N_SLABS = IN_WIDTH // LANES
SLAB_Q, SLAB_F, SLAB_I, SLAB_G = 0, 4, 8, 12
SLAB_SQ, SLAB_SK, SLAB_SV = 16, 20, 21

HG_CHUNK = 64
N_PLANES = 8
PLANE_ROWS = HG_CHUNK // N_PLANES
N_LEVELS = 7

MIX_TILE = 256
PROJECT_AFTER_MIX_STAGE = (-1, 0, 4, 5, 7, 8)
TAIL_TILE = 512
FF_CHUNK = 1024


def _mul(a, b):
    if a is None:
        return b
    if b is None:
        return a
    return a * b


def _rms_norm(x, g):
    return x * lax.rsqrt(jnp.mean(x * x, axis=-1, keepdims=True) + EPS) * g


def _dot(a, b):
    return jnp.dot(a, b, preferred_element_type=F32)


def _dot_nt(a, b):
    return lax.dot_general(a, b, (((1,), (1,)), ((), ())), preferred_element_type=F32)


def _dot_tn(a, b):
    return lax.dot_general(a, b, (((0,), (0,)), ((), ())), preferred_element_type=F32)


BF16_SPLIT_PARTS = 3


def _rope_table_kernel(pos_ref, invf_ref, ecos_ref, esin_ref, base_ref, ctab_ref, stab_ref):
    ang = pos_ref[0].astype(F32) * invf_ref[...]

    def spread(t, e_ref):
        parts, rest = [], t
        for _ in range(BF16_SPLIT_PARTS):
            piece = rest.astype(BF16).astype(F32)
            parts.append(piece)
            rest = rest - piece
        parts.append(jnp.zeros_like(t))
        return _dot_tn(jnp.concatenate(parts, axis=0).astype(BF16), e_ref[...])

    ctab_ref[0] = spread(jnp.cos(ang), ecos_ref) + base_ref[...]
    stab_ref[0] = spread(jnp.sin(ang), esin_ref)


def _rope_tables(positions):
    B, S = positions.shape
    inv_freq = ROPE_THETA ** (-(jnp.arange(ROT_HALF, dtype=F32) * 2.0 / ROT_DIM))
    dim = np.arange(LANES) % SW_HEAD_DIM
    freq = np.arange(ROT_HALF)[:, None]
    first, second = dim[None, :] == freq, dim[None, :] == freq + ROT_HALF
    pad = np.zeros((ROT_HALF, LANES))
    stack = lambda e: jnp.asarray(np.concatenate([e] * BF16_SPLIT_PARTS + [pad]), BF16)
    ecos = stack(first * 1.0 + second * 1.0)
    esin = stack(second * 1.0 - first * 1.0)
    base = jnp.asarray((dim >= ROT_DIM)[None, :], F32)
    rows = (BF16_SPLIT_PARTS + 1) * ROT_HALF
    const = lambda shape: pl.BlockSpec(shape, lambda b: (0, 0))
    return pl.pallas_call(
        _rope_table_kernel,
        grid=(B,),
        in_specs=[pl.BlockSpec((1, 1, S), lambda b: (b, 0, 0)), const((ROT_HALF, 1)),
                  const((rows, LANES)), const((rows, LANES)), const((1, LANES))],
        out_specs=[pl.BlockSpec((1, S, LANES), lambda b: (b, 0, 0))] * 2,
        out_shape=[jax.ShapeDtypeStruct((B, S, LANES), F32)] * 2,
        name="rope_tables",
    )(positions.reshape(B, 1, S), inv_freq.reshape(ROT_HALF, 1), ecos, esin, base)


def _rope(t, ctab, stab):
    width = t.shape[1]
    reps = width // LANES
    c = jnp.concatenate([ctab] * reps, axis=1) if reps > 1 else ctab
    s = jnp.concatenate([stab] * reps, axis=1) if reps > 1 else stab
    lane = lax.broadcasted_iota(jnp.int32, t.shape, 1)
    first_half = (lane % SW_HEAD_DIM) < ROT_HALF
    partner = jnp.where(first_half,
                        pltpu.roll(t, width - ROT_HALF, 1),
                        pltpu.roll(t, ROT_HALF, 1))
    return t * c + partner * s


def _mem_kv_kernel(mem_ref, g_ref, wkv_ref, gk_ref, k_ref, v_ref):
    mn = _rms_norm(mem_ref[0], g_ref[...]).astype(BF16)
    kv = _dot(mn, wkv_ref[...])
    for h in range(XA_HEADS):
        sl = slice(h * XA_HEAD_DIM, (h + 1) * XA_HEAD_DIM)
        k_ref[0, :, sl] = _rms_norm(kv[:, sl], gk_ref[...]).astype(BF16)
    v_ref[0] = kv[:, XA_WIDTH:].astype(BF16)


def _mem_kv(mem, g, wkv, gk):
    B, M, D = mem.shape
    return pl.pallas_call(
        _mem_kv_kernel,
        grid=(B,),
        in_specs=[pl.BlockSpec((1, M, D), lambda b: (b, 0, 0)),
                  pl.BlockSpec((1, D), lambda b: (0, 0)),
                  pl.BlockSpec((D, 2 * XA_WIDTH), lambda b: (0, 0)),
                  pl.BlockSpec((1, XA_HEAD_DIM), lambda b: (0, 0))],
        out_specs=[pl.BlockSpec((1, M, XA_WIDTH), lambda b: (b, 0, 0))] * 2,
        out_shape=[jax.ShapeDtypeStruct((B, M, XA_WIDTH), BF16)] * 2,
        name="mem_kv",
    )(mem, g, wkv, gk)


def _hgrn2_level_masks():
    p = np.arange(HG_CHUNK)
    tok = N_PLANES * (p % PLANE_ROWS) + p // PLANE_ROWS
    ti, tj = tok[:, None], tok[None, :]
    masks = []
    h = 1
    while h < HG_CHUNK:
        masks.append((ti // (2 * h) == tj // (2 * h)) & ((ti // h) % 2 == 1) & ((tj // h) % 2 == 0))
        h *= 2
    masks.append(ti == tj)
    masks = np.stack(masks).astype(np.float32)
    assert masks.shape[0] == N_LEVELS
    assert (masks.sum(0) == (tj <= ti)).all()
    return np.tile(masks, (1, 1, 2))


def _plane_block_products(f):
    pre = {1: list(f)}
    suf = {1: [None] * N_PLANES}
    h = 1
    while h < N_PLANES:
        p, s = pre[h], suf[h]
        new_p, new_s = [], []
        for r in range(N_PLANES):
            blk = r // h
            if blk % 2 == 1:
                new_p.append(_mul(p[r], p[blk * h - 1]))
                new_s.append(s[r])
            else:
                new_p.append(p[r])
                new_s.append(_mul(s[r], p[(blk + 2) * h - 1]))
        pre[2 * h], suf[2 * h] = new_p, new_s
        h *= 2
    return pre, suf


def _group_block_products(total):
    rows = total.shape[0]
    m_idx = lax.broadcasted_iota(jnp.int32, total.shape, 0) % PLANE_ROWS
    wpre, wsuf = [None], [None]
    for d in range(1, PLANE_ROWS):
        wpre.append(_mul(wpre[-1], pltpu.roll(total, d, 0)))
        wsuf.append(_mul(wsuf[-1], pltpu.roll(total, rows - d, 0)))
    cpre, csuf = {1: None}, {1: None}
    u = 2
    while u <= PLANE_ROWS:
        off = m_idx % u
        a = jnp.ones_like(total)
        b = jnp.ones_like(total)
        for d in range(1, u):
            a = jnp.where(off == d, wpre[d], a)
            b = jnp.where(off == u - 1 - d, wsuf[d], b)
        cpre[u], csuf[u] = a, b
        u *= 2
    return cpre, csuf


def _block_diag(a, b):
    za, zb = jnp.zeros_like(a), jnp.zeros_like(b)
    return jnp.concatenate([jnp.concatenate([a, zb], axis=1),
                            jnp.concatenate([za, b], axis=1)], axis=0)


def _hgrn2_tile(slab_ref, lb, norm_g, masks_ref, state_ref, out_ref):
    tile = slab_ref.shape[1]
    n_chunks = tile // HG_CHUNK
    plane_len = tile // N_PLANES
    pair_w = 2 * HG_DK
    n_pairs = HG_HEADS // 2

    def planes(slab0):
        return [jnp.concatenate(
            [slab_ref[slab0 + h, pl.ds(r, plane_len, stride=N_PLANES), :] for h in range(HG_HEADS)],
            axis=1) for r in range(N_PLANES)]

    def chunk_rows(ps, c, lanes=slice(None)):
        return jnp.concatenate([p[c * PLANE_ROWS:(c + 1) * PLANE_ROWS, lanes] for p in ps], axis=0)

    def pair_lanes(p):
        return slice(p * pair_w, (p + 1) * pair_w)

    def heads_block_diag(x):
        return _block_diag(x[:, :HG_DK], x[:, HG_DK:])

    f = [lb + (1.0 - lb) * jax.nn.sigmoid(x) for x in planes(SLAB_F)]
    q = planes(SLAB_Q)
    k = [1.0 - x for x in f]
    v = planes(SLAB_I)

    pre, suf = _plane_block_products(f)
    pre8, suf8 = pre[N_PLANES], suf[N_PLANES]
    cpre, csuf = _group_block_products(pre8[N_PLANES - 1])

    levels = []
    hsz = 1
    while hsz < N_PLANES:
        levels.append((pre[hsz], suf[hsz]))
        hsz *= 2
    u = 1
    while u < PLANE_ROWS:
        levels.append(([_mul(x, cpre[u]) for x in pre8], [_mul(x, csuf[u]) for x in suf8]))
        u *= 2
    levels.append(([None] * N_PLANES, [None] * N_PLANES))

    scores = [[None] * n_pairs for _ in range(n_chunks)]
    for lvl, (pq, sk) in enumerate(levels):
        ql = [_mul(a, b) for a, b in zip(q, pq)]
        kl = [_mul(a, b) for a, b in zip(k, sk)]
        for c in range(n_chunks):
            for p in range(n_pairs):
                lhs = chunk_rows(ql, c, pair_lanes(p)).astype(BF16)
                rhs = heads_block_diag(chunk_rows(kl, c, pair_lanes(p)).astype(BF16))
                s = _dot_nt(lhs, rhs) * masks_ref[lvl]
                scores[c][p] = s if scores[c][p] is None else scores[c][p] + s
        if lvl % 2 == 0:
            yield

    q_full = [_mul(x, cpre[PLANE_ROWS]) * a for x, a in zip(pre8, q)]
    k_full = [_mul(_mul(x, csuf[PLANE_ROWS]), a) for x, a in zip(suf8, k)]
    decay = pre8[N_PLANES - 1] * cpre[PLANE_ROWS]

    states = [state_ref[h] for h in range(HG_HEADS)]
    outs = []
    for c in range(n_chunks):
        last = (c + 1) * PLANE_ROWS - 1
        o_pairs = []
        for p in range(n_pairs):
            lanes = pair_lanes(p)
            vc = chunk_rows(v, c, lanes).astype(BF16)
            st = _block_diag(states[2 * p], states[2 * p + 1]).astype(BF16)
            o = (_dot_nt(chunk_rows(q_full, c, lanes).astype(BF16), st)
                 + _dot(scores[c][p].astype(BF16), heads_block_diag(vc)))
            upd = _dot_tn(vc, chunk_rows(k_full, c, lanes).astype(BF16))
            for i in range(2):
                h = 2 * p + i
                blk = slice(i * HG_DK, (i + 1) * HG_DK)
                states[h] = (states[h] * decay[last:last + 1, h * HG_DK:(h + 1) * HG_DK]
                             + upd[blk, blk])
            o_pairs.append(o)
        outs.append(jnp.concatenate(o_pairs, axis=1))
    for h in range(HG_HEADS):
        state_ref[h] = states[h]
    yield

    gate = planes(SLAB_G)
    for r in range(N_PLANES):
        o_r = jnp.concatenate([o[r * PLANE_ROWS:(r + 1) * PLANE_ROWS] for o in outs], axis=0)
        g_r = gate[r]
        for h in range(HG_HEADS):
            lanes = slice(h * HG_DV, (h + 1) * HG_DV)
            y = _rms_norm(o_r[:, lanes], norm_g) * (g_r[:, lanes] * jax.nn.sigmoid(g_r[:, lanes]))
            out_ref[h, pl.ds(r, plane_len, stride=N_PLANES), :] = y


def _swa_tile(slab_ref, seq_start, ctab, stab, gq, gk, seg_mean, sinks_ref, kpad_ref, vpad_ref,
              out_ref):
    tile = slab_ref.shape[1]
    n_blocks = tile // SW_BLOCK
    n_pairs = SW_HEADS // 2
    q = jnp.concatenate([slab_ref[SLAB_SQ + s] for s in range(SW_WIDTH // LANES)], axis=1)
    k = slab_ref[SLAB_SK]
    v = slab_ref[SLAB_SV]

    def head_norm(t, g, mean_mat):
        ms = _dot((t * t).astype(BF16), mean_mat)
        return t * lax.rsqrt(ms + EPS) * g

    qn = _rope(head_norm(q, gq, seg_mean[...]), ctab, stab) * (SW_HEAD_DIM ** -0.5)
    kn = _rope(head_norm(k, gk, seg_mean[:LANES, :LANES]), ctab, stab)

    lane = lax.broadcasted_iota(jnp.int32, (tile, LANES), 1)
    low = lane < SW_HEAD_DIM

    def padded(t):
        rolled = pltpu.roll(t, SW_HEAD_DIM, 1)
        zero = jnp.zeros_like(t)
        out = []
        for kv_head in range(SW_KV_HEADS):
            src_even, src_odd = (t, rolled) if kv_head == 0 else (rolled, t)
            out.append(jnp.where(low, src_even, zero).astype(BF16))
            out.append(jnp.where(low, zero, src_odd).astype(BF16))
        return out

    kpads, vpads = padded(kn), padded(v)
    qb = qn.astype(BF16)
    yield

    def block_operand(pads, carry_ref, blk, kv):
        parts = []
        for i in (kv, kv + 1):
            prev = carry_ref[i] if blk == 0 else pads[i][(blk - 1) * SW_BLOCK:blk * SW_BLOCK]
            parts += [prev, pads[i][blk * SW_BLOCK:(blk + 1) * SW_BLOCK]]
        return jnp.concatenate(parts, axis=0)

    units = [(blk, pair) for blk in range(n_blocks) for pair in range(n_pairs)]
    heads = [(blk, pair, half) for blk, pair in units for half in range(2)]
    kv_of = lambda pair: 2 * (pair // (n_pairs // SW_KV_HEADS))

    scores = {}
    for blk, pair in units:
        rows = slice(blk * SW_BLOCK, (blk + 1) * SW_BLOCK)
        scores[blk, pair] = _dot_nt(qb[rows, pair * LANES:(pair + 1) * LANES],
                                    block_operand(kpads, kpad_ref, blk, kv_of(pair)))

    yield
    qi = lax.broadcasted_iota(jnp.int32, (SW_BLOCK, SW_BLOCK), 0)
    kj = lax.broadcasted_iota(jnp.int32, (SW_BLOCK, SW_BLOCK), 1)
    from_prev = kj > qi
    start_bias = jnp.where(seq_start, -jnp.inf, 0.0)

    merged, sink = {}, {}
    for blk, pair, half in heads:
        s = scores[blk, pair]
        s_prev = s[:, (2 * half) * SW_BLOCK:(2 * half + 1) * SW_BLOCK]
        s_cur = s[:, (2 * half + 1) * SW_BLOCK:(2 * half + 2) * SW_BLOCK]
        if blk == 0:
            s_prev = s_prev + start_bias
        merged[blk, pair, half] = jnp.where(from_prev, s_prev, s_cur)
        sink[blk, pair, half] = sinks_ref[2 * pair + half]
    top = {u: jnp.maximum(jnp.max(merged[u], axis=-1, keepdims=True), sink[u]) for u in heads}
    prob = {u: jnp.exp(merged[u] - top[u]) for u in heads}
    inv = {u: 1.0 / (jnp.sum(prob[u], axis=-1, keepdims=True) + jnp.exp(sink[u] - top[u]))
           for u in heads}

    yield
    low_o = lax.broadcasted_iota(jnp.int32, (SW_BLOCK, LANES), 1) < SW_HEAD_DIM
    for blk, pair in units:
        ps = []
        for half in range(2):
            p = prob[blk, pair, half]
            zero = jnp.zeros_like(p)
            ps += [jnp.where(from_prev, p, zero).astype(BF16),
                   jnp.where(from_prev, zero, p).astype(BF16)]
        o = _dot(jnp.concatenate(ps, axis=1), block_operand(vpads, vpad_ref, blk, kv_of(pair)))
        o = o * jnp.where(low_o, inv[blk, pair, 0], inv[blk, pair, 1])
        out_ref[blk * SW_BLOCK:(blk + 1) * SW_BLOCK, pair * LANES:(pair + 1) * LANES] = (
            o.astype(out_ref.dtype))

    last = slice((n_blocks - 1) * SW_BLOCK, n_blocks * SW_BLOCK)
    for i in range(2 * SW_KV_HEADS):
        kpad_ref[i] = kpads[i][last]
        vpad_ref[i] = vpads[i][last]


def _project_pieces(x_ref, g1_ref, win_ref, slab_ref):
    xn = _rms_norm(x_ref[...], g1_ref[...]).astype(BF16)
    col_step = 4 * LANES

    def piece(c0):
        c1 = min(c0 + col_step, IN_WIDTH)
        proj = _dot(xn, win_ref[:, c0:c1])
        for s in range((c1 - c0) // LANES):
            slab_ref[c0 // LANES + s] = proj[:, s * LANES:(s + 1) * LANES]

    return [functools.partial(piece, c0) for c0 in range(0, IN_WIDTH, col_step)]


def _mix_pieces(slab_ref, seq_start, lb, sinks_ref, ctab_ref, stab_ref, hgn_ref, gq_ref, gk_ref,
                segm_ref, masks_ref, mix_ref, hg_ref, state_ref, kpad_ref, vpad_ref):
    yield from _hgrn2_tile(slab_ref, lb, hgn_ref[...], masks_ref, state_ref, hg_ref)
    for h in range(HG_HEADS):
        mix_ref[:, h * HG_DV:(h + 1) * HG_DV] = hg_ref[h].astype(BF16)
    yield
    yield from _swa_tile(slab_ref, seq_start, ctab_ref[...], stab_ref[...], gq_ref[...],
                         gk_ref[...], segm_ref, sinks_ref, kpad_ref, vpad_ref,
                         mix_ref.at[:, pl.ds(HG_WIDTH, SW_WIDTH)])


STEP_ORDER = "TMP TM TMP TM TMP MTP MT MP MTP M".replace(" ", "")


def _layer_kernel(sinks_ref, xp_ref, xt_ref, ctab_ref, stab_ref, g1_ref, win_ref, hlb_ref, hgn_ref,
                  gq_ref, gk_ref, segm_ref, masks_ref, wout_ref, g2_ref, wq_ref, xgq_ref, kmem_ref,
                  vmem_ref, wo_ref, g3_ref, up_ref, down_ref, out_ref,
                  slab_a, slab_b, mix_a, mix_b, hg_ref, state_ref, kpad_ref, vpad_ref,
                  *, tiles_per_seq):
    s = pl.program_id(0)
    seq_start = lax.rem(s + tiles_per_seq - 1, tiles_per_seq) == 0

    @pl.when(s == 0)
    def _():
        slab_b[...] = jnp.zeros_like(slab_b)
        mix_a[...] = jnp.zeros_like(mix_a)

    @pl.when(seq_start | (s == 0))
    def _():
        state_ref[...] = jnp.zeros_like(state_ref)
        kpad_ref[...] = jnp.zeros_like(kpad_ref)
        vpad_ref[...] = jnp.zeros_like(vpad_ref)

    hlb = hlb_ref[...]
    e = jnp.exp(hlb - jnp.max(hlb, axis=0, keepdims=True))
    lb = e[0:1, :] / jnp.sum(e, axis=0, keepdims=True)

    def step(project_slab, mix_slab, mix_out, tail_in):
        streams = {
            "P": iter(_project_pieces(xp_ref, g1_ref, win_ref, project_slab)),
            "M": _mix_pieces(mix_slab, seq_start, lb, sinks_ref, ctab_ref, stab_ref, hgn_ref,
                             gq_ref, gk_ref, segm_ref, masks_ref, mix_out, hg_ref, state_ref,
                             kpad_ref, vpad_ref),
            "T": _tail_stages(xt_ref, tail_in, wout_ref, g2_ref, wq_ref, xgq_ref, kmem_ref.at[0],
                              vmem_ref.at[0], wo_ref, g3_ref, up_ref, down_ref, out_ref),
        }
        for name in STEP_ORDER:
            piece = next(streams[name], None)
            if callable(piece):
                piece()
        for stream in streams.values():
            assert next(stream, "done") == "done"

    @pl.when(s % 2 == 0)
    def _():
        step(slab_a, slab_b, mix_b, mix_a)

    @pl.when(s % 2 == 1)
    def _():
        step(slab_b, slab_a, mix_a, mix_b)


def _layer(x, mem_k, mem_v, ctab, stab, sinks, g1, w_in, hlb, hgn, gq, gk, w_out, g2, wq, xgq, wo,
           g3, up, down):
    B, S, D = x.shape
    M = mem_k.shape[1]
    tile = MIX_TILE
    tiles_per_seq = S // tile
    n_tiles = B * tiles_per_seq
    seg_mean = jnp.asarray(np.kron(np.eye(SW_HEADS), np.full((SW_HEAD_DIM, SW_HEAD_DIM),
                                                              1.0 / SW_HEAD_DIM)), BF16)
    masks = jnp.asarray(_hgrn2_level_masks())

    def tile_index(lag):
        return lambda s: jnp.clip(s - lag, 0, n_tiles - 1)

    def tok(width, lag):
        index = tile_index(lag)
        return pl.BlockSpec((tile, width), lambda s: (index(s), 0))

    def const(shape):
        return pl.BlockSpec(shape, lambda s: (0,) * len(shape), pipeline_mode=pl.Buffered(1))

    tail_index = tile_index(2)
    per_seq = pl.BlockSpec((1, M, XA_WIDTH), lambda s: (tail_index(s) // tiles_per_seq, 0, 0))
    slab = pltpu.VMEM((N_SLABS, tile, LANES), F32)
    mix_buf = pltpu.VMEM((tile, MIX_WIDTH), BF16)
    x2 = x.reshape(B * S, D)
    out = pl.pallas_call(
        functools.partial(_layer_kernel, tiles_per_seq=tiles_per_seq),
        grid=(n_tiles + 2,),
        in_specs=[pl.BlockSpec(memory_space=pltpu.SMEM),
                  tok(D, 0), tok(D, 2), tok(LANES, 1), tok(LANES, 1),
                  const((1, D)), const(w_in.shape), const(hlb.shape), const((1, HG_DV)),
                  const((1, SW_WIDTH)), const((1, LANES)), const((SW_WIDTH, SW_WIDTH)),
                  const((N_LEVELS, HG_CHUNK, 2 * HG_CHUNK)),
                  const(w_out.shape), const((1, D)), const(wq.shape), const((1, XA_HEAD_DIM)),
                  per_seq, per_seq, const(wo.shape), const((1, D)), const(up.shape),
                  const(down.shape)],
        out_specs=tok(D, 2),
        out_shape=jax.ShapeDtypeStruct((B * S, D), F32),
        scratch_shapes=[slab, slab, mix_buf, mix_buf,
                        pltpu.VMEM((HG_HEADS, tile, LANES), F32),
                        pltpu.VMEM((HG_HEADS, HG_DV, HG_DK), F32),
                        pltpu.VMEM((2 * SW_KV_HEADS, SW_BLOCK, LANES), BF16),
                        pltpu.VMEM((2 * SW_KV_HEADS, SW_BLOCK, LANES), BF16)],
        compiler_params=pltpu.CompilerParams(
            dimension_semantics=("arbitrary",),
            vmem_limit_bytes=VMEM_LIMIT_BYTES),
        name="layer",
    )(sinks, x2, x2, ctab.reshape(B * S, LANES), stab.reshape(B * S, LANES), g1, w_in, hlb, hgn,
      gq, gk, seg_mean, masks, w_out, g2, wq, xgq, mem_k, mem_v, wo, g3, up, down)
    return out.reshape(B, S, D)


def _tail_stages(x_ref, mix_ref, wout_ref, g2_ref, wq_ref, gq_ref, k_ref, v_ref, wo_ref,
                 g3_ref, up_ref, down_ref, out_ref):
    h1 = x_ref[...] + _dot(mix_ref[...], wout_ref[...])
    yield

    q = _dot(_rms_norm(h1, g2_ref[...]).astype(BF16), wq_ref[...])
    yield

    heads = []
    for h in range(XA_HEADS):
        sl = slice(h * XA_HEAD_DIM, (h + 1) * XA_HEAD_DIM)
        qn = _rms_norm(q[:, sl], gq_ref[...]) * (XA_HEAD_DIM ** -0.5)
        s = _dot_nt(qn.astype(BF16), k_ref[:, sl])
        p = jnp.exp(s - jnp.max(s, axis=-1, keepdims=True))
        inv = 1.0 / jnp.sum(p, axis=-1, keepdims=True)
        heads.append(_dot(p.astype(BF16), v_ref[:, sl]) * inv)
    yield

    h2 = h1 + _dot(jnp.concatenate(heads, axis=1).astype(BF16), wo_ref[...])
    out_ref[...] = h2
    hn = _rms_norm(h2, g3_ref[...]).astype(BF16)
    for c0 in range(0, up_ref.shape[1], FF_CHUNK):
        yield
        a = jnp.maximum(_dot(hn, up_ref[:, c0:c0 + FF_CHUNK]), 0.0)
        out_ref[...] += _dot((a * a).astype(BF16), down_ref[c0:c0 + FF_CHUNK, :])


def kernel(x, mem, positions, norm1_g, w_in, hg_lower_bounds, hg_norm_g, sw_q_norm_g, sw_k_norm_g,
           sw_sinks, w_out, norm2_g, mem_norm_g, xa_wq, xa_wkv, xa_q_norm_g, xa_k_norm_g, xa_wo,
           norm3_g, mlp_up, mlp_down):
    depth = norm1_g.shape[0]
    assert depth == 1 and x.shape[1] % MIX_TILE == 0
    ctab, stab = _rope_tables(positions)
    h = x
    for l in range(depth):
        kmem, vmem = _mem_kv(mem, mem_norm_g[l][None], xa_wkv[l].astype(BF16), xa_k_norm_g[l][None])
        h = _layer(h, kmem, vmem, ctab, stab, sw_sinks[l], norm1_g[l][None], w_in[l].astype(BF16),
                   hg_lower_bounds, hg_norm_g[l][None], jnp.tile(sw_q_norm_g[l], SW_HEADS)[None],
                   jnp.tile(sw_k_norm_g[l], SW_KV_HEADS)[None], w_out[l].astype(BF16),
                   norm2_g[l][None], xa_wq[l].astype(BF16), xa_q_norm_g[l][None],
                   xa_wo[l].astype(BF16), norm3_g[l][None], mlp_up[l].astype(BF16),
                   mlp_down[l].astype(BF16))
    return h
```

```python
import functools

import numpy as np
import jax
import jax.numpy as jnp
from jax import lax
from jax.experimental import pallas as pl
from jax.experimental.pallas import tpu as pltpu

F32 = jnp.float32
BF16 = jnp.bfloat16

LANES = 128
SUBLANES = 8
VMEM_LIMIT_BYTES = 56 * 1024 * 1024

D_MODEL = 1024
EPS = 1e-6

HG_HEADS = 4
HG_DK = 128
HG_DV = 128
HG_WIDTH = HG_HEADS * HG_DV
HG_KEY_WIDTH = HG_HEADS * HG_DK

SW_HEADS = 8
SW_KV_HEADS = 2
SW_HEAD_DIM = 64
SW_WIDTH = SW_HEADS * SW_HEAD_DIM
SW_KV_WIDTH = SW_KV_HEADS * SW_HEAD_DIM
WINDOW = 128
SW_BLOCK = 128
ROPE_THETA = 500000.0
ROT_DIM = SW_HEAD_DIM // 4
ROT_HALF = ROT_DIM // 2

MIX_WIDTH = HG_WIDTH + SW_WIDTH
IN_WIDTH = 2 * HG_KEY_WIDTH + 2 * HG_WIDTH + SW_WIDTH + 2 * SW_KV_WIDTH

XA_HEADS = 4
XA_HEAD_DIM = 128
XA_WIDTH = XA_HEADS * XA_HEAD_DIM

N_SLABS = IN_WIDTH // LANES
SLAB_Q, SLAB_F, SLAB_I, SLAB_G = 0, 4, 8, 12
SLAB_SQ, SLAB_SK, SLAB_SV = 16, 20, 21

HG_CHUNK = 64
N_PLANES = 8
PLANE_ROWS = HG_CHUNK // N_PLANES
N_LEVELS = 7

MIX_TILE = 256
FF_CHUNK = 1024


def _mul(a, b):
    if a is None:
        return b
    if b is None:
        return a
    return a * b


def _rms_norm(x, g):
    return x * lax.rsqrt(jnp.mean(x * x, axis=-1, keepdims=True) + EPS) * g


def _dot(a, b):
    return jnp.dot(a, b, preferred_element_type=F32)


def _dot_nt(a, b):
    return lax.dot_general(a, b, (((1,), (1,)), ((), ())), preferred_element_type=F32)


def _dot_tn(a, b):
    return lax.dot_general(a, b, (((0,), (0,)), ((), ())), preferred_element_type=F32)


BF16_SPLIT_PARTS = 3


def _rope_table_kernel(pos_ref, invf_ref, ecos_ref, esin_ref, base_ref, ctab_ref, stab_ref):
    ang = pos_ref[0].astype(F32) * invf_ref[...]

    def spread(t, e_ref):
        parts, rest = [], t
        for _ in range(BF16_SPLIT_PARTS):
            piece = rest.astype(BF16).astype(F32)
            parts.append(piece)
            rest = rest - piece
        parts.append(jnp.zeros_like(t))
        return _dot_tn(jnp.concatenate(parts, axis=0).astype(BF16), e_ref[...])

    ctab_ref[0] = spread(jnp.cos(ang), ecos_ref) + base_ref[...]
    stab_ref[0] = spread(jnp.sin(ang), esin_ref)


def _rope_tables(positions):
    B, S = positions.shape
    inv_freq = ROPE_THETA ** (-(jnp.arange(ROT_HALF, dtype=F32) * 2.0 / ROT_DIM))
    dim = np.arange(LANES) % SW_HEAD_DIM
    freq = np.arange(ROT_HALF)[:, None]
    first, second = dim[None, :] == freq, dim[None, :] == freq + ROT_HALF
    pad = np.zeros((ROT_HALF, LANES))
    stack = lambda e: jnp.asarray(np.concatenate([e] * BF16_SPLIT_PARTS + [pad]), BF16)
    ecos = stack(first * 1.0 + second * 1.0)
    esin = stack(second * 1.0 - first * 1.0)
    base = jnp.asarray((dim >= ROT_DIM)[None, :], F32)
    rows = (BF16_SPLIT_PARTS + 1) * ROT_HALF
    const = lambda shape: pl.BlockSpec(shape, lambda b: (0, 0))
    return pl.pallas_call(
        _rope_table_kernel,
        grid=(B,),
        in_specs=[pl.BlockSpec((1, 1, S), lambda b: (b, 0, 0)), const((ROT_HALF, 1)),
                  const((rows, LANES)), const((rows, LANES)), const((1, LANES))],
        out_specs=[pl.BlockSpec((1, S, LANES), lambda b: (b, 0, 0))] * 2,
        out_shape=[jax.ShapeDtypeStruct((B, S, LANES), F32)] * 2,
        name="rope_tables",
    )(positions.reshape(B, 1, S), inv_freq.reshape(ROT_HALF, 1), ecos, esin, base)


def _rope(t, ctab, stab):
    width = t.shape[1]
    reps = width // LANES
    c = jnp.concatenate([ctab] * reps, axis=1) if reps > 1 else ctab
    s = jnp.concatenate([stab] * reps, axis=1) if reps > 1 else stab
    lane = lax.broadcasted_iota(jnp.int32, t.shape, 1)
    first_half = (lane % SW_HEAD_DIM) < ROT_HALF
    partner = jnp.where(first_half,
                        pltpu.roll(t, width - ROT_HALF, 1),
                        pltpu.roll(t, ROT_HALF, 1))
    return t * c + partner * s


def _mem_kv_kernel(mem_ref, g_ref, wkv_ref, gk_ref, k_ref, v_ref):
    mn = _rms_norm(mem_ref[0], g_ref[...]).astype(BF16)
    kv = _dot(mn, wkv_ref[...])
    for h in range(XA_HEADS):
        sl = slice(h * XA_HEAD_DIM, (h + 1) * XA_HEAD_DIM)
        k_ref[0, :, sl] = _rms_norm(kv[:, sl], gk_ref[...]).astype(BF16)
    v_ref[0] = kv[:, XA_WIDTH:].astype(BF16)


def _mem_kv(mem, g, wkv, gk):
    B, M, D = mem.shape
    return pl.pallas_call(
        _mem_kv_kernel,
        grid=(B,),
        in_specs=[pl.BlockSpec((1, M, D), lambda b: (b, 0, 0)),
                  pl.BlockSpec((1, D), lambda b: (0, 0)),
                  pl.BlockSpec((D, 2 * XA_WIDTH), lambda b: (0, 0)),
                  pl.BlockSpec((1, XA_HEAD_DIM), lambda b: (0, 0))],
        out_specs=[pl.BlockSpec((1, M, XA_WIDTH), lambda b: (b, 0, 0))] * 2,
        out_shape=[jax.ShapeDtypeStruct((B, M, XA_WIDTH), BF16)] * 2,
        name="mem_kv",
    )(mem, g, wkv, gk)


def _hgrn2_level_masks():
    p = np.arange(HG_CHUNK)
    tok = N_PLANES * (p % PLANE_ROWS) + p // PLANE_ROWS
    ti, tj = tok[:, None], tok[None, :]
    masks = []
    h = 1
    while h < HG_CHUNK:
        masks.append((ti // (2 * h) == tj // (2 * h)) & ((ti // h) % 2 == 1) & ((tj // h) % 2 == 0))
        h *= 2
    masks.append(ti == tj)
    masks = np.stack(masks).astype(np.float32)
    assert masks.shape[0] == N_LEVELS
    assert (masks.sum(0) == (tj <= ti)).all()
    return np.tile(masks, (1, 1, 2))


def _plane_block_products(f):
    pre = {1: list(f)}
    suf = {1: [None] * N_PLANES}
    h = 1
    while h < N_PLANES:
        p, s = pre[h], suf[h]
        new_p, new_s = [], []
        for r in range(N_PLANES):
            blk = r // h
            if blk % 2 == 1:
                new_p.append(_mul(p[r], p[blk * h - 1]))
                new_s.append(s[r])
            else:
                new_p.append(p[r])
                new_s.append(_mul(s[r], p[(blk + 2) * h - 1]))
        pre[2 * h], suf[2 * h] = new_p, new_s
        h *= 2
    return pre, suf


def _group_block_products(total):
    rows = total.shape[0]
    m_idx = lax.broadcasted_iota(jnp.int32, total.shape, 0) % PLANE_ROWS
    wpre, wsuf = [None], [None]
    for d in range(1, PLANE_ROWS):
        wpre.append(_mul(wpre[-1], pltpu.roll(total, d, 0)))
        wsuf.append(_mul(wsuf[-1], pltpu.roll(total, rows - d, 0)))
    cpre, csuf = {1: None}, {1: None}
    u = 2
    while u <= PLANE_ROWS:
        off = m_idx % u
        a = jnp.ones_like(total)
        b = jnp.ones_like(total)
        for d in range(1, u):
            a = jnp.where(off == d, wpre[d], a)
            b = jnp.where(off == u - 1 - d, wsuf[d], b)
        cpre[u], csuf[u] = a, b
        u *= 2
    return cpre, csuf


def _block_diag(a, b):
    za, zb = jnp.zeros_like(a), jnp.zeros_like(b)
    return jnp.concatenate([jnp.concatenate([a, zb], axis=1),
                            jnp.concatenate([za, b], axis=1)], axis=0)


def _hgrn2_tile(slab_ref, lb, norm_g, masks_ref, state_ref, out_ref):
    tile = slab_ref.shape[1]
    n_chunks = tile // HG_CHUNK
    plane_len = tile // N_PLANES
    pair_w = 2 * HG_DK
    n_pairs = HG_HEADS // 2

    def planes(slab0):
        return [jnp.concatenate(
            [slab_ref[slab0 + h, pl.ds(r, plane_len, stride=N_PLANES), :] for h in range(HG_HEADS)],
            axis=1) for r in range(N_PLANES)]

    def chunk_rows(ps, c, lanes=slice(None)):
        return jnp.concatenate([p[c * PLANE_ROWS:(c + 1) * PLANE_ROWS, lanes] for p in ps], axis=0)

    def pair_lanes(p):
        return slice(p * pair_w, (p + 1) * pair_w)

    def heads_block_diag(x):
        return _block_diag(x[:, :HG_DK], x[:, HG_DK:])

    f = [lb + (1.0 - lb) * jax.nn.sigmoid(x) for x in planes(SLAB_F)]
    q = planes(SLAB_Q)
    k = [1.0 - x for x in f]
    v = planes(SLAB_I)

    pre, suf = _plane_block_products(f)
    pre8, suf8 = pre[N_PLANES], suf[N_PLANES]
    cpre, csuf = _group_block_products(pre8[N_PLANES - 1])

    q8 = [_mul(a, b) for a, b in zip(q, pre8)]
    k8 = [_mul(a, b) for a, b in zip(k, suf8)]
    def scaled(xs, factors):
        return lambda: [_mul(a, b) for a, b in zip(xs, factors)]

    levels = []
    hsz = 1
    while hsz < N_PLANES:
        levels.append((scaled(q, pre[hsz]), scaled(k, suf[hsz])))
        hsz *= 2
    u = 1
    while u < PLANE_ROWS:
        levels.append((scaled(q8, [cpre[u]] * N_PLANES), scaled(k8, [csuf[u]] * N_PLANES)))
        u *= 2
    levels.append((lambda: q, lambda: k))

    scores = [[None] * n_pairs for _ in range(n_chunks)]
    for lvl, (level_q, level_k) in enumerate(levels):
        ql, kl = level_q(), level_k()
        for c in range(n_chunks):
            for p in range(n_pairs):
                lhs = chunk_rows(ql, c, pair_lanes(p)).astype(BF16)
                rhs = heads_block_diag(chunk_rows(kl, c, pair_lanes(p)).astype(BF16))
                s = _dot_nt(lhs, rhs) * masks_ref[lvl]
                scores[c][p] = s if scores[c][p] is None else scores[c][p] + s
        if lvl % 2 == 0:
            yield

    q_full = [x * cpre[PLANE_ROWS] for x in q8]
    k_full = [x * csuf[PLANE_ROWS] for x in k8]
    decay = pre8[N_PLANES - 1] * cpre[PLANE_ROWS]

    states = [state_ref[h] for h in range(HG_HEADS)]
    outs = []
    for c in range(n_chunks):
        last = (c + 1) * PLANE_ROWS - 1
        o_pairs = []
        for p in range(n_pairs):
            lanes = pair_lanes(p)
            vc = chunk_rows(v, c, lanes).astype(BF16)
            st = _block_diag(states[2 * p], states[2 * p + 1]).astype(BF16)
            o = (_dot_nt(chunk_rows(q_full, c, lanes).astype(BF16), st)
                 + _dot(scores[c][p].astype(BF16), heads_block_diag(vc)))
            upd = _dot_tn(vc, chunk_rows(k_full, c, lanes).astype(BF16))
            for i in range(2):
                h = 2 * p + i
                blk = slice(i * HG_DK, (i + 1) * HG_DK)
                states[h] = (states[h] * decay[last:last + 1, h * HG_DK:(h + 1) * HG_DK]
                             + upd[blk, blk])
            o_pairs.append(o)
        outs.append(jnp.concatenate(o_pairs, axis=1))
    for h in range(HG_HEADS):
        state_ref[h] = states[h]
    yield

    gate = planes(SLAB_G)
    for r in range(N_PLANES):
        o_r = jnp.concatenate([o[r * PLANE_ROWS:(r + 1) * PLANE_ROWS] for o in outs], axis=0)
        g_r = gate[r]
        for h in range(HG_HEADS):
            lanes = slice(h * HG_DV, (h + 1) * HG_DV)
            y = _rms_norm(o_r[:, lanes], norm_g) * (g_r[:, lanes] * jax.nn.sigmoid(g_r[:, lanes]))
            out_ref[h, pl.ds(r, plane_len, stride=N_PLANES), :] = y


def _swa_tile(slab_ref, seq_start, ctab, stab, gq, gk, seg_mean, sinks_ref, kpad_ref, vpad_ref,
              out_ref):
    tile = slab_ref.shape[1]
    n_blocks = tile // SW_BLOCK
    n_pairs = SW_HEADS // 2
    q = jnp.concatenate([slab_ref[SLAB_SQ + s] for s in range(SW_WIDTH // LANES)], axis=1)
    k = slab_ref[SLAB_SK]
    v = slab_ref[SLAB_SV]

    def head_norm(t, g, mean_mat):
        ms = _dot((t * t).astype(BF16), mean_mat)
        return t * lax.rsqrt(ms + EPS) * g

    qn = _rope(head_norm(q, gq, seg_mean[...]), ctab, stab) * (SW_HEAD_DIM ** -0.5)
    kn = _rope(head_norm(k, gk, seg_mean[:LANES, :LANES]), ctab, stab)

    lane = lax.broadcasted_iota(jnp.int32, (tile, LANES), 1)
    low = lane < SW_HEAD_DIM

    def padded(t):
        rolled = pltpu.roll(t, SW_HEAD_DIM, 1)
        zero = jnp.zeros_like(t)
        out = []
        for kv_head in range(SW_KV_HEADS):
            src_even, src_odd = (t, rolled) if kv_head == 0 else (rolled, t)
            out.append(jnp.where(low, src_even, zero).astype(BF16))
            out.append(jnp.where(low, zero, src_odd).astype(BF16))
        return out

    kpads, vpads = padded(kn), padded(v)
    qb = qn.astype(BF16)
    yield

    def block_operand(pads, carry_ref, blk, kv):
        parts = []
        for i in (kv, kv + 1):
            prev = carry_ref[i] if blk == 0 else pads[i][(blk - 1) * SW_BLOCK:blk * SW_BLOCK]
            parts += [prev, pads[i][blk * SW_BLOCK:(blk + 1) * SW_BLOCK]]
        return jnp.concatenate(parts, axis=0)

    units = [(blk, pair) for blk in range(n_blocks) for pair in range(n_pairs)]
    heads = [(blk, pair, half) for blk, pair in units for half in range(2)]
    kv_of = lambda pair: 2 * (pair // (n_pairs // SW_KV_HEADS))

    scores = {}
    for blk, pair in units:
        rows = slice(blk * SW_BLOCK, (blk + 1) * SW_BLOCK)
        scores[blk, pair] = _dot_nt(qb[rows, pair * LANES:(pair + 1) * LANES],
                                    block_operand(kpads, kpad_ref, blk, kv_of(pair)))

    yield
    qi = lax.broadcasted_iota(jnp.int32, (SW_BLOCK, SW_BLOCK), 0)
    kj = lax.broadcasted_iota(jnp.int32, (SW_BLOCK, SW_BLOCK), 1)
    from_prev = kj > qi
    start_bias = jnp.where(seq_start, -jnp.inf, 0.0)

    merged, sink = {}, {}
    for blk, pair, half in heads:
        s = scores[blk, pair]
        s_prev = s[:, (2 * half) * SW_BLOCK:(2 * half + 1) * SW_BLOCK]
        s_cur = s[:, (2 * half + 1) * SW_BLOCK:(2 * half + 2) * SW_BLOCK]
        if blk == 0:
            s_prev = s_prev + start_bias
        merged[blk, pair, half] = jnp.where(from_prev, s_prev, s_cur)
        sink[blk, pair, half] = sinks_ref[2 * pair + half]
    top = {u: jnp.maximum(jnp.max(merged[u], axis=-1, keepdims=True), sink[u]) for u in heads}
    prob = {u: jnp.exp(merged[u] - top[u]) for u in heads}
    inv = {u: 1.0 / (jnp.sum(prob[u], axis=-1, keepdims=True) + jnp.exp(sink[u] - top[u]))
           for u in heads}

    yield
    low_o = lax.broadcasted_iota(jnp.int32, (SW_BLOCK, LANES), 1) < SW_HEAD_DIM
    for blk, pair in units:
        ps = []
        for half in range(2):
            p = prob[blk, pair, half]
            zero = jnp.zeros_like(p)
            ps += [jnp.where(from_prev, p, zero).astype(BF16),
                   jnp.where(from_prev, zero, p).astype(BF16)]
        o = _dot(jnp.concatenate(ps, axis=1), block_operand(vpads, vpad_ref, blk, kv_of(pair)))
        o = o * jnp.where(low_o, inv[blk, pair, 0], inv[blk, pair, 1])
        out_ref[blk * SW_BLOCK:(blk + 1) * SW_BLOCK, pair * LANES:(pair + 1) * LANES] = (
            o.astype(out_ref.dtype))

    last = slice((n_blocks - 1) * SW_BLOCK, n_blocks * SW_BLOCK)
    for i in range(2 * SW_KV_HEADS):
        kpad_ref[i] = kpads[i][last]
        vpad_ref[i] = vpads[i][last]


def _project_pieces(x_ref, g1_ref, win_ref, slab_ref):
    xn = _rms_norm(x_ref[...], g1_ref[...]).astype(BF16)
    col_step = 4 * LANES

    def piece(c0):
        c1 = min(c0 + col_step, IN_WIDTH)
        proj = _dot(xn, win_ref[:, c0:c1])
        for s in range((c1 - c0) // LANES):
            slab_ref[c0 // LANES + s] = proj[:, s * LANES:(s + 1) * LANES]

    return [functools.partial(piece, c0) for c0 in range(0, IN_WIDTH, col_step)]


def _mix_pieces(slab_ref, seq_start, lb, sinks_ref, ctab_ref, stab_ref, hgn_ref, gq_ref, gk_ref,
                segm_ref, masks_ref, mix_ref, hg_ref, state_ref, kpad_ref, vpad_ref):
    yield from _swa_tile(slab_ref, seq_start, ctab_ref[...], stab_ref[...], gq_ref[...],
                         gk_ref[...], segm_ref, sinks_ref, kpad_ref, vpad_ref,
                         mix_ref.at[:, pl.ds(HG_WIDTH, SW_WIDTH)])
    yield
    yield from _hgrn2_tile(slab_ref, lb, hgn_ref[...], masks_ref, state_ref, hg_ref)
    for h in range(HG_HEADS):
        mix_ref[:, h * HG_DV:(h + 1) * HG_DV] = hg_ref[h].astype(BF16)


STEP_ORDER = "TMP TMP TMP TMP TM TMMP TMMP TM".replace(" ", "")


def _layer_kernel(sinks_ref, xp_ref, xt_ref, ctab_ref, stab_ref, g1_ref, win_ref, hlb_ref, hgn_ref,
                  gq_ref, gk_ref, segm_ref, masks_ref, wout_ref, g2_ref, wq_ref, xgq_ref, kmem_ref,
                  vmem_ref, wo_ref, g3_ref, up_ref, down_ref, out_ref,
                  slab_a, slab_b, mix_a, mix_b, hg_ref, state_ref, kpad_ref, vpad_ref,
                  *, tiles_per_seq):
    s = pl.program_id(0)
    seq_start = lax.rem(s + tiles_per_seq - 1, tiles_per_seq) == 0

    @pl.when(s == 0)
    def _():
        slab_b[...] = jnp.zeros_like(slab_b)
        mix_a[...] = jnp.zeros_like(mix_a)

    @pl.when(seq_start | (s == 0))
    def _():
        state_ref[...] = jnp.zeros_like(state_ref)
        kpad_ref[...] = jnp.zeros_like(kpad_ref)
        vpad_ref[...] = jnp.zeros_like(vpad_ref)

    hlb = hlb_ref[...]
    e = jnp.exp(hlb - jnp.max(hlb, axis=0, keepdims=True))
    lb = e[0:1, :] / jnp.sum(e, axis=0, keepdims=True)

    def step(project_slab, mix_slab, mix_out, tail_in):
        streams = {
            "P": iter(_project_pieces(xp_ref, g1_ref, win_ref, project_slab)),
            "M": _mix_pieces(mix_slab, seq_start, lb, sinks_ref, ctab_ref, stab_ref, hgn_ref,
                             gq_ref, gk_ref, segm_ref, masks_ref, mix_out, hg_ref, state_ref,
                             kpad_ref, vpad_ref),
            "T": _tail_stages(xt_ref, tail_in, wout_ref, g2_ref, wq_ref, xgq_ref, kmem_ref.at[0],
                              vmem_ref.at[0], wo_ref, g3_ref, up_ref, down_ref, out_ref),
        }
        for name in STEP_ORDER:
            piece = next(streams[name], None)
            if callable(piece):
                piece()
        for stream in streams.values():
            assert next(stream, "done") == "done"

    @pl.when(s % 2 == 0)
    def _():
        step(slab_a, slab_b, mix_b, mix_a)

    @pl.when(s % 2 == 1)
    def _():
        step(slab_b, slab_a, mix_a, mix_b)


def _layer(x, mem_k, mem_v, ctab, stab, sinks, g1, w_in, hlb, hgn, gq, gk, w_out, g2, wq, xgq, wo,
           g3, up, down):
    B, S, D = x.shape
    M = mem_k.shape[1]
    tile = MIX_TILE
    tiles_per_seq = S // tile
    n_tiles = B * tiles_per_seq
    seg_mean = jnp.asarray(np.kron(np.eye(SW_HEADS), np.full((SW_HEAD_DIM, SW_HEAD_DIM),
                                                              1.0 / SW_HEAD_DIM)), BF16)
    masks = jnp.asarray(_hgrn2_level_masks())

    def tile_index(lag):
        return lambda s: jnp.clip(s - lag, 0, n_tiles - 1)

    def tok(width, lag):
        index = tile_index(lag)
        return pl.BlockSpec((tile, width), lambda s: (index(s), 0))

    def const(shape):
        return pl.BlockSpec(shape, lambda s: (0,) * len(shape), pipeline_mode=pl.Buffered(1))

    tail_index = tile_index(2)
    per_seq = pl.BlockSpec((1, M, XA_WIDTH), lambda s: (tail_index(s) // tiles_per_seq, 0, 0))
    slab = pltpu.VMEM((N_SLABS, tile, LANES), F32)
    mix_buf = pltpu.VMEM((tile, MIX_WIDTH), BF16)
    x2 = x.reshape(B * S, D)
    out = pl.pallas_call(
        functools.partial(_layer_kernel, tiles_per_seq=tiles_per_seq),
        grid=(n_tiles + 2,),
        in_specs=[pl.BlockSpec(memory_space=pltpu.SMEM),
                  tok(D, 0), tok(D, 2), tok(LANES, 1), tok(LANES, 1),
                  const((1, D)), const(w_in.shape), const(hlb.shape), const((1, HG_DV)),
                  const((1, SW_WIDTH)), const((1, LANES)), const((SW_WIDTH, SW_WIDTH)),
                  const((N_LEVELS, HG_CHUNK, 2 * HG_CHUNK)),
                  const(w_out.shape), const((1, D)), const(wq.shape), const((1, XA_HEAD_DIM)),
                  per_seq, per_seq, const(wo.shape), const((1, D)), const(up.shape),
                  const(down.shape)],
        out_specs=tok(D, 2),
        out_shape=jax.ShapeDtypeStruct((B * S, D), F32),
        scratch_shapes=[slab, slab, mix_buf, mix_buf,
                        pltpu.VMEM((HG_HEADS, tile, LANES), F32),
                        pltpu.VMEM((HG_HEADS, HG_DV, HG_DK), F32),
                        pltpu.VMEM((2 * SW_KV_HEADS, SW_BLOCK, LANES), BF16),
                        pltpu.VMEM((2 * SW_KV_HEADS, SW_BLOCK, LANES), BF16)],
        compiler_params=pltpu.CompilerParams(
            dimension_semantics=("arbitrary",),
            vmem_limit_bytes=VMEM_LIMIT_BYTES),
        name="layer",
    )(sinks, x2, x2, ctab.reshape(B * S, LANES), stab.reshape(B * S, LANES), g1, w_in, hlb, hgn,
      gq, gk, seg_mean, masks, w_out, g2, wq, xgq, mem_k, mem_v, wo, g3, up, down)
    return out.reshape(B, S, D)


def _tail_stages(x_ref, mix_ref, wout_ref, g2_ref, wq_ref, gq_ref, k_ref, v_ref, wo_ref,
                 g3_ref, up_ref, down_ref, out_ref):
    h1 = x_ref[...] + _dot(mix_ref[...], wout_ref[...])
    yield

    q = _dot(_rms_norm(h1, g2_ref[...]).astype(BF16), wq_ref[...])
    yield

    heads = []
    for h in range(XA_HEADS):
        sl = slice(h * XA_HEAD_DIM, (h + 1) * XA_HEAD_DIM)
        qn = _rms_norm(q[:, sl], gq_ref[...]) * (XA_HEAD_DIM ** -0.5)
        s = _dot_nt(qn.astype(BF16), k_ref[:, sl])
        p = jnp.exp(s - jnp.max(s, axis=-1, keepdims=True))
        inv = 1.0 / jnp.sum(p, axis=-1, keepdims=True)
        heads.append(_dot(p.astype(BF16), v_ref[:, sl]) * inv)
    yield

    h2 = h1 + _dot(jnp.concatenate(heads, axis=1).astype(BF16), wo_ref[...])
    out_ref[...] = h2
    hn = _rms_norm(h2, g3_ref[...]).astype(BF16)
    for c0 in range(0, up_ref.shape[1], FF_CHUNK):
        yield
        a = jnp.maximum(_dot(hn, up_ref[:, c0:c0 + FF_CHUNK]), 0.0)
        out_ref[...] += _dot((a * a).astype(BF16), down_ref[c0:c0 + FF_CHUNK, :])


def kernel(x, mem, positions, norm1_g, w_in, hg_lower_bounds, hg_norm_g, sw_q_norm_g, sw_k_norm_g,
           sw_sinks, w_out, norm2_g, mem_norm_g, xa_wq, xa_wkv, xa_q_norm_g, xa_k_norm_g, xa_wo,
           norm3_g, mlp_up, mlp_down):
    depth = norm1_g.shape[0]
    assert depth == 1 and x.shape[1] % MIX_TILE == 0
    ctab, stab = _rope_tables(positions)
    h = x
    for l in range(depth):
        kmem, vmem = _mem_kv(mem, mem_norm_g[l][None], xa_wkv[l].astype(BF16), xa_k_norm_g[l][None])
        h = _layer(h, kmem, vmem, ctab, stab, sw_sinks[l], norm1_g[l][None], w_in[l].astype(BF16),
                   hg_lower_bounds, hg_norm_g[l][None], jnp.tile(sw_q_norm_g[l], SW_HEADS)[None],
                   jnp.tile(sw_k_norm_g[l], SW_KV_HEADS)[None], w_out[l].astype(BF16),
                   norm2_g[l][None], xa_wq[l].astype(BF16), xa_q_norm_g[l][None],
                   xa_wo[l].astype(BF16), norm3_g[l][None], mlp_up[l].astype(BF16),
                   mlp_down[l].astype(BF16))
    return h
```

```python
import functools

import numpy as np
import jax
import jax.numpy as jnp
from jax import lax
from jax.experimental import pallas as pl
from jax.experimental.pallas import tpu as pltpu

F32 = jnp.float32
BF16 = jnp.bfloat16

LANES = 128
VMEM_LIMIT_BYTES = 58 * 1024 * 1024

EPS = 1e-6

HG_HEADS = 4
HG_DK = 128
HG_DV = 128
HG_WIDTH = HG_HEADS * HG_DV
HG_KEY_WIDTH = HG_HEADS * HG_DK

SW_HEADS = 8
SW_KV_HEADS = 2
SW_HEAD_DIM = 64
SW_WIDTH = SW_HEADS * SW_HEAD_DIM
SW_KV_WIDTH = SW_KV_HEADS * SW_HEAD_DIM
WINDOW = 128
SW_BLOCK = 128
assert WINDOW == SW_BLOCK
ROPE_THETA = 500000.0
ROT_DIM = SW_HEAD_DIM // 4
ROT_HALF = ROT_DIM // 2

MIX_WIDTH = HG_WIDTH + SW_WIDTH
IN_WIDTH = 2 * HG_KEY_WIDTH + 2 * HG_WIDTH + SW_WIDTH + 2 * SW_KV_WIDTH

XA_HEADS = 4
XA_HEAD_DIM = 128
XA_WIDTH = XA_HEADS * XA_HEAD_DIM

N_SLABS = IN_WIDTH // LANES
SLAB_Q, SLAB_F, SLAB_I, SLAB_G = 0, 4, 8, 12
SLAB_SQ, SLAB_SK, SLAB_SV = 16, 20, 21

HG_CHUNK = 64
N_PLANES = 8
PLANE_ROWS = HG_CHUNK // N_PLANES
N_LEVELS = 7

MIX_TILE = 256
FF_CHUNK = 1024


def _mul(a, b):
    if a is None:
        return b
    if b is None:
        return a
    return a * b


def _rms_norm(x, g):
    return x * lax.rsqrt(jnp.mean(x * x, axis=-1, keepdims=True) + EPS) * g


def _dot(a, b):
    return jnp.dot(a, b, preferred_element_type=F32)


def _dot_nt(a, b):
    return lax.dot_general(a, b, (((1,), (1,)), ((), ())), preferred_element_type=F32)


def _dot_tn(a, b):
    return lax.dot_general(a, b, (((0,), (0,)), ((), ())), preferred_element_type=F32)


BF16_SPLIT_PARTS = 3


def _rope_table_kernel(pos_ref, invf_ref, ecos_ref, esin_ref, base_ref, ctab_ref, stab_ref):
    ang = pos_ref[0].astype(F32) * invf_ref[...]

    def spread(t, e_ref):
        parts, rest = [], t
        for _ in range(BF16_SPLIT_PARTS):
            piece = rest.astype(BF16).astype(F32)
            parts.append(piece)
            rest = rest - piece
        parts.append(jnp.zeros_like(t))
        return _dot_tn(jnp.concatenate(parts, axis=0).astype(BF16), e_ref[...])

    ctab_ref[0] = spread(jnp.cos(ang), ecos_ref) + base_ref[...]
    stab_ref[0] = spread(jnp.sin(ang), esin_ref)


def _rope_tables(positions):
    B, S = positions.shape
    inv_freq = ROPE_THETA ** (-(jnp.arange(ROT_HALF, dtype=F32) * 2.0 / ROT_DIM))
    dim = np.arange(LANES) % SW_HEAD_DIM
    freq = np.arange(ROT_HALF)[:, None]
    first, second = dim[None, :] == freq, dim[None, :] == freq + ROT_HALF
    pad = np.zeros((ROT_HALF, LANES))
    stack = lambda e: jnp.asarray(np.concatenate([e] * BF16_SPLIT_PARTS + [pad]), BF16)
    ecos = stack(first * 1.0 + second * 1.0)
    esin = stack(second * 1.0 - first * 1.0)
    base = jnp.asarray((dim >= ROT_DIM)[None, :], F32)
    rows = (BF16_SPLIT_PARTS + 1) * ROT_HALF
    const = lambda shape: pl.BlockSpec(shape, lambda b: (0, 0))
    return pl.pallas_call(
        _rope_table_kernel,
        grid=(B,),
        in_specs=[pl.BlockSpec((1, 1, S), lambda b: (b, 0, 0)), const((ROT_HALF, 1)),
                  const((rows, LANES)), const((rows, LANES)), const((1, LANES))],
        out_specs=[pl.BlockSpec((1, S, LANES), lambda b: (b, 0, 0))] * 2,
        out_shape=[jax.ShapeDtypeStruct((B, S, LANES), F32)] * 2,
        name="rope_tables",
    )(positions.reshape(B, 1, S), inv_freq.reshape(ROT_HALF, 1), ecos, esin, base)


def _rope(t, ctab, stab):
    width = t.shape[1]
    reps = width // LANES
    c = jnp.concatenate([ctab] * reps, axis=1) if reps > 1 else ctab
    s = jnp.concatenate([stab] * reps, axis=1) if reps > 1 else stab
    lane = lax.broadcasted_iota(jnp.int32, t.shape, 1)
    first_half = (lane % SW_HEAD_DIM) < ROT_HALF
    partner = jnp.where(first_half,
                        pltpu.roll(t, width - ROT_HALF, 1),
                        pltpu.roll(t, ROT_HALF, 1))
    return t * c + partner * s


def _mem_kv_kernel(mem_ref, g_ref, wkv_ref, gk_ref, k_ref, v_ref):
    mn = _rms_norm(mem_ref[0], g_ref[...]).astype(BF16)
    kv = _dot(mn, wkv_ref[...])
    for h in range(XA_HEADS):
        sl = slice(h * XA_HEAD_DIM, (h + 1) * XA_HEAD_DIM)
        k_ref[0, :, sl] = _rms_norm(kv[:, sl], gk_ref[...]).astype(BF16)
    v_ref[0] = kv[:, XA_WIDTH:].astype(BF16)


def _mem_kv(mem, g, wkv, gk):
    B, M, D = mem.shape
    return pl.pallas_call(
        _mem_kv_kernel,
        grid=(B,),
        in_specs=[pl.BlockSpec((1, M, D), lambda b: (b, 0, 0)),
                  pl.BlockSpec((1, D), lambda b: (0, 0)),
                  pl.BlockSpec((D, 2 * XA_WIDTH), lambda b: (0, 0)),
                  pl.BlockSpec((1, XA_HEAD_DIM), lambda b: (0, 0))],
        out_specs=[pl.BlockSpec((1, M, XA_WIDTH), lambda b: (b, 0, 0))] * 2,
        out_shape=[jax.ShapeDtypeStruct((B, M, XA_WIDTH), BF16)] * 2,
        name="mem_kv",
    )(mem, g, wkv, gk)


def _hgrn2_level_masks():
    p = np.arange(HG_CHUNK)
    tok = N_PLANES * (p % PLANE_ROWS) + p // PLANE_ROWS
    ti, tj = tok[:, None], tok[None, :]
    masks = []
    h = 1
    while h < HG_CHUNK:
        masks.append((ti // (2 * h) == tj // (2 * h)) & ((ti // h) % 2 == 1) & ((tj // h) % 2 == 0))
        h *= 2
    masks.append(ti == tj)
    masks = np.stack(masks).astype(np.float32)
    assert masks.shape[0] == N_LEVELS
    assert (masks.sum(0) == (tj <= ti)).all()
    return np.tile(masks, (1, 1, 2))


def _plane_block_products(f):
    pre = {1: list(f)}
    suf = {1: [None] * N_PLANES}
    h = 1
    while h < N_PLANES:
        p, s = pre[h], suf[h]
        new_p, new_s = [], []
        for r in range(N_PLANES):
            blk = r // h
            if blk % 2 == 1:
                new_p.append(_mul(p[r], p[blk * h - 1]))
                new_s.append(s[r])
            else:
                new_p.append(p[r])
                new_s.append(_mul(s[r], p[(blk + 2) * h - 1]))
        pre[2 * h], suf[2 * h] = new_p, new_s
        h *= 2
    return pre, suf


def _group_block_products(total):
    rows = total.shape[0]
    m_idx = lax.broadcasted_iota(jnp.int32, total.shape, 0) % PLANE_ROWS
    wpre, wsuf = [None], [None]
    for d in range(1, PLANE_ROWS):
        wpre.append(_mul(wpre[-1], pltpu.roll(total, d, 0)))
        wsuf.append(_mul(wsuf[-1], pltpu.roll(total, rows - d, 0)))
    cpre, csuf = {1: None}, {1: None}
    u = 2
    while u <= PLANE_ROWS:
        off = m_idx % u
        a = jnp.ones_like(total)
        b = jnp.ones_like(total)
        for d in range(1, u):
            a = jnp.where(off == d, wpre[d], a)
            b = jnp.where(off == u - 1 - d, wsuf[d], b)
        cpre[u], csuf[u] = a, b
        u *= 2
    return cpre, csuf


def _block_diag(a, b):
    za, zb = jnp.zeros_like(a), jnp.zeros_like(b)
    return jnp.concatenate([jnp.concatenate([a, zb], axis=1),
                            jnp.concatenate([za, b], axis=1)], axis=0)


def _hgrn2_tile(slab_ref, lb, norm_g, masks_ref, state_ref, out_ref):
    tile = slab_ref.shape[1]
    n_chunks = tile // HG_CHUNK
    plane_len = tile // N_PLANES
    pair_w = 2 * HG_DK
    n_pairs = HG_HEADS // 2

    def planes(slab0):
        return [jnp.concatenate(
            [slab_ref[slab0 + h, pl.ds(r, plane_len, stride=N_PLANES), :] for h in range(HG_HEADS)],
            axis=1) for r in range(N_PLANES)]

    def chunk_rows(ps, c, lanes=slice(None)):
        return jnp.concatenate([p[c * PLANE_ROWS:(c + 1) * PLANE_ROWS, lanes] for p in ps], axis=0)

    def pair_lanes(p):
        return slice(p * pair_w, (p + 1) * pair_w)

    def heads_block_diag(x):
        return _block_diag(x[:, :HG_DK], x[:, HG_DK:])

    f = [lb + (1.0 - lb) * jax.nn.sigmoid(x) for x in planes(SLAB_F)]
    q = planes(SLAB_Q)
    k = [1.0 - x for x in f]
    v = planes(SLAB_I)

    pre, suf = _plane_block_products(f)
    pre8, suf8 = pre[N_PLANES], suf[N_PLANES]
    cpre, csuf = _group_block_products(pre8[N_PLANES - 1])

    q8 = [_mul(a, b) for a, b in zip(q, pre8)]
    k8 = [_mul(a, b) for a, b in zip(k, suf8)]
    def scaled(xs, factors):
        return lambda rs: {r: _mul(xs[r], factors[r]) for r in rs}

    all_planes = list(range(N_PLANES))
    levels = []
    hsz = 1
    while hsz < N_PLANES:
        odd = [r for r in all_planes if (r // hsz) % 2 == 1]
        levels.append((odd, scaled(q, pre[hsz]), scaled(k, suf[hsz])))
        hsz *= 2
    u = 1
    while u < PLANE_ROWS:
        levels.append((all_planes, scaled(q8, [cpre[u]] * N_PLANES),
                       scaled(k8, [csuf[u]] * N_PLANES)))
        u *= 2
    levels.append((all_planes, scaled(q, [None] * N_PLANES), scaled(k, [None] * N_PLANES)))

    score_rows = [[[None] * N_PLANES for _ in range(n_pairs)] for _ in range(n_chunks)]
    for lvl, (q_planes, level_q, level_k) in enumerate(levels):
        ql = level_q(q_planes)
        keys = level_k(all_planes)
        kl = [keys[r] for r in all_planes]
        for c in range(n_chunks):
            rows = slice(c * PLANE_ROWS, (c + 1) * PLANE_ROWS)
            for p in range(n_pairs):
                lhs = jnp.concatenate([ql[r][rows, pair_lanes(p)] for r in q_planes],
                                      axis=0).astype(BF16)
                rhs = heads_block_diag(chunk_rows(kl, c, pair_lanes(p)).astype(BF16))
                s = _dot_nt(lhs, rhs)
                acc = score_rows[c][p]
                for i, r in enumerate(q_planes):
                    part = (s[i * PLANE_ROWS:(i + 1) * PLANE_ROWS]
                            * masks_ref[lvl, r * PLANE_ROWS:(r + 1) * PLANE_ROWS, :])
                    acc[r] = part if acc[r] is None else acc[r] + part
        if lvl % 2 == 0:
            yield
    scores = [[jnp.concatenate(score_rows[c][p], axis=0) for p in range(n_pairs)]
              for c in range(n_chunks)]

    q_full = [x * cpre[PLANE_ROWS] for x in q8]
    k_full = [x * csuf[PLANE_ROWS] for x in k8]
    decay = pre8[N_PLANES - 1] * cpre[PLANE_ROWS]

    states = [state_ref[h] for h in range(HG_HEADS)]
    outs = []
    for c in range(n_chunks):
        last = (c + 1) * PLANE_ROWS - 1
        o_pairs = []
        for p in range(n_pairs):
            lanes = pair_lanes(p)
            vc = chunk_rows(v, c, lanes).astype(BF16)
            st = _block_diag(states[2 * p], states[2 * p + 1]).astype(BF16)
            o = (_dot_nt(chunk_rows(q_full, c, lanes).astype(BF16), st)
                 + _dot(scores[c][p].astype(BF16), heads_block_diag(vc)))
            upd = _dot_tn(vc, chunk_rows(k_full, c, lanes).astype(BF16))
            for i in range(2):
                h = 2 * p + i
                blk = slice(i * HG_DK, (i + 1) * HG_DK)
                states[h] = (states[h] * decay[last:last + 1, h * HG_DK:(h + 1) * HG_DK]
                             + upd[blk, blk])
            o_pairs.append(o)
        outs.append(jnp.concatenate(o_pairs, axis=1))
    for h in range(HG_HEADS):
        state_ref[h] = states[h]
    yield

    gate = planes(SLAB_G)
    for r in range(N_PLANES):
        o_r = jnp.concatenate([o[r * PLANE_ROWS:(r + 1) * PLANE_ROWS] for o in outs], axis=0)
        g_r = gate[r]
        for h in range(HG_HEADS):
            lanes = slice(h * HG_DV, (h + 1) * HG_DV)
            y = _rms_norm(o_r[:, lanes], norm_g) * (g_r[:, lanes] * jax.nn.sigmoid(g_r[:, lanes]))
            out_ref[h, pl.ds(r, plane_len, stride=N_PLANES), :] = y


def _swa_tile(slab_ref, seq_start, ctab, stab, gq, gk, seg_mean, sinks_ref, kpad_ref, vpad_ref,
              out_ref):
    tile = slab_ref.shape[1]
    n_blocks = tile // SW_BLOCK
    n_pairs = SW_HEADS // 2
    q = jnp.concatenate([slab_ref[SLAB_SQ + s] for s in range(SW_WIDTH // LANES)], axis=1)
    k = slab_ref[SLAB_SK]
    v = slab_ref[SLAB_SV]

    def head_norm(t, g, mean_mat):
        ms = _dot((t * t).astype(BF16), mean_mat)
        return t * lax.rsqrt(ms + EPS) * g

    qn = _rope(head_norm(q, gq, seg_mean[...]), ctab, stab) * (SW_HEAD_DIM ** -0.5)
    kn = _rope(head_norm(k, gk, seg_mean[:LANES, :LANES]), ctab, stab)

    lane = lax.broadcasted_iota(jnp.int32, (tile, LANES), 1)
    low = lane < SW_HEAD_DIM

    def padded(t):
        rolled = pltpu.roll(t, SW_HEAD_DIM, 1)
        zero = jnp.zeros_like(t)
        out = []
        for kv_head in range(SW_KV_HEADS):
            src_even, src_odd = (t, rolled) if kv_head == 0 else (rolled, t)
            out.append(jnp.where(low, src_even, zero).astype(BF16))
            out.append(jnp.where(low, zero, src_odd).astype(BF16))
        return out

    kpads, vpads = padded(kn), padded(v)
    qb = qn.astype(BF16)
    yield

    def block_operand(pads, carry_ref, blk, kv):
        parts = []
        for i in (kv, kv + 1):
            prev = carry_ref[i] if blk == 0 else pads[i][(blk - 1) * SW_BLOCK:blk * SW_BLOCK]
            parts += [prev, pads[i][blk * SW_BLOCK:(blk + 1) * SW_BLOCK]]
        return jnp.concatenate(parts, axis=0)

    units = [(blk, pair) for blk in range(n_blocks) for pair in range(n_pairs)]
    heads = [(blk, pair, half) for blk, pair in units for half in range(2)]
    kv_of = lambda pair: 2 * (pair // (n_pairs // SW_KV_HEADS))

    scores = {}
    for blk, pair in units:
        rows = slice(blk * SW_BLOCK, (blk + 1) * SW_BLOCK)
        scores[blk, pair] = _dot_nt(qb[rows, pair * LANES:(pair + 1) * LANES],
                                    block_operand(kpads, kpad_ref, blk, kv_of(pair)))

    yield
    qi = lax.broadcasted_iota(jnp.int32, (SW_BLOCK, SW_BLOCK), 0)
    kj = lax.broadcasted_iota(jnp.int32, (SW_BLOCK, SW_BLOCK), 1)
    from_prev = kj > qi
    start_bias = jnp.where(seq_start, -jnp.inf, 0.0)

    merged, sink = {}, {}
    for blk, pair, half in heads:
        s = scores[blk, pair]
        s_prev = s[:, (2 * half) * SW_BLOCK:(2 * half + 1) * SW_BLOCK]
        s_cur = s[:, (2 * half + 1) * SW_BLOCK:(2 * half + 2) * SW_BLOCK]
        if blk == 0:
            s_prev = s_prev + start_bias
        merged[blk, pair, half] = jnp.where(from_prev, s_prev, s_cur)
        sink[blk, pair, half] = sinks_ref[2 * pair + half]
    top = {u: jnp.maximum(jnp.max(merged[u], axis=-1, keepdims=True), sink[u]) for u in heads}
    prob = {u: jnp.exp(merged[u] - top[u]) for u in heads}
    inv = {u: 1.0 / (jnp.sum(prob[u], axis=-1, keepdims=True) + jnp.exp(sink[u] - top[u]))
           for u in heads}

    yield
    low_o = lax.broadcasted_iota(jnp.int32, (SW_BLOCK, LANES), 1) < SW_HEAD_DIM
    for blk, pair in units:
        ps = []
        for half in range(2):
            p = prob[blk, pair, half]
            zero = jnp.zeros_like(p)
            ps += [jnp.where(from_prev, p, zero).astype(BF16),
                   jnp.where(from_prev, zero, p).astype(BF16)]
        o = _dot(jnp.concatenate(ps, axis=1), block_operand(vpads, vpad_ref, blk, kv_of(pair)))
        o = o * jnp.where(low_o, inv[blk, pair, 0], inv[blk, pair, 1])
        out_ref[blk * SW_BLOCK:(blk + 1) * SW_BLOCK, pair * LANES:(pair + 1) * LANES] = (
            o.astype(out_ref.dtype))

    last = slice((n_blocks - 1) * SW_BLOCK, n_blocks * SW_BLOCK)
    for i in range(2 * SW_KV_HEADS):
        kpad_ref[i] = kpads[i][last]
        vpad_ref[i] = vpads[i][last]


def _project_pieces(x_ref, g1_ref, win_ref, slab_ref):
    xn = _rms_norm(x_ref[...], g1_ref[...]).astype(BF16)
    col_step = 4 * LANES

    def piece(c0):
        c1 = min(c0 + col_step, IN_WIDTH)
        proj = _dot(xn, win_ref[:, c0:c1])
        for s in range((c1 - c0) // LANES):
            slab_ref[c0 // LANES + s] = proj[:, s * LANES:(s + 1) * LANES]

    return [functools.partial(piece, c0) for c0 in range(0, IN_WIDTH, col_step)]


def _mix_pieces(slab_ref, seq_start, lb, sinks_ref, ctab_ref, stab_ref, hgn_ref, gq_ref, gk_ref,
                segm_ref, masks_ref, mix_ref, hg_ref, state_ref, kpad_ref, vpad_ref):
    yield from _swa_tile(slab_ref, seq_start, ctab_ref[...], stab_ref[...], gq_ref[...],
                         gk_ref[...], segm_ref, sinks_ref, kpad_ref, vpad_ref,
                         mix_ref.at[:, pl.ds(HG_WIDTH, SW_WIDTH)])
    yield
    yield from _hgrn2_tile(slab_ref, lb, hgn_ref[...], masks_ref, state_ref, hg_ref)
    for h in range(HG_HEADS):
        mix_ref[:, h * HG_DV:(h + 1) * HG_DV] = hg_ref[h].astype(BF16)


STEP_ORDER = "FMAP MAP FMAP MAP FMMP MFMP MM".replace(" ", "")


def _layer_kernel(sinks_ref, xp_ref, xt_ref, ctab_ref, stab_ref, g1_ref, win_ref, hlb_ref, hgn_ref,
                  gq_ref, gk_ref, segm_ref, masks_ref, wout_ref, g2_ref, wq_ref, xgq_ref, kmem_ref,
                  vmem_ref, wo_ref, g3_ref, up_ref, down_ref, out_ref,
                  slab_a, slab_b, mix_a, mix_b, h2_a, h2_b, hn_a, hn_b, hg_ref, state_ref,
                  kpad_ref, vpad_ref, *, tiles_per_seq):
    s = pl.program_id(0)
    seq_start = lax.rem(s + tiles_per_seq - 1, tiles_per_seq) == 0

    @pl.when(s == 0)
    def _():
        slab_b[...] = jnp.zeros_like(slab_b)
        mix_a[...] = jnp.zeros_like(mix_a)
        h2_b[...] = jnp.zeros_like(h2_b)
        hn_b[...] = jnp.zeros_like(hn_b)

    @pl.when(seq_start | (s == 0))
    def _():
        state_ref[...] = jnp.zeros_like(state_ref)
        kpad_ref[...] = jnp.zeros_like(kpad_ref)
        vpad_ref[...] = jnp.zeros_like(vpad_ref)

    hlb = hlb_ref[...]
    e = jnp.exp(hlb - jnp.max(hlb, axis=0, keepdims=True))
    lb = e[0:1, :] / jnp.sum(e, axis=0, keepdims=True)

    def step(project_slab, mix_slab, mix_out, attn_in, attn_h2, attn_hn, mlp_h2, mlp_hn):
        streams = {
            "P": iter(_project_pieces(xp_ref, g1_ref, win_ref, project_slab)),
            "M": _mix_pieces(mix_slab, seq_start, lb, sinks_ref, ctab_ref, stab_ref, hgn_ref,
                             gq_ref, gk_ref, segm_ref, masks_ref, mix_out, hg_ref, state_ref,
                             kpad_ref, vpad_ref),
            "A": _attn_stages(xt_ref, attn_in, wout_ref, g2_ref, wq_ref, xgq_ref, kmem_ref.at[0],
                              vmem_ref.at[0], wo_ref, g3_ref, attn_h2, attn_hn),
            "F": _mlp_stages(mlp_h2, mlp_hn, up_ref, down_ref, out_ref),
        }
        for name in STEP_ORDER:
            piece = next(streams[name], None)
            if callable(piece):
                piece()
        for stream in streams.values():
            assert next(stream, "done") == "done"

    @pl.when(s % 2 == 0)
    def _():
        step(slab_a, slab_b, mix_b, mix_a, h2_a, hn_a, h2_b, hn_b)

    @pl.when(s % 2 == 1)
    def _():
        step(slab_b, slab_a, mix_a, mix_b, h2_b, hn_b, h2_a, hn_a)


def _layer(x, mem_k, mem_v, ctab, stab, sinks, g1, w_in, hlb, hgn, gq, gk, w_out, g2, wq, xgq, wo,
           g3, up, down):
    B, S, D = x.shape
    M = mem_k.shape[1]
    tile = MIX_TILE
    tiles_per_seq = S // tile
    n_tiles = B * tiles_per_seq
    seg_mean = jnp.asarray(np.kron(np.eye(SW_HEADS), np.full((SW_HEAD_DIM, SW_HEAD_DIM),
                                                              1.0 / SW_HEAD_DIM)), BF16)
    masks = jnp.asarray(_hgrn2_level_masks())

    def tile_index(lag):
        return lambda s: jnp.clip(s - lag, 0, n_tiles - 1)

    def tok(width, lag):
        index = tile_index(lag)
        return pl.BlockSpec((tile, width), lambda s: (index(s), 0))

    def const(shape):
        return pl.BlockSpec(shape, lambda s: (0,) * len(shape), pipeline_mode=pl.Buffered(1))

    tail_index = tile_index(2)
    per_seq = pl.BlockSpec((1, M, XA_WIDTH), lambda s: (tail_index(s) // tiles_per_seq, 0, 0))
    slab = pltpu.VMEM((N_SLABS, tile, LANES), F32)
    mix_buf = pltpu.VMEM((tile, MIX_WIDTH), BF16)
    x2 = x.reshape(B * S, D)
    out = pl.pallas_call(
        functools.partial(_layer_kernel, tiles_per_seq=tiles_per_seq),
        grid=(n_tiles + 3,),
        in_specs=[pl.BlockSpec(memory_space=pltpu.SMEM),
                  tok(D, 0), tok(D, 2), tok(LANES, 1), tok(LANES, 1),
                  const((1, D)), const(w_in.shape), const(hlb.shape), const((1, HG_DV)),
                  const((1, SW_WIDTH)), const((1, LANES)), const((SW_WIDTH, SW_WIDTH)),
                  const((N_LEVELS, HG_CHUNK, 2 * HG_CHUNK)),
                  const(w_out.shape), const((1, D)), const(wq.shape), const((1, XA_HEAD_DIM)),
                  per_seq, per_seq, const(wo.shape), const((1, D)), const(up.shape),
                  const(down.shape)],
        out_specs=tok(D, 3),
        out_shape=jax.ShapeDtypeStruct((B * S, D), F32),
        scratch_shapes=[slab, slab, mix_buf, mix_buf,
                        pltpu.VMEM((tile, D), F32), pltpu.VMEM((tile, D), F32),
                        pltpu.VMEM((tile, D), BF16), pltpu.VMEM((tile, D), BF16),
                        pltpu.VMEM((HG_HEADS, tile, LANES), F32),
                        pltpu.VMEM((HG_HEADS, HG_DV, HG_DK), F32),
                        pltpu.VMEM((2 * SW_KV_HEADS, SW_BLOCK, LANES), BF16),
                        pltpu.VMEM((2 * SW_KV_HEADS, SW_BLOCK, LANES), BF16)],
        compiler_params=pltpu.CompilerParams(
            dimension_semantics=("arbitrary",),
            vmem_limit_bytes=VMEM_LIMIT_BYTES),
        name="layer",
    )(sinks, x2, x2, ctab.reshape(B * S, LANES), stab.reshape(B * S, LANES), g1, w_in, hlb, hgn,
      gq, gk, seg_mean, masks, w_out, g2, wq, xgq, mem_k, mem_v, wo, g3, up, down)
    return out.reshape(B, S, D)


def _attn_stages(x_ref, mix_ref, wout_ref, g2_ref, wq_ref, gq_ref, k_ref, v_ref, wo_ref, g3_ref,
                 h2_ref, hn_ref):
    h1 = x_ref[...] + _dot(mix_ref[...], wout_ref[...])
    yield

    q = _dot(_rms_norm(h1, g2_ref[...]).astype(BF16), wq_ref[...])
    yield

    heads = []
    for h in range(XA_HEADS):
        sl = slice(h * XA_HEAD_DIM, (h + 1) * XA_HEAD_DIM)
        qn = _rms_norm(q[:, sl], gq_ref[...]) * (XA_HEAD_DIM ** -0.5)
        s = _dot_nt(qn.astype(BF16), k_ref[:, sl])
        p = jnp.exp(s - jnp.max(s, axis=-1, keepdims=True))
        inv = 1.0 / jnp.sum(p, axis=-1, keepdims=True)
        heads.append(_dot(p.astype(BF16), v_ref[:, sl]) * inv)
    yield

    h2 = h1 + _dot(jnp.concatenate(heads, axis=1).astype(BF16), wo_ref[...])
    h2_ref[...] = h2
    hn_ref[...] = _rms_norm(h2, g3_ref[...]).astype(BF16)


def _mlp_stages(h2_ref, hn_ref, up_ref, down_ref, out_ref):
    hn = hn_ref[...]
    for i, c0 in enumerate(range(0, up_ref.shape[1], FF_CHUNK)):
        if i:
            yield
        a = jnp.maximum(_dot(hn, up_ref[:, c0:c0 + FF_CHUNK]), 0.0)
        part = _dot((a * a).astype(BF16), down_ref[c0:c0 + FF_CHUNK, :])
        out_ref[...] = (h2_ref[...] if i == 0 else out_ref[...]) + part


def kernel(x, mem, positions, norm1_g, w_in, hg_lower_bounds, hg_norm_g, sw_q_norm_g, sw_k_norm_g,
           sw_sinks, w_out, norm2_g, mem_norm_g, xa_wq, xa_wkv, xa_q_norm_g, xa_k_norm_g, xa_wo,
           norm3_g, mlp_up, mlp_down):
    depth = norm1_g.shape[0]
    assert depth == 1 and x.shape[1] % MIX_TILE == 0
    ctab, stab = _rope_tables(positions)
    h = x
    for l in range(depth):
        kmem, vmem = _mem_kv(mem, mem_norm_g[l][None], xa_wkv[l].astype(BF16), xa_k_norm_g[l][None])
        h = _layer(h, kmem, vmem, ctab, stab, sw_sinks[l], norm1_g[l][None], w_in[l].astype(BF16),
                   hg_lower_bounds, hg_norm_g[l][None], jnp.tile(sw_q_norm_g[l], SW_HEADS)[None],
                   jnp.tile(sw_k_norm_g[l], SW_KV_HEADS)[None], w_out[l].astype(BF16),
                   norm2_g[l][None], xa_wq[l].astype(BF16), xa_q_norm_g[l][None],
                   xa_wo[l].astype(BF16), norm3_g[l][None], mlp_up[l].astype(BF16),
                   mlp_down[l].astype(BF16))
    return h
```

```python
import functools

import numpy as np
import jax
import jax.numpy as jnp
from jax import lax
from jax.experimental import pallas as pl
from jax.experimental.pallas import tpu as pltpu

F32 = jnp.float32
BF16 = jnp.bfloat16

LANES = 128
VMEM_LIMIT_BYTES = 56 * 1024 * 1024

EPS = 1e-6

HG_HEADS = 4
HG_DK = 128
HG_DV = 128
HG_WIDTH = HG_HEADS * HG_DV
HG_KEY_WIDTH = HG_HEADS * HG_DK

SW_HEADS = 8
SW_KV_HEADS = 2
SW_HEAD_DIM = 64
SW_WIDTH = SW_HEADS * SW_HEAD_DIM
SW_KV_WIDTH = SW_KV_HEADS * SW_HEAD_DIM
WINDOW = 128
SW_BLOCK = 128
assert WINDOW == SW_BLOCK
ROPE_THETA = 500000.0
ROT_DIM = SW_HEAD_DIM // 4
ROT_HALF = ROT_DIM // 2

MIX_WIDTH = HG_WIDTH + SW_WIDTH
IN_WIDTH = 2 * HG_KEY_WIDTH + 2 * HG_WIDTH + SW_WIDTH + 2 * SW_KV_WIDTH

XA_HEADS = 4
XA_HEAD_DIM = 128
XA_WIDTH = XA_HEADS * XA_HEAD_DIM

N_SLABS = IN_WIDTH // LANES
SLAB_Q, SLAB_F, SLAB_I, SLAB_G = 0, 4, 8, 12
SLAB_SQ, SLAB_SK, SLAB_SV = 16, 20, 21

HG_CHUNK = 64
N_PLANES = 8
PLANE_ROWS = HG_CHUNK // N_PLANES
N_LEVELS = 7

MIX_TILE = 256
FF_CHUNK = 1024


def _mul(a, b):
    if a is None:
        return b
    if b is None:
        return a
    return a * b


def _rms_norm(x, g):
    return x * lax.rsqrt(jnp.mean(x * x, axis=-1, keepdims=True) + EPS) * g


def _dot(a, b):
    return jnp.dot(a, b, preferred_element_type=F32)


def _dot_nt(a, b):
    return lax.dot_general(a, b, (((1,), (1,)), ((), ())), preferred_element_type=F32)


def _dot_tn(a, b):
    return lax.dot_general(a, b, (((0,), (0,)), ((), ())), preferred_element_type=F32)


BF16_SPLIT_PARTS = 3


def _rope_table_kernel(pos_ref, invf_ref, ecos_ref, esin_ref, base_ref, ctab_ref, stab_ref):
    ang = pos_ref[0].astype(F32) * invf_ref[...]

    def spread(t, e_ref):
        parts, rest = [], t
        for _ in range(BF16_SPLIT_PARTS):
            piece = rest.astype(BF16).astype(F32)
            parts.append(piece)
            rest = rest - piece
        parts.append(jnp.zeros_like(t))
        return _dot_tn(jnp.concatenate(parts, axis=0).astype(BF16), e_ref[...])

    ctab_ref[0] = spread(jnp.cos(ang), ecos_ref) + base_ref[...]
    stab_ref[0] = spread(jnp.sin(ang), esin_ref)


def _rope_tables(positions):
    B, S = positions.shape
    inv_freq = ROPE_THETA ** (-(jnp.arange(ROT_HALF, dtype=F32) * 2.0 / ROT_DIM))
    dim = np.arange(LANES) % SW_HEAD_DIM
    freq = np.arange(ROT_HALF)[:, None]
    first, second = dim[None, :] == freq, dim[None, :] == freq + ROT_HALF
    pad = np.zeros((ROT_HALF, LANES))
    stack = lambda e: jnp.asarray(np.concatenate([e] * BF16_SPLIT_PARTS + [pad]), BF16)
    ecos = stack(first * 1.0 + second * 1.0)
    esin = stack(second * 1.0 - first * 1.0)
    base = jnp.asarray((dim >= ROT_DIM)[None, :], F32)
    rows = (BF16_SPLIT_PARTS + 1) * ROT_HALF
    const = lambda shape: pl.BlockSpec(shape, lambda b: (0, 0))
    return pl.pallas_call(
        _rope_table_kernel,
        grid=(B,),
        in_specs=[pl.BlockSpec((1, 1, S), lambda b: (b, 0, 0)), const((ROT_HALF, 1)),
                  const((rows, LANES)), const((rows, LANES)), const((1, LANES))],
        out_specs=[pl.BlockSpec((1, S, LANES), lambda b: (b, 0, 0))] * 2,
        out_shape=[jax.ShapeDtypeStruct((B, S, LANES), F32)] * 2,
        name="rope_tables",
    )(positions.reshape(B, 1, S), inv_freq.reshape(ROT_HALF, 1), ecos, esin, base)


def _rope(t, ctab, stab):
    width = t.shape[1]
    reps = width // LANES
    c = jnp.concatenate([ctab] * reps, axis=1) if reps > 1 else ctab
    s = jnp.concatenate([stab] * reps, axis=1) if reps > 1 else stab
    lane = lax.broadcasted_iota(jnp.int32, t.shape, 1)
    first_half = (lane % SW_HEAD_DIM) < ROT_HALF
    partner = jnp.where(first_half,
                        pltpu.roll(t, width - ROT_HALF, 1),
                        pltpu.roll(t, ROT_HALF, 1))
    return t * c + partner * s


def _mem_kv_kernel(mem_ref, g_ref, wkv_ref, gk_ref, k_ref, v_ref):
    mn = _rms_norm(mem_ref[0], g_ref[...]).astype(BF16)
    kv = _dot(mn, wkv_ref[...])
    for h in range(XA_HEADS):
        sl = slice(h * XA_HEAD_DIM, (h + 1) * XA_HEAD_DIM)
        k_ref[0, :, sl] = _rms_norm(kv[:, sl], gk_ref[...]).astype(BF16)
    v_ref[0] = kv[:, XA_WIDTH:].astype(BF16)


def _mem_kv(mem, g, wkv, gk):
    B, M, D = mem.shape
    return pl.pallas_call(
        _mem_kv_kernel,
        grid=(B,),
        in_specs=[pl.BlockSpec((1, M, D), lambda b: (b, 0, 0)),
                  pl.BlockSpec((1, D), lambda b: (0, 0)),
                  pl.BlockSpec((D, 2 * XA_WIDTH), lambda b: (0, 0)),
                  pl.BlockSpec((1, XA_HEAD_DIM), lambda b: (0, 0))],
        out_specs=[pl.BlockSpec((1, M, XA_WIDTH), lambda b: (b, 0, 0))] * 2,
        out_shape=[jax.ShapeDtypeStruct((B, M, XA_WIDTH), BF16)] * 2,
        name="mem_kv",
    )(mem, g, wkv, gk)


def _hgrn2_level_masks():
    p = np.arange(HG_CHUNK)
    tok = N_PLANES * (p % PLANE_ROWS) + p // PLANE_ROWS
    ti, tj = tok[:, None], tok[None, :]
    masks = []
    h = 1
    while h < HG_CHUNK:
        masks.append((ti // (2 * h) == tj // (2 * h)) & ((ti // h) % 2 == 1) & ((tj // h) % 2 == 0))
        h *= 2
    masks.append(ti == tj)
    masks = np.stack(masks).astype(np.float32)
    assert masks.shape[0] == N_LEVELS
    assert (masks.sum(0) == (tj <= ti)).all()
    return np.tile(masks, (1, 1, 2))


def _plane_block_products(f):
    pre = {1: list(f)}
    suf = {1: [None] * N_PLANES}
    h = 1
    while h < N_PLANES:
        p, s = pre[h], suf[h]
        new_p, new_s = [], []
        for r in range(N_PLANES):
            blk = r // h
            if blk % 2 == 1:
                new_p.append(_mul(p[r], p[blk * h - 1]))
                new_s.append(s[r])
            else:
                new_p.append(p[r])
                new_s.append(_mul(s[r], p[(blk + 2) * h - 1]))
        pre[2 * h], suf[2 * h] = new_p, new_s
        h *= 2
    return pre, suf


def _group_block_products(total):
    rows = total.shape[0]
    m_idx = lax.broadcasted_iota(jnp.int32, total.shape, 0) % PLANE_ROWS
    wpre, wsuf = [None], [None]
    for d in range(1, PLANE_ROWS):
        wpre.append(_mul(wpre[-1], pltpu.roll(total, d, 0)))
        wsuf.append(_mul(wsuf[-1], pltpu.roll(total, rows - d, 0)))
    cpre, csuf = {1: None}, {1: None}
    u = 2
    while u <= PLANE_ROWS:
        off = m_idx % u
        a = jnp.ones_like(total)
        b = jnp.ones_like(total)
        for d in range(1, u):
            a = jnp.where(off == d, wpre[d], a)
            b = jnp.where(off == u - 1 - d, wsuf[d], b)
        cpre[u], csuf[u] = a, b
        u *= 2
    return cpre, csuf


def _block_diag(a, b):
    za, zb = jnp.zeros_like(a), jnp.zeros_like(b)
    return jnp.concatenate([jnp.concatenate([a, zb], axis=1),
                            jnp.concatenate([za, b], axis=1)], axis=0)


def _hgrn2_tile(slab_ref, lb, norm_g, masks_ref, state_ref, out_ref):
    tile = slab_ref.shape[1]
    n_chunks = tile // HG_CHUNK
    plane_len = tile // N_PLANES
    pair_w = 2 * HG_DK
    n_pairs = HG_HEADS // 2

    def planes(slab0):
        return [jnp.concatenate(
            [slab_ref[slab0 + h, pl.ds(r, plane_len, stride=N_PLANES), :] for h in range(HG_HEADS)],
            axis=1) for r in range(N_PLANES)]

    def chunk_rows(ps, c, lanes=slice(None)):
        return jnp.concatenate([p[c * PLANE_ROWS:(c + 1) * PLANE_ROWS, lanes] for p in ps], axis=0)

    def pair_lanes(p):
        return slice(p * pair_w, (p + 1) * pair_w)

    def heads_block_diag(x):
        return _block_diag(x[:, :HG_DK], x[:, HG_DK:])

    f = [lb + (1.0 - lb) * jax.nn.sigmoid(x) for x in planes(SLAB_F)]
    q = planes(SLAB_Q)
    k = [1.0 - x for x in f]
    v = planes(SLAB_I)

    pre, suf = _plane_block_products(f)
    pre8, suf8 = pre[N_PLANES], suf[N_PLANES]
    cpre, csuf = _group_block_products(pre8[N_PLANES - 1])

    q8 = [_mul(a, b) for a, b in zip(q, pre8)]
    k8 = [_mul(a, b) for a, b in zip(k, suf8)]
    def scaled(xs, factors):
        return lambda rs: {r: _mul(xs[r], factors[r]) for r in rs}

    all_planes = list(range(N_PLANES))
    levels = []
    hsz = 1
    while hsz < N_PLANES:
        odd = [r for r in all_planes if (r // hsz) % 2 == 1]
        levels.append((odd, scaled(q, pre[hsz]), scaled(k, suf[hsz])))
        hsz *= 2
    u = 1
    while u < PLANE_ROWS:
        levels.append((all_planes, scaled(q8, [cpre[u]] * N_PLANES),
                       scaled(k8, [csuf[u]] * N_PLANES)))
        u *= 2
    levels.append((all_planes, scaled(q, [None] * N_PLANES), scaled(k, [None] * N_PLANES)))

    score_rows = [[[None] * N_PLANES for _ in range(n_pairs)] for _ in range(n_chunks)]
    for lvl, (q_planes, level_q, level_k) in enumerate(levels):
        ql = level_q(q_planes)
        keys = level_k(all_planes)
        kl = [keys[r] for r in all_planes]
        for c in range(n_chunks):
            rows = slice(c * PLANE_ROWS, (c + 1) * PLANE_ROWS)
            for p in range(n_pairs):
                lhs = jnp.concatenate([ql[r][rows, pair_lanes(p)] for r in q_planes],
                                      axis=0).astype(BF16)
                rhs = heads_block_diag(chunk_rows(kl, c, pair_lanes(p)).astype(BF16))
                s = _dot_nt(lhs, rhs)
                acc = score_rows[c][p]
                for i, r in enumerate(q_planes):
                    part = (s[i * PLANE_ROWS:(i + 1) * PLANE_ROWS]
                            * masks_ref[lvl, r * PLANE_ROWS:(r + 1) * PLANE_ROWS, :])
                    acc[r] = part if acc[r] is None else acc[r] + part
        if lvl % 2 == 0:
            yield
    scores = [[jnp.concatenate(score_rows[c][p], axis=0) for p in range(n_pairs)]
              for c in range(n_chunks)]

    q_full = [x * cpre[PLANE_ROWS] for x in q8]
    k_full = [x * csuf[PLANE_ROWS] for x in k8]
    decay = pre8[N_PLANES - 1] * cpre[PLANE_ROWS]

    states = [state_ref[h] for h in range(HG_HEADS)]
    outs = []
    for c in range(n_chunks):
        last = (c + 1) * PLANE_ROWS - 1
        o_pairs = []
        for p in range(n_pairs):
            lanes = pair_lanes(p)
            vc = chunk_rows(v, c, lanes).astype(BF16)
            st = _block_diag(states[2 * p], states[2 * p + 1]).astype(BF16)
            o = (_dot_nt(chunk_rows(q_full, c, lanes).astype(BF16), st)
                 + _dot(scores[c][p].astype(BF16), heads_block_diag(vc)))
            upd = _dot_tn(vc, chunk_rows(k_full, c, lanes).astype(BF16))
            for i in range(2):
                h = 2 * p + i
                blk = slice(i * HG_DK, (i + 1) * HG_DK)
                states[h] = (states[h] * decay[last:last + 1, h * HG_DK:(h + 1) * HG_DK]
                             + upd[blk, blk])
            o_pairs.append(o)
        outs.append(jnp.concatenate(o_pairs, axis=1))
    for h in range(HG_HEADS):
        state_ref[h] = states[h]
    yield

    gate = planes(SLAB_G)
    for r in range(N_PLANES):
        o_r = jnp.concatenate([o[r * PLANE_ROWS:(r + 1) * PLANE_ROWS] for o in outs], axis=0)
        g_r = gate[r]
        for h in range(HG_HEADS):
            lanes = slice(h * HG_DV, (h + 1) * HG_DV)
            y = _rms_norm(o_r[:, lanes], norm_g) * (g_r[:, lanes] * jax.nn.sigmoid(g_r[:, lanes]))
            out_ref[h, pl.ds(r, plane_len, stride=N_PLANES), :] = y


def _swa_tile(slab_ref, seq_start, ctab, stab, gq, gk, seg_mean, sinks_ref, kpad_ref, vpad_ref,
              out_ref):
    tile = slab_ref.shape[1]
    n_blocks = tile // SW_BLOCK
    n_pairs = SW_HEADS // 2
    q = jnp.concatenate([slab_ref[SLAB_SQ + s] for s in range(SW_WIDTH // LANES)], axis=1)
    k = slab_ref[SLAB_SK]
    v = slab_ref[SLAB_SV]

    def head_norm(t, g, mean_mat):
        ms = _dot((t * t).astype(BF16), mean_mat)
        return t * lax.rsqrt(ms + EPS) * g

    qn = _rope(head_norm(q, gq, seg_mean[...]), ctab, stab) * (SW_HEAD_DIM ** -0.5)
    kn = _rope(head_norm(k, gk, seg_mean[:LANES, :LANES]), ctab, stab)

    lane = lax.broadcasted_iota(jnp.int32, (tile, LANES), 1)
    low = lane < SW_HEAD_DIM

    def padded(t):
        rolled = pltpu.roll(t, SW_HEAD_DIM, 1)
        zero = jnp.zeros_like(t)
        out = []
        for kv_head in range(SW_KV_HEADS):
            src_even, src_odd = (t, rolled) if kv_head == 0 else (rolled, t)
            out.append(jnp.where(low, src_even, zero).astype(BF16))
            out.append(jnp.where(low, zero, src_odd).astype(BF16))
        return out

    kpads, vpads = padded(kn), padded(v)
    qb = qn.astype(BF16)
    yield

    def block_operand(pads, carry_ref, blk, kv):
        parts = []
        for i in (kv, kv + 1):
            prev = carry_ref[i] if blk == 0 else pads[i][(blk - 1) * SW_BLOCK:blk * SW_BLOCK]
            parts += [prev, pads[i][blk * SW_BLOCK:(blk + 1) * SW_BLOCK]]
        return jnp.concatenate(parts, axis=0)

    units = [(blk, pair) for blk in range(n_blocks) for pair in range(n_pairs)]
    heads = [(blk, pair, half) for blk, pair in units for half in range(2)]
    kv_of = lambda pair: 2 * (pair // (n_pairs // SW_KV_HEADS))

    scores = {}
    for blk, pair in units:
        rows = slice(blk * SW_BLOCK, (blk + 1) * SW_BLOCK)
        scores[blk, pair] = _dot_nt(qb[rows, pair * LANES:(pair + 1) * LANES],
                                    block_operand(kpads, kpad_ref, blk, kv_of(pair)))

    yield
    qi = lax.broadcasted_iota(jnp.int32, (SW_BLOCK, SW_BLOCK), 0)
    kj = lax.broadcasted_iota(jnp.int32, (SW_BLOCK, SW_BLOCK), 1)
    from_prev = kj > qi
    start_bias = jnp.where(seq_start, -jnp.inf, 0.0)

    merged, sink = {}, {}
    for blk, pair, half in heads:
        s = scores[blk, pair]
        s_prev = s[:, (2 * half) * SW_BLOCK:(2 * half + 1) * SW_BLOCK]
        s_cur = s[:, (2 * half + 1) * SW_BLOCK:(2 * half + 2) * SW_BLOCK]
        if blk == 0:
            s_prev = s_prev + start_bias
        merged[blk, pair, half] = jnp.where(from_prev, s_prev, s_cur)
        sink[blk, pair, half] = sinks_ref[2 * pair + half]
    top = {u: jnp.maximum(jnp.max(merged[u], axis=-1, keepdims=True), sink[u]) for u in heads}
    prob = {u: jnp.exp(merged[u] - top[u]) for u in heads}
    inv = {u: 1.0 / (jnp.sum(prob[u], axis=-1, keepdims=True) + jnp.exp(sink[u] - top[u]))
           for u in heads}

    yield
    low_o = lax.broadcasted_iota(jnp.int32, (SW_BLOCK, LANES), 1) < SW_HEAD_DIM
    for blk, pair in units:
        ps = []
        for half in range(2):
            p = prob[blk, pair, half]
            zero = jnp.zeros_like(p)
            ps += [jnp.where(from_prev, p, zero).astype(BF16),
                   jnp.where(from_prev, zero, p).astype(BF16)]
        o = _dot(jnp.concatenate(ps, axis=1), block_operand(vpads, vpad_ref, blk, kv_of(pair)))
        o = o * jnp.where(low_o, inv[blk, pair, 0], inv[blk, pair, 1])
        out_ref[blk * SW_BLOCK:(blk + 1) * SW_BLOCK, pair * LANES:(pair + 1) * LANES] = (
            o.astype(out_ref.dtype))

    last = slice((n_blocks - 1) * SW_BLOCK, n_blocks * SW_BLOCK)
    for i in range(2 * SW_KV_HEADS):
        kpad_ref[i] = kpads[i][last]
        vpad_ref[i] = vpads[i][last]


def _project_pieces(x_ref, g1_ref, win_ref, slab_ref):
    xn = _rms_norm(x_ref[...], g1_ref[...]).astype(BF16)
    col_step = 4 * LANES

    def piece(c0):
        c1 = min(c0 + col_step, IN_WIDTH)
        proj = _dot(xn, win_ref[:, c0:c1])
        for s in range((c1 - c0) // LANES):
            slab_ref[c0 // LANES + s] = proj[:, s * LANES:(s + 1) * LANES]

    return [functools.partial(piece, c0) for c0 in range(0, IN_WIDTH, col_step)]


def _mix_pieces(slab_ref, seq_start, lb, sinks_ref, ctab_ref, stab_ref, hgn_ref, gq_ref, gk_ref,
                segm_ref, masks_ref, mix_ref, hg_ref, state_ref, kpad_ref, vpad_ref):
    yield from _swa_tile(slab_ref, seq_start, ctab_ref[...], stab_ref[...], gq_ref[...],
                         gk_ref[...], segm_ref, sinks_ref, kpad_ref, vpad_ref,
                         mix_ref.at[:, pl.ds(HG_WIDTH, SW_WIDTH)])
    yield
    yield from _hgrn2_tile(slab_ref, lb, hgn_ref[...], masks_ref, state_ref, hg_ref)
    for h in range(HG_HEADS):
        mix_ref[:, h * HG_DV:(h + 1) * HG_DV] = hg_ref[h].astype(BF16)


STEP_ORDER = "TMP TMP TMP TMP TM TMMP TMMP TM".replace(" ", "")


def _layer_kernel(sinks_ref, xp_ref, xt_ref, ctab_ref, stab_ref, g1_ref, win_ref, hlb_ref, hgn_ref,
                  gq_ref, gk_ref, segm_ref, masks_ref, wout_ref, g2_ref, wq_ref, xgq_ref, kmem_ref,
                  vmem_ref, wo_ref, g3_ref, up_ref, down_ref, out_ref,
                  slab_a, slab_b, mix_a, mix_b, hg_ref, state_ref, kpad_ref, vpad_ref,
                  *, tiles_per_seq):
    s = pl.program_id(0)
    seq_start = lax.rem(s + tiles_per_seq - 1, tiles_per_seq) == 0

    @pl.when(s == 0)
    def _():
        slab_b[...] = jnp.zeros_like(slab_b)
        mix_a[...] = jnp.zeros_like(mix_a)

    @pl.when(seq_start | (s == 0))
    def _():
        state_ref[...] = jnp.zeros_like(state_ref)
        kpad_ref[...] = jnp.zeros_like(kpad_ref)
        vpad_ref[...] = jnp.zeros_like(vpad_ref)

    hlb = hlb_ref[...]
    e = jnp.exp(hlb - jnp.max(hlb, axis=0, keepdims=True))
    lb = e[0:1, :] / jnp.sum(e, axis=0, keepdims=True)

    def step(project_slab, mix_slab, mix_out, tail_in):
        streams = {
            "P": iter(_project_pieces(xp_ref, g1_ref, win_ref, project_slab)),
            "M": _mix_pieces(mix_slab, seq_start, lb, sinks_ref, ctab_ref, stab_ref, hgn_ref,
                             gq_ref, gk_ref, segm_ref, masks_ref, mix_out, hg_ref, state_ref,
                             kpad_ref, vpad_ref),
            "T": _tail_stages(xt_ref, tail_in, wout_ref, g2_ref, wq_ref, xgq_ref, kmem_ref.at[0],
                              vmem_ref.at[0], wo_ref, g3_ref, up_ref, down_ref, out_ref),
        }
        for name in STEP_ORDER:
            piece = next(streams[name], None)
            if callable(piece):
                piece()
        for stream in streams.values():
            assert next(stream, "done") == "done"

    @pl.when(s % 2 == 0)
    def _():
        step(slab_a, slab_b, mix_b, mix_a)

    @pl.when(s % 2 == 1)
    def _():
        step(slab_b, slab_a, mix_a, mix_b)


def _layer(x, mem_k, mem_v, ctab, stab, sinks, g1, w_in, hlb, hgn, gq, gk, w_out, g2, wq, xgq, wo,
           g3, up, down):
    B, S, D = x.shape
    M = mem_k.shape[1]
    tile = MIX_TILE
    tiles_per_seq = S // tile
    n_tiles = B * tiles_per_seq
    seg_mean = jnp.asarray(np.kron(np.eye(SW_HEADS), np.full((SW_HEAD_DIM, SW_HEAD_DIM),
                                                              1.0 / SW_HEAD_DIM)), BF16)
    masks = jnp.asarray(_hgrn2_level_masks())

    def tile_index(lag):
        return lambda s: jnp.clip(s - lag, 0, n_tiles - 1)

    def tok(width, lag):
        index = tile_index(lag)
        return pl.BlockSpec((tile, width), lambda s: (index(s), 0))

    def const(shape):
        return pl.BlockSpec(shape, lambda s: (0,) * len(shape), pipeline_mode=pl.Buffered(1))

    tail_index = tile_index(2)
    per_seq = pl.BlockSpec((1, M, XA_WIDTH), lambda s: (tail_index(s) // tiles_per_seq, 0, 0))
    slab = pltpu.VMEM((N_SLABS, tile, LANES), F32)
    mix_buf = pltpu.VMEM((tile, MIX_WIDTH), BF16)
    x2 = x.reshape(B * S, D)
    out = pl.pallas_call(
        functools.partial(_layer_kernel, tiles_per_seq=tiles_per_seq),
        grid=(n_tiles + 2,),
        in_specs=[pl.BlockSpec(memory_space=pltpu.SMEM),
                  tok(D, 0), tok(D, 2), tok(LANES, 1), tok(LANES, 1),
                  const((1, D)), const(w_in.shape), const(hlb.shape), const((1, HG_DV)),
                  const((1, SW_WIDTH)), const((1, LANES)), const((SW_WIDTH, SW_WIDTH)),
                  const((N_LEVELS, HG_CHUNK, 2 * HG_CHUNK)),
                  const(w_out.shape), const((1, D)), const(wq.shape), const((1, XA_HEAD_DIM)),
                  per_seq, per_seq, const(wo.shape), const((1, D)), const(up.shape),
                  const(down.shape)],
        out_specs=tok(D, 2),
        out_shape=jax.ShapeDtypeStruct((B * S, D), F32),
        scratch_shapes=[slab, slab, mix_buf, mix_buf,
                        pltpu.VMEM((HG_HEADS, tile, LANES), F32),
                        pltpu.VMEM((HG_HEADS, HG_DV, HG_DK), F32),
                        pltpu.VMEM((2 * SW_KV_HEADS, SW_BLOCK, LANES), BF16),
                        pltpu.VMEM((2 * SW_KV_HEADS, SW_BLOCK, LANES), BF16)],
        compiler_params=pltpu.CompilerParams(
            dimension_semantics=("arbitrary",),
            vmem_limit_bytes=VMEM_LIMIT_BYTES),
        name="layer",
    )(sinks, x2, x2, ctab.reshape(B * S, LANES), stab.reshape(B * S, LANES), g1, w_in, hlb, hgn,
      gq, gk, seg_mean, masks, w_out, g2, wq, xgq, mem_k, mem_v, wo, g3, up, down)
    return out.reshape(B, S, D)


def _tail_stages(x_ref, mix_ref, wout_ref, g2_ref, wq_ref, gq_ref, k_ref, v_ref, wo_ref,
                 g3_ref, up_ref, down_ref, out_ref):
    h1 = x_ref[...] + _dot(mix_ref[...], wout_ref[...])
    yield

    q = _dot(_rms_norm(h1, g2_ref[...]).astype(BF16), wq_ref[...])
    yield

    heads = []
    for h in range(XA_HEADS):
        sl = slice(h * XA_HEAD_DIM, (h + 1) * XA_HEAD_DIM)
        qn = _rms_norm(q[:, sl], gq_ref[...]) * (XA_HEAD_DIM ** -0.5)
        s = _dot_nt(qn.astype(BF16), k_ref[:, sl])
        p = jnp.exp(s - jnp.max(s, axis=-1, keepdims=True))
        inv = 1.0 / jnp.sum(p, axis=-1, keepdims=True)
        heads.append(_dot(p.astype(BF16), v_ref[:, sl]) * inv)
    yield

    h2 = h1 + _dot(jnp.concatenate(heads, axis=1).astype(BF16), wo_ref[...])
    out_ref[...] = h2
    hn = _rms_norm(h2, g3_ref[...]).astype(BF16)
    for c0 in range(0, up_ref.shape[1], FF_CHUNK):
        yield
        a = jnp.maximum(_dot(hn, up_ref[:, c0:c0 + FF_CHUNK]), 0.0)
        out_ref[...] += _dot((a * a).astype(BF16), down_ref[c0:c0 + FF_CHUNK, :])


def kernel(x, mem, positions, norm1_g, w_in, hg_lower_bounds, hg_norm_g, sw_q_norm_g, sw_k_norm_g,
           sw_sinks, w_out, norm2_g, mem_norm_g, xa_wq, xa_wkv, xa_q_norm_g, xa_k_norm_g, xa_wo,
           norm3_g, mlp_up, mlp_down):
    depth = norm1_g.shape[0]
    assert depth == 1 and x.shape[1] % MIX_TILE == 0
    ctab, stab = _rope_tables(positions)
    h = x
    for l in range(depth):
        kmem, vmem = _mem_kv(mem, mem_norm_g[l][None], xa_wkv[l].astype(BF16), xa_k_norm_g[l][None])
        h = _layer(h, kmem, vmem, ctab, stab, sw_sinks[l], norm1_g[l][None], w_in[l].astype(BF16),
                   hg_lower_bounds, hg_norm_g[l][None], jnp.tile(sw_q_norm_g[l], SW_HEADS)[None],
                   jnp.tile(sw_k_norm_g[l], SW_KV_HEADS)[None], w_out[l].astype(BF16),
                   norm2_g[l][None], xa_wq[l].astype(BF16), xa_q_norm_g[l][None],
                   xa_wo[l].astype(BF16), norm3_g[l][None], mlp_up[l].astype(BF16),
                   mlp_down[l].astype(BF16))
    return h
```

```python
import functools

import numpy as np
import jax
import jax.numpy as jnp
from jax import lax
from jax.experimental import pallas as pl
from jax.experimental.pallas import tpu as pltpu

F32 = jnp.float32
BF16 = jnp.bfloat16

LANES = 128
VMEM_LIMIT_BYTES = 56 * 1024 * 1024

EPS = 1e-6

HG_HEADS = 4
HG_DK = 128
HG_DV = 128
HG_WIDTH = HG_HEADS * HG_DV
HG_KEY_WIDTH = HG_HEADS * HG_DK

SW_HEADS = 8
SW_KV_HEADS = 2
SW_HEAD_DIM = 64
SW_WIDTH = SW_HEADS * SW_HEAD_DIM
SW_KV_WIDTH = SW_KV_HEADS * SW_HEAD_DIM
WINDOW = 128
SW_BLOCK = 128
assert WINDOW == SW_BLOCK
ROPE_THETA = 500000.0
ROT_DIM = SW_HEAD_DIM // 4
ROT_HALF = ROT_DIM // 2

MIX_WIDTH = HG_WIDTH + SW_WIDTH
IN_WIDTH = 2 * HG_KEY_WIDTH + 2 * HG_WIDTH + SW_WIDTH + 2 * SW_KV_WIDTH

XA_HEADS = 4
XA_HEAD_DIM = 128
XA_WIDTH = XA_HEADS * XA_HEAD_DIM

N_SLABS = IN_WIDTH // LANES
SLAB_Q, SLAB_F, SLAB_I, SLAB_G = 0, 4, 8, 12
SLAB_SQ, SLAB_SK, SLAB_SV = 16, 20, 21

HG_CHUNK = 64
N_PLANES = 8
PLANE_ROWS = HG_CHUNK // N_PLANES
N_LEVELS = 7

MIX_TILE = 256
FF_CHUNK = 1024


def _mul(a, b):
    if a is None:
        return b
    if b is None:
        return a
    return a * b


def _rms_norm(x, g):
    return x * lax.rsqrt(jnp.mean(x * x, axis=-1, keepdims=True) + EPS) * g


def _dot(a, b):
    return jnp.dot(a, b, preferred_element_type=F32)


def _dot_nt(a, b):
    return lax.dot_general(a, b, (((1,), (1,)), ((), ())), preferred_element_type=F32)


def _dot_tn(a, b):
    return lax.dot_general(a, b, (((0,), (0,)), ((), ())), preferred_element_type=F32)


BF16_SPLIT_PARTS = 3


def _rope_table_kernel(pos_ref, invf_ref, ecos_ref, esin_ref, base_ref, ctab_ref, stab_ref):
    ang = pos_ref[0].astype(F32) * invf_ref[...]

    def spread(t, e_ref):
        parts, rest = [], t
        for _ in range(BF16_SPLIT_PARTS):
            piece = rest.astype(BF16).astype(F32)
            parts.append(piece)
            rest = rest - piece
        parts.append(jnp.zeros_like(t))
        return _dot_tn(jnp.concatenate(parts, axis=0).astype(BF16), e_ref[...])

    ctab_ref[0] = spread(jnp.cos(ang), ecos_ref) + base_ref[...]
    stab_ref[0] = spread(jnp.sin(ang), esin_ref)


def _rope_tables(positions):
    B, S = positions.shape
    inv_freq = ROPE_THETA ** (-(jnp.arange(ROT_HALF, dtype=F32) * 2.0 / ROT_DIM))
    dim = np.arange(LANES) % SW_HEAD_DIM
    freq = np.arange(ROT_HALF)[:, None]
    first, second = dim[None, :] == freq, dim[None, :] == freq + ROT_HALF
    pad = np.zeros((ROT_HALF, LANES))
    stack = lambda e: jnp.asarray(np.concatenate([e] * BF16_SPLIT_PARTS + [pad]), BF16)
    ecos = stack(first * 1.0 + second * 1.0)
    esin = stack(second * 1.0 - first * 1.0)
    base = jnp.asarray((dim >= ROT_DIM)[None, :], F32)
    rows = (BF16_SPLIT_PARTS + 1) * ROT_HALF
    const = lambda shape: pl.BlockSpec(shape, lambda b: (0, 0))
    return pl.pallas_call(
        _rope_table_kernel,
        grid=(B,),
        in_specs=[pl.BlockSpec((1, 1, S), lambda b: (b, 0, 0)), const((ROT_HALF, 1)),
                  const((rows, LANES)), const((rows, LANES)), const((1, LANES))],
        out_specs=[pl.BlockSpec((1, S, LANES), lambda b: (b, 0, 0))] * 2,
        out_shape=[jax.ShapeDtypeStruct((B, S, LANES), F32)] * 2,
        name="rope_tables",
    )(positions.reshape(B, 1, S), inv_freq.reshape(ROT_HALF, 1), ecos, esin, base)


def _rope(t, ctab, stab):
    width = t.shape[1]
    reps = width // LANES
    c = jnp.concatenate([ctab] * reps, axis=1) if reps > 1 else ctab
    s = jnp.concatenate([stab] * reps, axis=1) if reps > 1 else stab
    lane = lax.broadcasted_iota(jnp.int32, t.shape, 1)
    first_half = (lane % SW_HEAD_DIM) < ROT_HALF
    partner = jnp.where(first_half,
                        pltpu.roll(t, width - ROT_HALF, 1),
                        pltpu.roll(t, ROT_HALF, 1))
    return t * c + partner * s


def _mem_kv_kernel(mem_ref, g_ref, wkv_ref, gk_ref, k_ref, v_ref):
    mn = _rms_norm(mem_ref[0], g_ref[...]).astype(BF16)
    kv = _dot(mn, wkv_ref[...])
    for h in range(XA_HEADS):
        sl = slice(h * XA_HEAD_DIM, (h + 1) * XA_HEAD_DIM)
        k_ref[0, :, sl] = _rms_norm(kv[:, sl], gk_ref[...]).astype(BF16)
    v_ref[0] = kv[:, XA_WIDTH:].astype(BF16)


def _mem_kv(mem, g, wkv, gk):
    B, M, D = mem.shape
    return pl.pallas_call(
        _mem_kv_kernel,
        grid=(B,),
        in_specs=[pl.BlockSpec((1, M, D), lambda b: (b, 0, 0)),
                  pl.BlockSpec((1, D), lambda b: (0, 0)),
                  pl.BlockSpec((D, 2 * XA_WIDTH), lambda b: (0, 0)),
                  pl.BlockSpec((1, XA_HEAD_DIM), lambda b: (0, 0))],
        out_specs=[pl.BlockSpec((1, M, XA_WIDTH), lambda b: (b, 0, 0))] * 2,
        out_shape=[jax.ShapeDtypeStruct((B, M, XA_WIDTH), BF16)] * 2,
        name="mem_kv",
    )(mem, g, wkv, gk)


def _hgrn2_level_masks():
    p = np.arange(HG_CHUNK)
    tok = N_PLANES * (p % PLANE_ROWS) + p // PLANE_ROWS
    ti, tj = tok[:, None], tok[None, :]
    masks = []
    h = 1
    while h < HG_CHUNK:
        masks.append((ti // (2 * h) == tj // (2 * h)) & ((ti // h) % 2 == 1) & ((tj // h) % 2 == 0))
        h *= 2
    masks.append(ti == tj)
    masks = np.stack(masks).astype(np.float32)
    assert masks.shape[0] == N_LEVELS
    assert (masks.sum(0) == (tj <= ti)).all()
    return np.tile(masks, (1, 1, 2))


def _plane_block_products(f):
    pre = {1: list(f)}
    suf = {1: [None] * N_PLANES}
    h = 1
    while h < N_PLANES:
        p, s = pre[h], suf[h]
        new_p, new_s = [], []
        for r in range(N_PLANES):
            blk = r // h
            if blk % 2 == 1:
                new_p.append(_mul(p[r], p[blk * h - 1]))
                new_s.append(s[r])
            else:
                new_p.append(p[r])
                new_s.append(_mul(s[r], p[(blk + 2) * h - 1]))
        pre[2 * h], suf[2 * h] = new_p, new_s
        h *= 2
    return pre, suf


def _group_block_products(total):
    rows = total.shape[0]
    m_idx = lax.broadcasted_iota(jnp.int32, total.shape, 0) % PLANE_ROWS
    wpre, wsuf = [None], [None]
    for d in range(1, PLANE_ROWS):
        wpre.append(_mul(wpre[-1], pltpu.roll(total, d, 0)))
        wsuf.append(_mul(wsuf[-1], pltpu.roll(total, rows - d, 0)))
    cpre, csuf = {1: None}, {1: None}
    u = 2
    while u <= PLANE_ROWS:
        off = m_idx % u
        a = jnp.ones_like(total)
        b = jnp.ones_like(total)
        for d in range(1, u):
            a = jnp.where(off == d, wpre[d], a)
            b = jnp.where(off == u - 1 - d, wsuf[d], b)
        cpre[u], csuf[u] = a, b
        u *= 2
    return cpre, csuf


def _block_diag(a, b):
    za, zb = jnp.zeros_like(a), jnp.zeros_like(b)
    return jnp.concatenate([jnp.concatenate([a, zb], axis=1),
                            jnp.concatenate([za, b], axis=1)], axis=0)


def _hgrn2_tile(slab_ref, lb, norm_g, masks_ref, state_ref, out_ref):
    tile = slab_ref.shape[1]
    n_chunks = tile // HG_CHUNK
    plane_len = tile // N_PLANES
    pair_w = 2 * HG_DK
    n_pairs = HG_HEADS // 2

    def planes(slab0):
        return [jnp.concatenate(
            [slab_ref[slab0 + h, pl.ds(r, plane_len, stride=N_PLANES), :] for h in range(HG_HEADS)],
            axis=1) for r in range(N_PLANES)]

    def chunk_rows(ps, c, lanes=slice(None)):
        return jnp.concatenate([p[c * PLANE_ROWS:(c + 1) * PLANE_ROWS, lanes] for p in ps], axis=0)

    def pair_lanes(p):
        return slice(p * pair_w, (p + 1) * pair_w)

    def heads_block_diag(x):
        return _block_diag(x[:, :HG_DK], x[:, HG_DK:])

    lb_gap = 1.0 - lb
    f = [lb + lb_gap * jax.nn.sigmoid(x) for x in planes(SLAB_F)]
    q = planes(SLAB_Q)
    k = [1.0 - x for x in f]
    v = planes(SLAB_I)

    pre, suf = _plane_block_products(f)
    pre8, suf8 = pre[N_PLANES], suf[N_PLANES]
    cpre, csuf = _group_block_products(pre8[N_PLANES - 1])

    q8 = [_mul(a, b) for a, b in zip(q, pre8)]
    k8 = [_mul(a, b) for a, b in zip(k, suf8)]
    def scaled(xs, factors):
        return lambda: [_mul(a, b) for a, b in zip(xs, factors)]

    levels = []
    hsz = 1
    while hsz < N_PLANES:
        levels.append((scaled(q, pre[hsz]), scaled(k, suf[hsz])))
        hsz *= 2
    u = 1
    while u < PLANE_ROWS:
        levels.append((scaled(q8, [cpre[u]] * N_PLANES), scaled(k8, [csuf[u]] * N_PLANES)))
        u *= 2
    levels.append((lambda: q, lambda: k))

    scores = [[None] * n_pairs for _ in range(n_chunks)]
    for lvl, (level_q, level_k) in enumerate(levels):
        ql, kl = level_q(), level_k()
        for c in range(n_chunks):
            for p in range(n_pairs):
                lhs = chunk_rows(ql, c, pair_lanes(p)).astype(BF16)
                rhs = heads_block_diag(chunk_rows(kl, c, pair_lanes(p)).astype(BF16))
                s = _dot_nt(lhs, rhs) * masks_ref[lvl]
                scores[c][p] = s if scores[c][p] is None else scores[c][p] + s
        if lvl % 2 == 0:
            yield

    q_full = [x * cpre[PLANE_ROWS] for x in q8]
    k_full = [x * csuf[PLANE_ROWS] for x in k8]
    decay = pre8[N_PLANES - 1] * cpre[PLANE_ROWS]

    states = [state_ref[h] for h in range(HG_HEADS)]
    outs = []
    for c in range(n_chunks):
        last = (c + 1) * PLANE_ROWS - 1
        o_pairs = []
        for p in range(n_pairs):
            lanes = pair_lanes(p)
            vc = chunk_rows(v, c, lanes).astype(BF16)
            st = _block_diag(states[2 * p], states[2 * p + 1]).astype(BF16)
            o = (_dot_nt(chunk_rows(q_full, c, lanes).astype(BF16), st)
                 + _dot(scores[c][p].astype(BF16), heads_block_diag(vc)))
            upd = _dot_tn(vc, chunk_rows(k_full, c, lanes).astype(BF16))
            for i in range(2):
                h = 2 * p + i
                blk = slice(i * HG_DK, (i + 1) * HG_DK)
                states[h] = (states[h] * decay[last:last + 1, h * HG_DK:(h + 1) * HG_DK]
                             + upd[blk, blk])
            o_pairs.append(o)
        outs.append(jnp.concatenate(o_pairs, axis=1))
    for h in range(HG_HEADS):
        state_ref[h] = states[h]
    yield

    gate = planes(SLAB_G)
    for r in range(N_PLANES):
        o_r = jnp.concatenate([o[r * PLANE_ROWS:(r + 1) * PLANE_ROWS] for o in outs], axis=0)
        g_r = gate[r]
        for h in range(HG_HEADS):
            lanes = slice(h * HG_DV, (h + 1) * HG_DV)
            y = _rms_norm(o_r[:, lanes], norm_g) * (g_r[:, lanes] * jax.nn.sigmoid(g_r[:, lanes]))
            out_ref[h, pl.ds(r, plane_len, stride=N_PLANES), :] = y


def _swa_tile(slab_ref, seq_start, ctab, stab, gq, gk, seg_mean, sinks_ref, kpad_ref, vpad_ref,
              out_ref):
    tile = slab_ref.shape[1]
    n_blocks = tile // SW_BLOCK
    n_pairs = SW_HEADS // 2
    q = jnp.concatenate([slab_ref[SLAB_SQ + s] for s in range(SW_WIDTH // LANES)], axis=1)
    k = slab_ref[SLAB_SK]
    v = slab_ref[SLAB_SV]

    def head_norm(t, g, mean_mat):
        ms = _dot((t * t).astype(BF16), mean_mat)
        return t * lax.rsqrt(ms + EPS) * g

    qn = _rope(head_norm(q, gq, seg_mean[...]), ctab, stab) * (SW_HEAD_DIM ** -0.5)
    kn = _rope(head_norm(k, gk, seg_mean[:LANES, :LANES]), ctab, stab)

    lane = lax.broadcasted_iota(jnp.int32, (tile, LANES), 1)
    low = lane < SW_HEAD_DIM

    def padded(t):
        rolled = pltpu.roll(t, SW_HEAD_DIM, 1)
        zero = jnp.zeros_like(t)
        out = []
        for kv_head in range(SW_KV_HEADS):
            src_even, src_odd = (t, rolled) if kv_head == 0 else (rolled, t)
            out.append(jnp.where(low, src_even, zero).astype(BF16))
            out.append(jnp.where(low, zero, src_odd).astype(BF16))
        return out

    kpads, vpads = padded(kn), padded(v)
    qb = qn.astype(BF16)
    yield

    def block_operand(pads, carry_ref, blk, kv):
        parts = []
        for i in (kv, kv + 1):
            prev = carry_ref[i] if blk == 0 else pads[i][(blk - 1) * SW_BLOCK:blk * SW_BLOCK]
            parts += [prev, pads[i][blk * SW_BLOCK:(blk + 1) * SW_BLOCK]]
        return jnp.concatenate(parts, axis=0)

    units = [(blk, pair) for blk in range(n_blocks) for pair in range(n_pairs)]
    heads = [(blk, pair, half) for blk, pair in units for half in range(2)]
    kv_of = lambda pair: 2 * (pair // (n_pairs // SW_KV_HEADS))

    scores = {}
    for blk, pair in units:
        rows = slice(blk * SW_BLOCK, (blk + 1) * SW_BLOCK)
        scores[blk, pair] = _dot_nt(qb[rows, pair * LANES:(pair + 1) * LANES],
                                    block_operand(kpads, kpad_ref, blk, kv_of(pair)))

    yield
    qi = lax.broadcasted_iota(jnp.int32, (SW_BLOCK, SW_BLOCK), 0)
    kj = lax.broadcasted_iota(jnp.int32, (SW_BLOCK, SW_BLOCK), 1)
    from_prev = kj > qi
    start_bias = jnp.where(seq_start, -jnp.inf, 0.0)

    merged, sink = {}, {}
    for blk, pair, half in heads:
        s = scores[blk, pair]
        s_prev = s[:, (2 * half) * SW_BLOCK:(2 * half + 1) * SW_BLOCK]
        s_cur = s[:, (2 * half + 1) * SW_BLOCK:(2 * half + 2) * SW_BLOCK]
        if blk == 0:
            s_prev = s_prev + start_bias
        merged[blk, pair, half] = jnp.where(from_prev, s_prev, s_cur)
        sink[blk, pair, half] = sinks_ref[2 * pair + half]
    top = {u: jnp.maximum(jnp.max(merged[u], axis=-1, keepdims=True), sink[u]) for u in heads}
    prob = {u: jnp.exp(merged[u] - top[u]) for u in heads}
    inv = {u: 1.0 / (jnp.sum(prob[u], axis=-1, keepdims=True) + jnp.exp(sink[u] - top[u]))
           for u in heads}

    yield
    low_o = lax.broadcasted_iota(jnp.int32, (SW_BLOCK, LANES), 1) < SW_HEAD_DIM
    for blk, pair in units:
        ps = []
        for half in range(2):
            p = prob[blk, pair, half]
            zero = jnp.zeros_like(p)
            ps += [jnp.where(from_prev, p, zero).astype(BF16),
                   jnp.where(from_prev, zero, p).astype(BF16)]
        o = _dot(jnp.concatenate(ps, axis=1), block_operand(vpads, vpad_ref, blk, kv_of(pair)))
        o = o * jnp.where(low_o, inv[blk, pair, 0], inv[blk, pair, 1])
        out_ref[blk * SW_BLOCK:(blk + 1) * SW_BLOCK, pair * LANES:(pair + 1) * LANES] = (
            o.astype(out_ref.dtype))

    last = slice((n_blocks - 1) * SW_BLOCK, n_blocks * SW_BLOCK)
    for i in range(2 * SW_KV_HEADS):
        kpad_ref[i] = kpads[i][last]
        vpad_ref[i] = vpads[i][last]


def _project_pieces(x_ref, g1_ref, win_ref, slab_ref):
    xn = _rms_norm(x_ref[...], g1_ref[...]).astype(BF16)
    col_step = 4 * LANES

    def piece(c0):
        c1 = min(c0 + col_step, IN_WIDTH)
        proj = _dot(xn, win_ref[:, c0:c1])
        for s in range((c1 - c0) // LANES):
            slab_ref[c0 // LANES + s] = proj[:, s * LANES:(s + 1) * LANES]

    return [functools.partial(piece, c0) for c0 in range(0, IN_WIDTH, col_step)]


def _mix_pieces(slab_ref, seq_start, lb, sinks_ref, ctab_ref, stab_ref, hgn_ref, gq_ref, gk_ref,
                segm_ref, masks_ref, mix_ref, hg_ref, state_ref, kpad_ref, vpad_ref):
    yield from _swa_tile(slab_ref, seq_start, ctab_ref[...], stab_ref[...], gq_ref[...],
                         gk_ref[...], segm_ref, sinks_ref, kpad_ref, vpad_ref,
                         mix_ref.at[:, pl.ds(HG_WIDTH, SW_WIDTH)])
    yield
    yield from _hgrn2_tile(slab_ref, lb, hgn_ref[...], masks_ref, state_ref, hg_ref)
    for h in range(HG_HEADS):
        mix_ref[:, h * HG_DV:(h + 1) * HG_DV] = hg_ref[h].astype(BF16)


STEP_ORDER = "TMPP TMPP TMP TMP TMM TMM TM TM".replace(" ", "")


def _layer_kernel(sinks_ref, xp_ref, xt_ref, ctab_ref, stab_ref, g1_ref, win_ref, hlb_ref, hgn_ref,
                  gq_ref, gk_ref, segm_ref, masks_ref, wout_ref, g2_ref, wq_ref, xgq_ref, kmem_ref,
                  vmem_ref, wo_ref, g3_ref, up_ref, down_ref, out_ref,
                  slab_a, slab_b, mix_a, mix_b, hg_ref, state_ref, kpad_ref, vpad_ref,
                  *, tiles_per_seq):
    s = pl.program_id(0)
    seq_start = lax.rem(s + tiles_per_seq - 1, tiles_per_seq) == 0

    @pl.when(s == 0)
    def _():
        slab_b[...] = jnp.zeros_like(slab_b)
        mix_a[...] = jnp.zeros_like(mix_a)

    @pl.when(seq_start | (s == 0))
    def _():
        state_ref[...] = jnp.zeros_like(state_ref)
        kpad_ref[...] = jnp.zeros_like(kpad_ref)
        vpad_ref[...] = jnp.zeros_like(vpad_ref)

    hlb = hlb_ref[...]
    e = jnp.exp(hlb - jnp.max(hlb, axis=0, keepdims=True))
    lb = e[0:1, :] / jnp.sum(e, axis=0, keepdims=True)

    def step(project_slab, mix_slab, mix_out, tail_in):
        streams = {
            "P": iter(_project_pieces(xp_ref, g1_ref, win_ref, project_slab)),
            "M": _mix_pieces(mix_slab, seq_start, lb, sinks_ref, ctab_ref, stab_ref, hgn_ref,
                             gq_ref, gk_ref, segm_ref, masks_ref, mix_out, hg_ref, state_ref,
                             kpad_ref, vpad_ref),
            "T": _tail_stages(xt_ref, tail_in, wout_ref, g2_ref, wq_ref, xgq_ref, kmem_ref.at[0],
                              vmem_ref.at[0], wo_ref, g3_ref, up_ref, down_ref, out_ref),
        }
        for name in STEP_ORDER:
            piece = next(streams[name], None)
            if callable(piece):
                piece()
        for stream in streams.values():
            assert next(stream, "done") == "done"

    @pl.when(s % 2 == 0)
    def _():
        step(slab_a, slab_b, mix_b, mix_a)

    @pl.when(s % 2 == 1)
    def _():
        step(slab_b, slab_a, mix_a, mix_b)


def _layer(x, mem_k, mem_v, ctab, stab, sinks, g1, w_in, hlb, hgn, gq, gk, w_out, g2, wq, xgq, wo,
           g3, up, down):
    B, S, D = x.shape
    M = mem_k.shape[1]
    tile = MIX_TILE
    tiles_per_seq = S // tile
    n_tiles = B * tiles_per_seq
    seg_mean = jnp.asarray(np.kron(np.eye(SW_HEADS), np.full((SW_HEAD_DIM, SW_HEAD_DIM),
                                                              1.0 / SW_HEAD_DIM)), BF16)
    masks = jnp.asarray(_hgrn2_level_masks())

    def tile_index(lag):
        return lambda s: jnp.clip(s - lag, 0, n_tiles - 1)

    def tok(width, lag):
        index = tile_index(lag)
        return pl.BlockSpec((tile, width), lambda s: (index(s), 0))

    def const(shape):
        return pl.BlockSpec(shape, lambda s: (0,) * len(shape), pipeline_mode=pl.Buffered(1))

    tail_index = tile_index(2)
    per_seq = pl.BlockSpec((1, M, XA_WIDTH), lambda s: (tail_index(s) // tiles_per_seq, 0, 0))
    slab = pltpu.VMEM((N_SLABS, tile, LANES), F32)
    mix_buf = pltpu.VMEM((tile, MIX_WIDTH), BF16)
    x2 = x.reshape(B * S, D)
    out = pl.pallas_call(
        functools.partial(_layer_kernel, tiles_per_seq=tiles_per_seq),
        grid=(n_tiles + 2,),
        in_specs=[pl.BlockSpec(memory_space=pltpu.SMEM),
                  tok(D, 0), tok(D, 2), tok(LANES, 1), tok(LANES, 1),
                  const((1, D)), const(w_in.shape), const(hlb.shape), const((1, HG_DV)),
                  const((1, SW_WIDTH)), const((1, LANES)), const((SW_WIDTH, SW_WIDTH)),
                  const((N_LEVELS, HG_CHUNK, 2 * HG_CHUNK)),
                  const(w_out.shape), const((1, D)), const(wq.shape), const((1, XA_HEAD_DIM)),
                  per_seq, per_seq, const(wo.shape), const((1, D)), const(up.shape),
                  const(down.shape)],
        out_specs=tok(D, 2),
        out_shape=jax.ShapeDtypeStruct((B * S, D), F32),
        scratch_shapes=[slab, slab, mix_buf, mix_buf,
                        pltpu.VMEM((HG_HEADS, tile, LANES), F32),
                        pltpu.VMEM((HG_HEADS, HG_DV, HG_DK), F32),
                        pltpu.VMEM((2 * SW_KV_HEADS, SW_BLOCK, LANES), BF16),
                        pltpu.VMEM((2 * SW_KV_HEADS, SW_BLOCK, LANES), BF16)],
        compiler_params=pltpu.CompilerParams(
            dimension_semantics=("arbitrary",),
            vmem_limit_bytes=VMEM_LIMIT_BYTES),
        name="layer",
    )(sinks, x2, x2, ctab.reshape(B * S, LANES), stab.reshape(B * S, LANES), g1, w_in, hlb, hgn,
      gq, gk, seg_mean, masks, w_out, g2, wq, xgq, mem_k, mem_v, wo, g3, up, down)
    return out.reshape(B, S, D)


def _tail_stages(x_ref, mix_ref, wout_ref, g2_ref, wq_ref, gq_ref, k_ref, v_ref, wo_ref,
                 g3_ref, up_ref, down_ref, out_ref):
    h1 = x_ref[...] + _dot(mix_ref[...], wout_ref[...])
    yield

    q = _dot(_rms_norm(h1, g2_ref[...]).astype(BF16), wq_ref[...])
    yield

    heads = []
    for h in range(XA_HEADS):
        sl = slice(h * XA_HEAD_DIM, (h + 1) * XA_HEAD_DIM)
        qn = _rms_norm(q[:, sl], gq_ref[...]) * (XA_HEAD_DIM ** -0.5)
        s = _dot_nt(qn.astype(BF16), k_ref[:, sl])
        p = jnp.exp(s - jnp.max(s, axis=-1, keepdims=True))
        inv = 1.0 / jnp.sum(p, axis=-1, keepdims=True)
        heads.append(_dot(p.astype(BF16), v_ref[:, sl]) * inv)
    yield

    h2 = h1 + _dot(jnp.concatenate(heads, axis=1).astype(BF16), wo_ref[...])
    out_ref[...] = h2
    hn = _rms_norm(h2, g3_ref[...]).astype(BF16)
    for c0 in range(0, up_ref.shape[1], FF_CHUNK):
        yield
        a = jnp.maximum(_dot(hn, up_ref[:, c0:c0 + FF_CHUNK]), 0.0)
        out_ref[...] += _dot((a * a).astype(BF16), down_ref[c0:c0 + FF_CHUNK, :])


def kernel(x, mem, positions, norm1_g, w_in, hg_lower_bounds, hg_norm_g, sw_q_norm_g, sw_k_norm_g,
           sw_sinks, w_out, norm2_g, mem_norm_g, xa_wq, xa_wkv, xa_q_norm_g, xa_k_norm_g, xa_wo,
           norm3_g, mlp_up, mlp_down):
    depth = norm1_g.shape[0]
    assert depth == 1 and x.shape[1] % MIX_TILE == 0
    ctab, stab = _rope_tables(positions)
    h = x
    for l in range(depth):
        kmem, vmem = _mem_kv(mem, mem_norm_g[l][None], xa_wkv[l].astype(BF16), xa_k_norm_g[l][None])
        h = _layer(h, kmem, vmem, ctab, stab, sw_sinks[l], norm1_g[l][None], w_in[l].astype(BF16),
                   hg_lower_bounds, hg_norm_g[l][None], jnp.tile(sw_q_norm_g[l], SW_HEADS)[None],
                   jnp.tile(sw_k_norm_g[l], SW_KV_HEADS)[None], w_out[l].astype(BF16),
                   norm2_g[l][None], xa_wq[l].astype(BF16), xa_q_norm_g[l][None],
                   xa_wo[l].astype(BF16), norm3_g[l][None], mlp_up[l].astype(BF16),
                   mlp_down[l].astype(BF16))
    return h
```

```python
import functools

import numpy as np
import jax
import jax.numpy as jnp
from jax import lax
from jax.experimental import pallas as pl
from jax.experimental.pallas import tpu as pltpu

F32 = jnp.float32
BF16 = jnp.bfloat16

LANES = 128
VMEM_LIMIT_BYTES = 56 * 1024 * 1024

EPS = 1e-6

HG_HEADS = 4
HG_DK = 128
HG_DV = 128
HG_WIDTH = HG_HEADS * HG_DV
HG_KEY_WIDTH = HG_HEADS * HG_DK

SW_HEADS = 8
SW_KV_HEADS = 2
SW_HEAD_DIM = 64
SW_WIDTH = SW_HEADS * SW_HEAD_DIM
SW_KV_WIDTH = SW_KV_HEADS * SW_HEAD_DIM
WINDOW = 128
SW_BLOCK = 128
assert WINDOW == SW_BLOCK
ROPE_THETA = 500000.0
ROT_DIM = SW_HEAD_DIM // 4
ROT_HALF = ROT_DIM // 2

MIX_WIDTH = HG_WIDTH + SW_WIDTH
IN_WIDTH = 2 * HG_KEY_WIDTH + 2 * HG_WIDTH + SW_WIDTH + 2 * SW_KV_WIDTH

XA_HEADS = 4
XA_HEAD_DIM = 128
XA_WIDTH = XA_HEADS * XA_HEAD_DIM

N_SLABS = IN_WIDTH // LANES
SLAB_Q, SLAB_F, SLAB_I, SLAB_G = 0, 4, 8, 12
SLAB_SQ, SLAB_SK, SLAB_SV = 16, 20, 21

HG_CHUNK = 64
N_PLANES = 8
PLANE_ROWS = HG_CHUNK // N_PLANES
N_LEVELS = 3

MIX_TILE = 256
FF_CHUNK = 1024


def _mul(a, b):
    if a is None:
        return b
    if b is None:
        return a
    return a * b


def _rms_norm(x, g):
    return x * lax.rsqrt(jnp.mean(x * x, axis=-1, keepdims=True) + EPS) * g


def _dot(a, b):
    return jnp.dot(a, b, preferred_element_type=F32)


def _dot_nt(a, b):
    return lax.dot_general(a, b, (((1,), (1,)), ((), ())), preferred_element_type=F32)


def _dot_tn(a, b):
    return lax.dot_general(a, b, (((0,), (0,)), ((), ())), preferred_element_type=F32)


BF16_SPLIT_PARTS = 3


def _rope_table_kernel(pos_ref, invf_ref, ecos_ref, esin_ref, base_ref, ctab_ref, stab_ref):
    ang = pos_ref[0].astype(F32) * invf_ref[...]

    def spread(t, e_ref):
        parts, rest = [], t
        for _ in range(BF16_SPLIT_PARTS):
            piece = rest.astype(BF16).astype(F32)
            parts.append(piece)
            rest = rest - piece
        parts.append(jnp.zeros_like(t))
        return _dot_tn(jnp.concatenate(parts, axis=0).astype(BF16), e_ref[...])

    ctab_ref[0] = spread(jnp.cos(ang), ecos_ref) + base_ref[...]
    stab_ref[0] = spread(jnp.sin(ang), esin_ref)


def _rope_tables(positions):
    B, S = positions.shape
    inv_freq = ROPE_THETA ** (-(jnp.arange(ROT_HALF, dtype=F32) * 2.0 / ROT_DIM))
    dim = np.arange(LANES) % SW_HEAD_DIM
    freq = np.arange(ROT_HALF)[:, None]
    first, second = dim[None, :] == freq, dim[None, :] == freq + ROT_HALF
    pad = np.zeros((ROT_HALF, LANES))
    stack = lambda e: jnp.asarray(np.concatenate([e] * BF16_SPLIT_PARTS + [pad]), BF16)
    ecos = stack(first * 1.0 + second * 1.0)
    esin = stack(second * 1.0 - first * 1.0)
    base = jnp.asarray((dim >= ROT_DIM)[None, :], F32)
    rows = (BF16_SPLIT_PARTS + 1) * ROT_HALF
    const = lambda shape: pl.BlockSpec(shape, lambda b: (0, 0))
    return pl.pallas_call(
        _rope_table_kernel,
        grid=(B,),
        in_specs=[pl.BlockSpec((1, 1, S), lambda b: (b, 0, 0)), const((ROT_HALF, 1)),
                  const((rows, LANES)), const((rows, LANES)), const((1, LANES))],
        out_specs=[pl.BlockSpec((1, S, LANES), lambda b: (b, 0, 0))] * 2,
        out_shape=[jax.ShapeDtypeStruct((B, S, LANES), F32)] * 2,
        name="rope_tables",
    )(positions.reshape(B, 1, S), inv_freq.reshape(ROT_HALF, 1), ecos, esin, base)


def _rope(t, ctab, stab):
    width = t.shape[1]
    reps = width // LANES
    c = jnp.concatenate([ctab] * reps, axis=1) if reps > 1 else ctab
    s = jnp.concatenate([stab] * reps, axis=1) if reps > 1 else stab
    lane = lax.broadcasted_iota(jnp.int32, t.shape, 1)
    first_half = (lane % SW_HEAD_DIM) < ROT_HALF
    partner = jnp.where(first_half,
                        pltpu.roll(t, width - ROT_HALF, 1),
                        pltpu.roll(t, ROT_HALF, 1))
    return t * c + partner * s


def _mem_kv_kernel(mem_ref, g_ref, wkv_ref, gk_ref, k_ref, v_ref):
    mn = _rms_norm(mem_ref[0], g_ref[...]).astype(BF16)
    kv = _dot(mn, wkv_ref[...])
    for h in range(XA_HEADS):
        sl = slice(h * XA_HEAD_DIM, (h + 1) * XA_HEAD_DIM)
        k_ref[0, :, sl] = _rms_norm(kv[:, sl], gk_ref[...]).astype(BF16)
    v_ref[0] = kv[:, XA_WIDTH:].astype(BF16)


def _mem_kv(mem, g, wkv, gk):
    B, M, D = mem.shape
    return pl.pallas_call(
        _mem_kv_kernel,
        grid=(B,),
        in_specs=[pl.BlockSpec((1, M, D), lambda b: (b, 0, 0)),
                  pl.BlockSpec((1, D), lambda b: (0, 0)),
                  pl.BlockSpec((D, 2 * XA_WIDTH), lambda b: (0, 0)),
                  pl.BlockSpec((1, XA_HEAD_DIM), lambda b: (0, 0))],
        out_specs=[pl.BlockSpec((1, M, XA_WIDTH), lambda b: (b, 0, 0))] * 2,
        out_shape=[jax.ShapeDtypeStruct((B, M, XA_WIDTH), BF16)] * 2,
        name="mem_kv",
    )(mem, g, wkv, gk)


def _hgrn2_level_masks():
    p = np.arange(HG_CHUNK)
    tok = N_PLANES * (p % PLANE_ROWS) + p // PLANE_ROWS
    ti, tj = tok[:, None], tok[None, :]
    masks = []
    h = N_PLANES
    while h < HG_CHUNK:
        masks.append((ti // (2 * h) == tj // (2 * h)) & ((ti // h) % 2 == 1) & ((tj // h) % 2 == 0))
        h *= 2
    masks = np.stack(masks).astype(np.float32)
    assert masks.shape[0] == N_LEVELS
    in_group = (ti // N_PLANES == tj // N_PLANES) & (tj <= ti)
    assert (masks.sum(0) + in_group == (tj <= ti)).all()
    return np.tile(masks, (1, 1, 2))


def _plane_block_products(f):
    pre = {1: list(f)}
    suf = {1: [None] * N_PLANES}
    h = 1
    while h < N_PLANES:
        p, s = pre[h], suf[h]
        new_p, new_s = [], []
        for r in range(N_PLANES):
            blk = r // h
            if blk % 2 == 1:
                new_p.append(_mul(p[r], p[blk * h - 1]))
                new_s.append(s[r])
            else:
                new_p.append(p[r])
                new_s.append(_mul(s[r], p[(blk + 2) * h - 1]))
        pre[2 * h], suf[2 * h] = new_p, new_s
        h *= 2
    return pre, suf


def _group_block_products(total):
    rows = total.shape[0]
    m_idx = lax.broadcasted_iota(jnp.int32, total.shape, 0) % PLANE_ROWS
    wpre, wsuf = [None], [None]
    for d in range(1, PLANE_ROWS):
        wpre.append(_mul(wpre[-1], pltpu.roll(total, d, 0)))
        wsuf.append(_mul(wsuf[-1], pltpu.roll(total, rows - d, 0)))
    cpre, csuf = {1: None}, {1: None}
    u = 2
    while u <= PLANE_ROWS:
        off = m_idx % u
        a = jnp.ones_like(total)
        b = jnp.ones_like(total)
        for d in range(1, u):
            a = jnp.where(off == d, wpre[d], a)
            b = jnp.where(off == u - 1 - d, wsuf[d], b)
        cpre[u], csuf[u] = a, b
        u *= 2
    return cpre, csuf


def _block_diag(a, b):
    za, zb = jnp.zeros_like(a), jnp.zeros_like(b)
    return jnp.concatenate([jnp.concatenate([a, zb], axis=1),
                            jnp.concatenate([za, b], axis=1)], axis=0)


def _hgrn2_tile(slab_ref, lb, norm_g, masks_ref, state_ref, out_ref):
    tile = slab_ref.shape[1]
    n_chunks = tile // HG_CHUNK
    plane_len = tile // N_PLANES
    pair_w = 2 * HG_DK
    n_pairs = HG_HEADS // 2

    def planes(slab0):
        return [jnp.concatenate(
            [slab_ref[slab0 + h, pl.ds(r, plane_len, stride=N_PLANES), :] for h in range(HG_HEADS)],
            axis=1) for r in range(N_PLANES)]

    def chunk_rows(ps, c, lanes=slice(None)):
        return jnp.concatenate([p[c * PLANE_ROWS:(c + 1) * PLANE_ROWS, lanes] for p in ps], axis=0)

    def pair_lanes(p):
        return slice(p * pair_w, (p + 1) * pair_w)

    def heads_block_diag(x):
        return _block_diag(x[:, :HG_DK], x[:, HG_DK:])

    lb_gap = 1.0 - lb
    f = [lb + lb_gap * jax.nn.sigmoid(x) for x in planes(SLAB_F)]
    q = planes(SLAB_Q)
    k = [1.0 - x for x in f]
    v = planes(SLAB_I)

    pre, suf = _plane_block_products(f)
    pre8, suf8 = pre[N_PLANES], suf[N_PLANES]
    assert 2 ** (N_LEVELS - 1) * N_PLANES * 2 == HG_CHUNK
    cpre, csuf = _group_block_products(pre8[N_PLANES - 1])

    q8 = [_mul(a, b) for a, b in zip(q, pre8)]
    k8 = [_mul(a, b) for a, b in zip(k, suf8)]
    def per_head(x, fn):
        return jnp.concatenate(
            [fn(x[:, h * HG_DK:(h + 1) * HG_DK], h) for h in range(HG_HEADS)], axis=1)

    group_out = [None] * N_PLANES
    for rj in range(N_PLANES):
        decay = None
        for ri in range(rj, N_PLANES):
            if ri > rj:
                decay = _mul(decay, f[ri])
            weighted = _mul(q[ri] * k[rj], decay)
            part = per_head(weighted, lambda a, h: jnp.sum(a, axis=-1, keepdims=True)
                            * v[rj][:, h * HG_DV:(h + 1) * HG_DV])
            group_out[ri] = part if group_out[ri] is None else group_out[ri] + part
        if rj % 4 == 3:
            yield

    scores = [[None] * n_pairs for _ in range(n_chunks)]
    u = 1
    for lvl in range(N_LEVELS):
        ql = [_mul(x, cpre[u]) for x in q8]
        kl = [_mul(x, csuf[u]) for x in k8]
        for c in range(n_chunks):
            for p in range(n_pairs):
                lhs = chunk_rows(ql, c, pair_lanes(p)).astype(BF16)
                rhs = heads_block_diag(chunk_rows(kl, c, pair_lanes(p)).astype(BF16))
                s = _dot_nt(lhs, rhs) * masks_ref[lvl]
                scores[c][p] = s if scores[c][p] is None else scores[c][p] + s
        u *= 2
        if lvl != 1:
            yield

    q_full = [x * cpre[PLANE_ROWS] for x in q8]
    k_full = [x * csuf[PLANE_ROWS] for x in k8]
    decay = pre8[N_PLANES - 1] * cpre[PLANE_ROWS]

    states = [state_ref[h] for h in range(HG_HEADS)]
    outs = []
    for c in range(n_chunks):
        last = (c + 1) * PLANE_ROWS - 1
        o_pairs = []
        for p in range(n_pairs):
            lanes = pair_lanes(p)
            vc = chunk_rows(v, c, lanes).astype(BF16)
            st = _block_diag(states[2 * p], states[2 * p + 1]).astype(BF16)
            o = (_dot_nt(chunk_rows(q_full, c, lanes).astype(BF16), st)
                 + _dot(scores[c][p].astype(BF16), heads_block_diag(vc)))
            upd = _dot_tn(vc, chunk_rows(k_full, c, lanes).astype(BF16))
            for i in range(2):
                h = 2 * p + i
                blk = slice(i * HG_DK, (i + 1) * HG_DK)
                states[h] = (states[h] * decay[last:last + 1, h * HG_DK:(h + 1) * HG_DK]
                             + upd[blk, blk])
            o_pairs.append(o)
        outs.append(jnp.concatenate(o_pairs, axis=1))
    for h in range(HG_HEADS):
        state_ref[h] = states[h]
    yield

    gate = planes(SLAB_G)
    for r in range(N_PLANES):
        o_r = (jnp.concatenate([o[r * PLANE_ROWS:(r + 1) * PLANE_ROWS] for o in outs], axis=0)
               + group_out[r])
        g_r = gate[r]
        for h in range(HG_HEADS):
            lanes = slice(h * HG_DV, (h + 1) * HG_DV)
            y = _rms_norm(o_r[:, lanes], norm_g) * (g_r[:, lanes] * jax.nn.sigmoid(g_r[:, lanes]))
            out_ref[h, pl.ds(r, plane_len, stride=N_PLANES), :] = y


def _swa_tile(slab_ref, seq_start, ctab, stab, gq, gk, seg_mean, sinks_ref, kpad_ref, vpad_ref,
              out_ref):
    tile = slab_ref.shape[1]
    n_blocks = tile // SW_BLOCK
    n_pairs = SW_HEADS // 2
    q = jnp.concatenate([slab_ref[SLAB_SQ + s] for s in range(SW_WIDTH // LANES)], axis=1)
    k = slab_ref[SLAB_SK]
    v = slab_ref[SLAB_SV]

    def head_norm(t, g, mean_mat):
        ms = _dot((t * t).astype(BF16), mean_mat)
        return t * lax.rsqrt(ms + EPS) * g

    qn = _rope(head_norm(q, gq, seg_mean[...]), ctab, stab) * (SW_HEAD_DIM ** -0.5)
    kn = _rope(head_norm(k, gk, seg_mean[:LANES, :LANES]), ctab, stab)

    lane = lax.broadcasted_iota(jnp.int32, (tile, LANES), 1)
    low = lane < SW_HEAD_DIM

    def padded(t):
        rolled = pltpu.roll(t, SW_HEAD_DIM, 1)
        zero = jnp.zeros_like(t)
        out = []
        for kv_head in range(SW_KV_HEADS):
            src_even, src_odd = (t, rolled) if kv_head == 0 else (rolled, t)
            out.append(jnp.where(low, src_even, zero).astype(BF16))
            out.append(jnp.where(low, zero, src_odd).astype(BF16))
        return out

    kpads, vpads = padded(kn), padded(v)
    qb = qn.astype(BF16)
    yield

    def block_operand(pads, carry_ref, blk, kv):
        parts = []
        for i in (kv, kv + 1):
            prev = carry_ref[i] if blk == 0 else pads[i][(blk - 1) * SW_BLOCK:blk * SW_BLOCK]
            parts += [prev, pads[i][blk * SW_BLOCK:(blk + 1) * SW_BLOCK]]
        return jnp.concatenate(parts, axis=0)

    units = [(blk, pair) for blk in range(n_blocks) for pair in range(n_pairs)]
    heads = [(blk, pair, half) for blk, pair in units for half in range(2)]
    kv_of = lambda pair: 2 * (pair // (n_pairs // SW_KV_HEADS))

    scores = {}
    for blk, pair in units:
        rows = slice(blk * SW_BLOCK, (blk + 1) * SW_BLOCK)
        scores[blk, pair] = _dot_nt(qb[rows, pair * LANES:(pair + 1) * LANES],
                                    block_operand(kpads, kpad_ref, blk, kv_of(pair)))

    yield
    qi = lax.broadcasted_iota(jnp.int32, (SW_BLOCK, SW_BLOCK), 0)
    kj = lax.broadcasted_iota(jnp.int32, (SW_BLOCK, SW_BLOCK), 1)
    from_prev = kj > qi
    start_bias = jnp.where(seq_start, -jnp.inf, 0.0)

    merged, sink = {}, {}
    for blk, pair, half in heads:
        s = scores[blk, pair]
        s_prev = s[:, (2 * half) * SW_BLOCK:(2 * half + 1) * SW_BLOCK]
        s_cur = s[:, (2 * half + 1) * SW_BLOCK:(2 * half + 2) * SW_BLOCK]
        if blk == 0:
            s_prev = s_prev + start_bias
        merged[blk, pair, half] = jnp.where(from_prev, s_prev, s_cur)
        sink[blk, pair, half] = sinks_ref[2 * pair + half]
    top = {u: jnp.maximum(jnp.max(merged[u], axis=-1, keepdims=True), sink[u]) for u in heads}
    prob = {u: jnp.exp(merged[u] - top[u]) for u in heads}
    inv = {u: 1.0 / (jnp.sum(prob[u], axis=-1, keepdims=True) + jnp.exp(sink[u] - top[u]))
           for u in heads}

    yield
    low_o = lax.broadcasted_iota(jnp.int32, (SW_BLOCK, LANES), 1) < SW_HEAD_DIM
    for blk, pair in units:
        ps = []
        for half in range(2):
            p = prob[blk, pair, half]
            zero = jnp.zeros_like(p)
            ps += [jnp.where(from_prev, p, zero).astype(BF16),
                   jnp.where(from_prev, zero, p).astype(BF16)]
        o = _dot(jnp.concatenate(ps, axis=1), block_operand(vpads, vpad_ref, blk, kv_of(pair)))
        o = o * jnp.where(low_o, inv[blk, pair, 0], inv[blk, pair, 1])
        out_ref[blk * SW_BLOCK:(blk + 1) * SW_BLOCK, pair * LANES:(pair + 1) * LANES] = (
            o.astype(out_ref.dtype))

    last = slice((n_blocks - 1) * SW_BLOCK, n_blocks * SW_BLOCK)
    for i in range(2 * SW_KV_HEADS):
        kpad_ref[i] = kpads[i][last]
        vpad_ref[i] = vpads[i][last]


def _project_pieces(x_ref, g1_ref, win_ref, slab_ref):
    xn = _rms_norm(x_ref[...], g1_ref[...]).astype(BF16)
    col_step = 4 * LANES

    def piece(c0):
        c1 = min(c0 + col_step, IN_WIDTH)
        proj = _dot(xn, win_ref[:, c0:c1])
        for s in range((c1 - c0) // LANES):
            slab_ref[c0 // LANES + s] = proj[:, s * LANES:(s + 1) * LANES]

    return [functools.partial(piece, c0) for c0 in range(0, IN_WIDTH, col_step)]


def _mix_pieces(slab_ref, seq_start, lb, sinks_ref, ctab_ref, stab_ref, hgn_ref, gq_ref, gk_ref,
                segm_ref, masks_ref, mix_ref, hg_ref, state_ref, kpad_ref, vpad_ref):
    yield from _swa_tile(slab_ref, seq_start, ctab_ref[...], stab_ref[...], gq_ref[...],
                         gk_ref[...], segm_ref, sinks_ref, kpad_ref, vpad_ref,
                         mix_ref.at[:, pl.ds(HG_WIDTH, SW_WIDTH)])
    yield
    yield from _hgrn2_tile(slab_ref, lb, hgn_ref[...], masks_ref, state_ref, hg_ref)
    for h in range(HG_HEADS):
        mix_ref[:, h * HG_DV:(h + 1) * HG_DV] = hg_ref[h].astype(BF16)


STEP_ORDER = "TMPP TMPP TMP TMP TMM TMM TM TM".replace(" ", "")


def _layer_kernel(sinks_ref, xp_ref, xt_ref, ctab_ref, stab_ref, g1_ref, win_ref, hlb_ref, hgn_ref,
                  gq_ref, gk_ref, segm_ref, masks_ref, wout_ref, g2_ref, wq_ref, xgq_ref, kmem_ref,
                  vmem_ref, wo_ref, g3_ref, up_ref, down_ref, out_ref,
                  slab_a, slab_b, mix_a, mix_b, hg_ref, state_ref, kpad_ref, vpad_ref,
                  *, tiles_per_seq):
    s = pl.program_id(0)
    seq_start = lax.rem(s + tiles_per_seq - 1, tiles_per_seq) == 0

    @pl.when(s == 0)
    def _():
        slab_b[...] = jnp.zeros_like(slab_b)
        mix_a[...] = jnp.zeros_like(mix_a)

    @pl.when(seq_start | (s == 0))
    def _():
        state_ref[...] = jnp.zeros_like(state_ref)
        kpad_ref[...] = jnp.zeros_like(kpad_ref)
        vpad_ref[...] = jnp.zeros_like(vpad_ref)

    hlb = hlb_ref[...]
    e = jnp.exp(hlb - jnp.max(hlb, axis=0, keepdims=True))
    lb = e[0:1, :] / jnp.sum(e, axis=0, keepdims=True)

    def step(project_slab, mix_slab, mix_out, tail_in):
        streams = {
            "P": iter(_project_pieces(xp_ref, g1_ref, win_ref, project_slab)),
            "M": _mix_pieces(mix_slab, seq_start, lb, sinks_ref, ctab_ref, stab_ref, hgn_ref,
                             gq_ref, gk_ref, segm_ref, masks_ref, mix_out, hg_ref, state_ref,
                             kpad_ref, vpad_ref),
            "T": _tail_stages(xt_ref, tail_in, wout_ref, g2_ref, wq_ref, xgq_ref, kmem_ref.at[0],
                              vmem_ref.at[0], wo_ref, g3_ref, up_ref, down_ref, out_ref),
        }
        for name in STEP_ORDER:
            piece = next(streams[name], None)
            if callable(piece):
                piece()
        for stream in streams.values():
            assert next(stream, "done") == "done"

    @pl.when(s % 2 == 0)
    def _():
        step(slab_a, slab_b, mix_b, mix_a)

    @pl.when(s % 2 == 1)
    def _():
        step(slab_b, slab_a, mix_a, mix_b)


def _layer(x, mem_k, mem_v, ctab, stab, sinks, g1, w_in, hlb, hgn, gq, gk, w_out, g2, wq, xgq, wo,
           g3, up, down):
    B, S, D = x.shape
    M = mem_k.shape[1]
    tile = MIX_TILE
    tiles_per_seq = S // tile
    n_tiles = B * tiles_per_seq
    seg_mean = jnp.asarray(np.kron(np.eye(SW_HEADS), np.full((SW_HEAD_DIM, SW_HEAD_DIM),
                                                              1.0 / SW_HEAD_DIM)), BF16)
    masks = jnp.asarray(_hgrn2_level_masks())

    def tile_index(lag):
        return lambda s: jnp.clip(s - lag, 0, n_tiles - 1)

    def tok(width, lag):
        index = tile_index(lag)
        return pl.BlockSpec((tile, width), lambda s: (index(s), 0))

    def const(shape):
        return pl.BlockSpec(shape, lambda s: (0,) * len(shape), pipeline_mode=pl.Buffered(1))

    tail_index = tile_index(2)
    per_seq = pl.BlockSpec((1, M, XA_WIDTH), lambda s: (tail_index(s) // tiles_per_seq, 0, 0))
    slab = pltpu.VMEM((N_SLABS, tile, LANES), F32)
    mix_buf = pltpu.VMEM((tile, MIX_WIDTH), BF16)
    x2 = x.reshape(B * S, D)
    out = pl.pallas_call(
        functools.partial(_layer_kernel, tiles_per_seq=tiles_per_seq),
        grid=(n_tiles + 2,),
        in_specs=[pl.BlockSpec(memory_space=pltpu.SMEM),
                  tok(D, 0), tok(D, 2), tok(LANES, 1), tok(LANES, 1),
                  const((1, D)), const(w_in.shape), const(hlb.shape), const((1, HG_DV)),
                  const((1, SW_WIDTH)), const((1, LANES)), const((SW_WIDTH, SW_WIDTH)),
                  const((N_LEVELS, HG_CHUNK, 2 * HG_CHUNK)),
                  const(w_out.shape), const((1, D)), const(wq.shape), const((1, XA_HEAD_DIM)),
                  per_seq, per_seq, const(wo.shape), const((1, D)), const(up.shape),
                  const(down.shape)],
        out_specs=tok(D, 2),
        out_shape=jax.ShapeDtypeStruct((B * S, D), F32),
        scratch_shapes=[slab, slab, mix_buf, mix_buf,
                        pltpu.VMEM((HG_HEADS, tile, LANES), F32),
                        pltpu.VMEM((HG_HEADS, HG_DV, HG_DK), F32),
                        pltpu.VMEM((2 * SW_KV_HEADS, SW_BLOCK, LANES), BF16),
                        pltpu.VMEM((2 * SW_KV_HEADS, SW_BLOCK, LANES), BF16)],
        compiler_params=pltpu.CompilerParams(
            dimension_semantics=("arbitrary",),
            vmem_limit_bytes=VMEM_LIMIT_BYTES),
        name="layer",
    )(sinks, x2, x2, ctab.reshape(B * S, LANES), stab.reshape(B * S, LANES), g1, w_in, hlb, hgn,
      gq, gk, seg_mean, masks, w_out, g2, wq, xgq, mem_k, mem_v, wo, g3, up, down)
    return out.reshape(B, S, D)


def _tail_stages(x_ref, mix_ref, wout_ref, g2_ref, wq_ref, gq_ref, k_ref, v_ref, wo_ref,
                 g3_ref, up_ref, down_ref, out_ref):
    h1 = x_ref[...] + _dot(mix_ref[...], wout_ref[...])
    yield

    q = _dot(_rms_norm(h1, g2_ref[...]).astype(BF16), wq_ref[...])
    yield

    heads = []
    for h in range(XA_HEADS):
        sl = slice(h * XA_HEAD_DIM, (h + 1) * XA_HEAD_DIM)
        qn = _rms_norm(q[:, sl], gq_ref[...]) * (XA_HEAD_DIM ** -0.5)
        s = _dot_nt(qn.astype(BF16), k_ref[:, sl])
        p = jnp.exp(s - jnp.max(s, axis=-1, keepdims=True))
        inv = 1.0 / jnp.sum(p, axis=-1, keepdims=True)
        heads.append(_dot(p.astype(BF16), v_ref[:, sl]) * inv)
    yield

    h2 = h1 + _dot(jnp.concatenate(heads, axis=1).astype(BF16), wo_ref[...])
    out_ref[...] = h2
    hn = _rms_norm(h2, g3_ref[...]).astype(BF16)
    for c0 in range(0, up_ref.shape[1], FF_CHUNK):
        yield
        a = jnp.maximum(_dot(hn, up_ref[:, c0:c0 + FF_CHUNK]), 0.0)
        out_ref[...] += _dot((a * a).astype(BF16), down_ref[c0:c0 + FF_CHUNK, :])


def kernel(x, mem, positions, norm1_g, w_in, hg_lower_bounds, hg_norm_g, sw_q_norm_g, sw_k_norm_g,
           sw_sinks, w_out, norm2_g, mem_norm_g, xa_wq, xa_wkv, xa_q_norm_g, xa_k_norm_g, xa_wo,
           norm3_g, mlp_up, mlp_down):
    depth = norm1_g.shape[0]
    assert depth == 1 and x.shape[1] % MIX_TILE == 0
    ctab, stab = _rope_tables(positions)
    h = x
    for l in range(depth):
        kmem, vmem = _mem_kv(mem, mem_norm_g[l][None], xa_wkv[l].astype(BF16), xa_k_norm_g[l][None])
        h = _layer(h, kmem, vmem, ctab, stab, sw_sinks[l], norm1_g[l][None], w_in[l].astype(BF16),
                   hg_lower_bounds, hg_norm_g[l][None], jnp.tile(sw_q_norm_g[l], SW_HEADS)[None],
                   jnp.tile(sw_k_norm_g[l], SW_KV_HEADS)[None], w_out[l].astype(BF16),
                   norm2_g[l][None], xa_wq[l].astype(BF16), xa_q_norm_g[l][None],
                   xa_wo[l].astype(BF16), norm3_g[l][None], mlp_up[l].astype(BF16),
                   mlp_down[l].astype(BF16))
    return h
```

```python
import functools

import numpy as np
import jax
import jax.numpy as jnp
from jax import lax
from jax.experimental import pallas as pl
from jax.experimental.pallas import tpu as pltpu

F32 = jnp.float32
BF16 = jnp.bfloat16

LANES = 128
VMEM_LIMIT_BYTES = 56 * 1024 * 1024

EPS = 1e-6

HG_HEADS = 4
HG_DK = 128
HG_DV = 128
HG_WIDTH = HG_HEADS * HG_DV
HG_KEY_WIDTH = HG_HEADS * HG_DK

SW_HEADS = 8
SW_KV_HEADS = 2
SW_HEAD_DIM = 64
SW_WIDTH = SW_HEADS * SW_HEAD_DIM
SW_KV_WIDTH = SW_KV_HEADS * SW_HEAD_DIM
WINDOW = 128
SW_BLOCK = 128
assert WINDOW == SW_BLOCK
ROPE_THETA = 500000.0
ROT_DIM = SW_HEAD_DIM // 4
ROT_HALF = ROT_DIM // 2

MIX_WIDTH = HG_WIDTH + SW_WIDTH
IN_WIDTH = 2 * HG_KEY_WIDTH + 2 * HG_WIDTH + SW_WIDTH + 2 * SW_KV_WIDTH

XA_HEADS = 4
XA_HEAD_DIM = 128
XA_WIDTH = XA_HEADS * XA_HEAD_DIM

N_SLABS = IN_WIDTH // LANES
SLAB_Q, SLAB_F, SLAB_I, SLAB_G = 0, 4, 8, 12
SLAB_SQ, SLAB_SK, SLAB_SV = 16, 20, 21

HG_CHUNK = 64
N_PLANES = 8
PLANE_ROWS = HG_CHUNK // N_PLANES
N_LEVELS = 3

MIX_TILE = 256
FF_CHUNK = 1024


def _mul(a, b):
    if a is None:
        return b
    if b is None:
        return a
    return a * b


def _rms_norm(x, g):
    return x * lax.rsqrt(jnp.mean(x * x, axis=-1, keepdims=True) + EPS) * g


def _dot(a, b):
    return jnp.dot(a, b, preferred_element_type=F32)


def _dot_nt(a, b):
    return lax.dot_general(a, b, (((1,), (1,)), ((), ())), preferred_element_type=F32)


def _dot_tn(a, b):
    return lax.dot_general(a, b, (((0,), (0,)), ((), ())), preferred_element_type=F32)


BF16_SPLIT_PARTS = 3


def _rope_table_kernel(pos_ref, invf_ref, ecos_ref, esin_ref, base_ref, ctab_ref, stab_ref):
    ang = pos_ref[0].astype(F32) * invf_ref[...]

    def spread(t, e_ref):
        parts, rest = [], t
        for _ in range(BF16_SPLIT_PARTS):
            piece = rest.astype(BF16).astype(F32)
            parts.append(piece)
            rest = rest - piece
        parts.append(jnp.zeros_like(t))
        return _dot_tn(jnp.concatenate(parts, axis=0).astype(BF16), e_ref[...])

    ctab_ref[0] = spread(jnp.cos(ang), ecos_ref) + base_ref[...]
    stab_ref[0] = spread(jnp.sin(ang), esin_ref)


def _rope_tables(positions):
    B, S = positions.shape
    inv_freq = ROPE_THETA ** (-(jnp.arange(ROT_HALF, dtype=F32) * 2.0 / ROT_DIM))
    dim = np.arange(LANES) % SW_HEAD_DIM
    freq = np.arange(ROT_HALF)[:, None]
    first, second = dim[None, :] == freq, dim[None, :] == freq + ROT_HALF
    pad = np.zeros((ROT_HALF, LANES))
    stack = lambda e: jnp.asarray(np.concatenate([e] * BF16_SPLIT_PARTS + [pad]), BF16)
    ecos = stack(first * 1.0 + second * 1.0)
    esin = stack(second * 1.0 - first * 1.0)
    base = jnp.asarray((dim >= ROT_DIM)[None, :], F32)
    rows = (BF16_SPLIT_PARTS + 1) * ROT_HALF
    const = lambda shape: pl.BlockSpec(shape, lambda b: (0, 0))
    return pl.pallas_call(
        _rope_table_kernel,
        grid=(B,),
        in_specs=[pl.BlockSpec((1, 1, S), lambda b: (b, 0, 0)), const((ROT_HALF, 1)),
                  const((rows, LANES)), const((rows, LANES)), const((1, LANES))],
        out_specs=[pl.BlockSpec((1, S, LANES), lambda b: (b, 0, 0))] * 2,
        out_shape=[jax.ShapeDtypeStruct((B, S, LANES), F32)] * 2,
        name="rope_tables",
    )(positions.reshape(B, 1, S), inv_freq.reshape(ROT_HALF, 1), ecos, esin, base)


def _rope(t, ctab, stab):
    width = t.shape[1]
    reps = width // LANES
    c = jnp.concatenate([ctab] * reps, axis=1) if reps > 1 else ctab
    s = jnp.concatenate([stab] * reps, axis=1) if reps > 1 else stab
    lane = lax.broadcasted_iota(jnp.int32, t.shape, 1)
    first_half = (lane % SW_HEAD_DIM) < ROT_HALF
    partner = jnp.where(first_half,
                        pltpu.roll(t, width - ROT_HALF, 1),
                        pltpu.roll(t, ROT_HALF, 1))
    return t * c + partner * s


def _mem_kv_kernel(mem_ref, g_ref, wkv_ref, gk_ref, k_ref, v_ref):
    mn = _rms_norm(mem_ref[0], g_ref[...]).astype(BF16)
    kv = _dot(mn, wkv_ref[...])
    for h in range(XA_HEADS):
        sl = slice(h * XA_HEAD_DIM, (h + 1) * XA_HEAD_DIM)
        k_ref[0, :, sl] = _rms_norm(kv[:, sl], gk_ref[...]).astype(BF16)
    v_ref[0] = kv[:, XA_WIDTH:].astype(BF16)


def _mem_kv(mem, g, wkv, gk):
    B, M, D = mem.shape
    return pl.pallas_call(
        _mem_kv_kernel,
        grid=(B,),
        in_specs=[pl.BlockSpec((1, M, D), lambda b: (b, 0, 0)),
                  pl.BlockSpec((1, D), lambda b: (0, 0)),
                  pl.BlockSpec((D, 2 * XA_WIDTH), lambda b: (0, 0)),
                  pl.BlockSpec((1, XA_HEAD_DIM), lambda b: (0, 0))],
        out_specs=[pl.BlockSpec((1, M, XA_WIDTH), lambda b: (b, 0, 0))] * 2,
        out_shape=[jax.ShapeDtypeStruct((B, M, XA_WIDTH), BF16)] * 2,
        name="mem_kv",
    )(mem, g, wkv, gk)


def _hgrn2_level_masks():
    p = np.arange(HG_CHUNK)
    tok = N_PLANES * (p % PLANE_ROWS) + p // PLANE_ROWS
    ti, tj = tok[:, None], tok[None, :]
    masks = []
    h = N_PLANES
    while h < HG_CHUNK:
        masks.append((ti // (2 * h) == tj // (2 * h)) & ((ti // h) % 2 == 1) & ((tj // h) % 2 == 0))
        h *= 2
    masks = np.stack(masks).astype(np.float32)
    assert masks.shape[0] == N_LEVELS
    in_group = (ti // N_PLANES == tj // N_PLANES) & (tj <= ti)
    assert (masks.sum(0) + in_group == (tj <= ti)).all()
    return np.tile(masks, (1, 1, 2))


def _plane_block_products(f):
    pre = {1: list(f)}
    suf = {1: [None] * N_PLANES}
    h = 1
    while h < N_PLANES:
        p, s = pre[h], suf[h]
        new_p, new_s = [], []
        for r in range(N_PLANES):
            blk = r // h
            if blk % 2 == 1:
                new_p.append(_mul(p[r], p[blk * h - 1]))
                new_s.append(s[r])
            else:
                new_p.append(p[r])
                new_s.append(_mul(s[r], p[(blk + 2) * h - 1]))
        pre[2 * h], suf[2 * h] = new_p, new_s
        h *= 2
    return pre, suf


def _group_block_products(total):
    rows = total.shape[0]
    m_idx = lax.broadcasted_iota(jnp.int32, total.shape, 0) % PLANE_ROWS
    wpre, wsuf = [None], [None]
    for d in range(1, PLANE_ROWS):
        wpre.append(_mul(wpre[-1], pltpu.roll(total, d, 0)))
        wsuf.append(_mul(wsuf[-1], pltpu.roll(total, rows - d, 0)))
    cpre, csuf = {1: None}, {1: None}
    u = 2
    while u <= PLANE_ROWS:
        off = m_idx % u
        a = jnp.ones_like(total)
        b = jnp.ones_like(total)
        for d in range(1, u):
            a = jnp.where(off == d, wpre[d], a)
            b = jnp.where(off == u - 1 - d, wsuf[d], b)
        cpre[u], csuf[u] = a, b
        u *= 2
    return cpre, csuf


def _block_diag(a, b):
    za, zb = jnp.zeros_like(a), jnp.zeros_like(b)
    return jnp.concatenate([jnp.concatenate([a, zb], axis=1),
                            jnp.concatenate([za, b], axis=1)], axis=0)


def _hgrn2_tile(slab_ref, lb, norm_g, masks_ref, state_ref, out_ref):
    tile = slab_ref.shape[1]
    n_chunks = tile // HG_CHUNK
    plane_len = tile // N_PLANES
    pair_w = 2 * HG_DK
    n_pairs = HG_HEADS // 2

    def planes(slab0):
        return [jnp.concatenate(
            [slab_ref[slab0 + h, pl.ds(r, plane_len, stride=N_PLANES), :] for h in range(HG_HEADS)],
            axis=1) for r in range(N_PLANES)]

    def chunk_rows(ps, c, lanes=slice(None)):
        return jnp.concatenate([p[c * PLANE_ROWS:(c + 1) * PLANE_ROWS, lanes] for p in ps], axis=0)

    def pair_lanes(p):
        return slice(p * pair_w, (p + 1) * pair_w)

    def heads_block_diag(x):
        return _block_diag(x[:, :HG_DK], x[:, HG_DK:])

    lb_gap = 1.0 - lb
    f = [lb + lb_gap * jax.nn.sigmoid(x) for x in planes(SLAB_F)]
    q = planes(SLAB_Q)
    k = [1.0 - x for x in f]
    v = planes(SLAB_I)

    pre, suf = _plane_block_products(f)
    pre8, suf8 = pre[N_PLANES], suf[N_PLANES]
    assert 2 ** (N_LEVELS - 1) * N_PLANES * 2 == HG_CHUNK
    cpre, csuf = _group_block_products(pre8[N_PLANES - 1])

    q8 = [_mul(a, b) for a, b in zip(q, pre8)]
    k8 = [_mul(a, b) for a, b in zip(k, suf8)]
    def per_head(x, fn):
        return jnp.concatenate(
            [fn(x[:, h * HG_DK:(h + 1) * HG_DK], h) for h in range(HG_HEADS)], axis=1)

    group_out = [None] * N_PLANES
    for rj in range(N_PLANES):
        decay = None
        for ri in range(rj, N_PLANES):
            if ri > rj:
                decay = _mul(decay, f[ri])
            weighted = _mul(q[ri] * k[rj], decay)
            part = per_head(weighted, lambda a, h: jnp.sum(a, axis=-1, keepdims=True)
                            * v[rj][:, h * HG_DV:(h + 1) * HG_DV])
            group_out[ri] = part if group_out[ri] is None else group_out[ri] + part
        if rj % 4 == 3:
            yield

    scores = [[None] * n_pairs for _ in range(n_chunks)]
    u = 1
    for lvl in range(N_LEVELS):
        ql = [_mul(x, cpre[u]) for x in q8]
        kl = [_mul(x, csuf[u]) for x in k8]
        for c in range(n_chunks):
            for p in range(n_pairs):
                lhs = chunk_rows(ql, c, pair_lanes(p)).astype(BF16)
                rhs = heads_block_diag(chunk_rows(kl, c, pair_lanes(p)).astype(BF16))
                s = _dot_nt(lhs, rhs) * masks_ref[lvl]
                scores[c][p] = s if scores[c][p] is None else scores[c][p] + s
        u *= 2
        if lvl != 1:
            yield

    q_full = [x * cpre[PLANE_ROWS] for x in q8]
    k_full = [x * csuf[PLANE_ROWS] for x in k8]
    decay = pre8[N_PLANES - 1] * cpre[PLANE_ROWS]

    states = [state_ref[h] for h in range(HG_HEADS)]
    outs = []
    for c in range(n_chunks):
        last = (c + 1) * PLANE_ROWS - 1
        o_pairs = []
        for p in range(n_pairs):
            lanes = pair_lanes(p)
            vc = chunk_rows(v, c, lanes).astype(BF16)
            st = _block_diag(states[2 * p], states[2 * p + 1]).astype(BF16)
            o = (_dot_nt(chunk_rows(q_full, c, lanes).astype(BF16), st)
                 + _dot(scores[c][p].astype(BF16), heads_block_diag(vc)))
            upd = _dot_tn(vc, chunk_rows(k_full, c, lanes).astype(BF16))
            for i in range(2):
                h = 2 * p + i
                blk = slice(i * HG_DK, (i + 1) * HG_DK)
                states[h] = (states[h] * decay[last:last + 1, h * HG_DK:(h + 1) * HG_DK]
                             + upd[blk, blk])
            o_pairs.append(o)
        outs.append(jnp.concatenate(o_pairs, axis=1))
    for h in range(HG_HEADS):
        state_ref[h] = states[h]
    yield

    gate = planes(SLAB_G)
    for r in range(N_PLANES):
        o_r = (jnp.concatenate([o[r * PLANE_ROWS:(r + 1) * PLANE_ROWS] for o in outs], axis=0)
               + group_out[r])
        g_r = gate[r]
        for h in range(HG_HEADS):
            lanes = slice(h * HG_DV, (h + 1) * HG_DV)
            y = _rms_norm(o_r[:, lanes], norm_g) * (g_r[:, lanes] * jax.nn.sigmoid(g_r[:, lanes]))
            out_ref[h, pl.ds(r, plane_len, stride=N_PLANES), :] = y


def _swa_tile(slab_ref, seq_start, ctab, stab, gq, gk, seg_mean, sinks_ref, kpad_ref, vpad_ref,
              out_ref):
    tile = slab_ref.shape[1]
    n_blocks = tile // SW_BLOCK
    n_pairs = SW_HEADS // 2
    q = jnp.concatenate([slab_ref[SLAB_SQ + s] for s in range(SW_WIDTH // LANES)], axis=1)
    k = slab_ref[SLAB_SK]
    v = slab_ref[SLAB_SV]

    def head_norm(t, g, mean_mat):
        ms = _dot((t * t).astype(BF16), mean_mat)
        return t * lax.rsqrt(ms + EPS) * g

    qn = _rope(head_norm(q, gq, seg_mean[...]), ctab, stab) * (SW_HEAD_DIM ** -0.5)
    kn = _rope(head_norm(k, gk, seg_mean[:LANES, :LANES]), ctab, stab)

    lane = lax.broadcasted_iota(jnp.int32, (tile, LANES), 1)
    low = lane < SW_HEAD_DIM

    def padded(t):
        rolled = pltpu.roll(t, SW_HEAD_DIM, 1)
        zero = jnp.zeros_like(t)
        out = []
        for kv_head in range(SW_KV_HEADS):
            src_even, src_odd = (t, rolled) if kv_head == 0 else (rolled, t)
            out.append(jnp.where(low, src_even, zero).astype(BF16))
            out.append(jnp.where(low, zero, src_odd).astype(BF16))
        return out

    kpads, vpads = padded(kn), padded(v)
    qb = qn.astype(BF16)
    yield

    def block_operand(pads, carry_ref, blk, kv):
        parts = []
        for i in (kv, kv + 1):
            prev = carry_ref[i] if blk == 0 else pads[i][(blk - 1) * SW_BLOCK:blk * SW_BLOCK]
            parts += [prev, pads[i][blk * SW_BLOCK:(blk + 1) * SW_BLOCK]]
        return jnp.concatenate(parts, axis=0)

    units = [(blk, pair) for blk in range(n_blocks) for pair in range(n_pairs)]
    heads = [(blk, pair, half) for blk, pair in units for half in range(2)]
    kv_of = lambda pair: 2 * (pair // (n_pairs // SW_KV_HEADS))

    scores = {}
    for blk, pair in units:
        rows = slice(blk * SW_BLOCK, (blk + 1) * SW_BLOCK)
        scores[blk, pair] = _dot_nt(qb[rows, pair * LANES:(pair + 1) * LANES],
                                    block_operand(kpads, kpad_ref, blk, kv_of(pair)))

    yield
    qi = lax.broadcasted_iota(jnp.int32, (SW_BLOCK, SW_BLOCK), 0)
    kj = lax.broadcasted_iota(jnp.int32, (SW_BLOCK, SW_BLOCK), 1)
    from_prev = kj > qi
    start_bias = jnp.where(seq_start, -jnp.inf, 0.0)

    merged, sink = {}, {}
    for blk, pair, half in heads:
        s = scores[blk, pair]
        s_prev = s[:, (2 * half) * SW_BLOCK:(2 * half + 1) * SW_BLOCK]
        s_cur = s[:, (2 * half + 1) * SW_BLOCK:(2 * half + 2) * SW_BLOCK]
        if blk == 0:
            s_prev = s_prev + start_bias
        merged[blk, pair, half] = jnp.where(from_prev, s_prev, s_cur)
        sink[blk, pair, half] = sinks_ref[2 * pair + half]
    top = {u: jnp.maximum(jnp.max(merged[u], axis=-1, keepdims=True), sink[u]) for u in heads}
    prob = {u: jnp.exp(merged[u] - top[u]) for u in heads}
    inv = {u: 1.0 / (jnp.sum(prob[u], axis=-1, keepdims=True) + jnp.exp(sink[u] - top[u]))
           for u in heads}

    yield
    low_o = lax.broadcasted_iota(jnp.int32, (SW_BLOCK, LANES), 1) < SW_HEAD_DIM
    for blk, pair in units:
        ps = []
        for half in range(2):
            p = prob[blk, pair, half]
            zero = jnp.zeros_like(p)
            ps += [jnp.where(from_prev, p, zero).astype(BF16),
                   jnp.where(from_prev, zero, p).astype(BF16)]
        o = _dot(jnp.concatenate(ps, axis=1), block_operand(vpads, vpad_ref, blk, kv_of(pair)))
        o = o * jnp.where(low_o, inv[blk, pair, 0], inv[blk, pair, 1])
        out_ref[blk * SW_BLOCK:(blk + 1) * SW_BLOCK, pair * LANES:(pair + 1) * LANES] = (
            o.astype(out_ref.dtype))

    last = slice((n_blocks - 1) * SW_BLOCK, n_blocks * SW_BLOCK)
    for i in range(2 * SW_KV_HEADS):
        kpad_ref[i] = kpads[i][last]
        vpad_ref[i] = vpads[i][last]


def _project_pieces(x_ref, g1_ref, win_ref, slab_ref):
    xn = _rms_norm(x_ref[...], g1_ref[...]).astype(BF16)
    col_step = 2 * LANES

    def piece(c0):
        c1 = min(c0 + col_step, IN_WIDTH)
        proj = _dot(xn, win_ref[:, c0:c1])
        for s in range((c1 - c0) // LANES):
            slab_ref[c0 // LANES + s] = proj[:, s * LANES:(s + 1) * LANES]

    return [functools.partial(piece, c0) for c0 in range(0, IN_WIDTH, col_step)]


def _mix_pieces(slab_ref, seq_start, lb, sinks_ref, ctab_ref, stab_ref, hgn_ref, gq_ref, gk_ref,
                segm_ref, masks_ref, mix_ref, hg_ref, state_ref, kpad_ref, vpad_ref):
    yield from _swa_tile(slab_ref, seq_start, ctab_ref[...], stab_ref[...], gq_ref[...],
                         gk_ref[...], segm_ref, sinks_ref, kpad_ref, vpad_ref,
                         mix_ref.at[:, pl.ds(HG_WIDTH, SW_WIDTH)])
    yield
    yield from _hgrn2_tile(slab_ref, lb, hgn_ref[...], masks_ref, state_ref, hg_ref)
    for h in range(HG_HEADS):
        mix_ref[:, h * HG_DV:(h + 1) * HG_DV] = hg_ref[h].astype(BF16)


STEP_ORDER = "TMPPP TMPPP TMPP TMPP TMMP TMM TM TM".replace(" ", "")


def _layer_kernel(sinks_ref, xp_ref, xt_ref, ctab_ref, stab_ref, g1_ref, win_ref, hlb_ref, hgn_ref,
                  gq_ref, gk_ref, segm_ref, masks_ref, wout_ref, g2_ref, wq_ref, xgq_ref, kmem_ref,
                  vmem_ref, wo_ref, g3_ref, up_ref, down_ref, out_ref,
                  slab_a, slab_b, mix_a, mix_b, hg_ref, state_ref, kpad_ref, vpad_ref,
                  *, tiles_per_seq):
    s = pl.program_id(0)
    seq_start = lax.rem(s + tiles_per_seq - 1, tiles_per_seq) == 0

    @pl.when(s == 0)
    def _():
        slab_b[...] = jnp.zeros_like(slab_b)
        mix_a[...] = jnp.zeros_like(mix_a)

    @pl.when(seq_start | (s == 0))
    def _():
        state_ref[...] = jnp.zeros_like(state_ref)
        kpad_ref[...] = jnp.zeros_like(kpad_ref)
        vpad_ref[...] = jnp.zeros_like(vpad_ref)

    hlb = hlb_ref[...]
    e = jnp.exp(hlb - jnp.max(hlb, axis=0, keepdims=True))
    lb = e[0:1, :] / jnp.sum(e, axis=0, keepdims=True)

    def step(project_slab, mix_slab, mix_out, tail_in):
        streams = {
            "P": iter(_project_pieces(xp_ref, g1_ref, win_ref, project_slab)),
            "M": _mix_pieces(mix_slab, seq_start, lb, sinks_ref, ctab_ref, stab_ref, hgn_ref,
                             gq_ref, gk_ref, segm_ref, masks_ref, mix_out, hg_ref, state_ref,
                             kpad_ref, vpad_ref),
            "T": _tail_stages(xt_ref, tail_in, wout_ref, g2_ref, wq_ref, xgq_ref, kmem_ref.at[0],
                              vmem_ref.at[0], wo_ref, g3_ref, up_ref, down_ref, out_ref),
        }
        for name in STEP_ORDER:
            piece = next(streams[name], None)
            if callable(piece):
                piece()
        for stream in streams.values():
            assert next(stream, "done") == "done"

    @pl.when(s % 2 == 0)
    def _():
        step(slab_a, slab_b, mix_b, mix_a)

    @pl.when(s % 2 == 1)
    def _():
        step(slab_b, slab_a, mix_a, mix_b)


def _layer(x, mem_k, mem_v, ctab, stab, sinks, g1, w_in, hlb, hgn, gq, gk, w_out, g2, wq, xgq, wo,
           g3, up, down):
    B, S, D = x.shape
    M = mem_k.shape[1]
    tile = MIX_TILE
    tiles_per_seq = S // tile
    n_tiles = B * tiles_per_seq
    seg_mean = jnp.asarray(np.kron(np.eye(SW_HEADS), np.full((SW_HEAD_DIM, SW_HEAD_DIM),
                                                              1.0 / SW_HEAD_DIM)), BF16)
    masks = jnp.asarray(_hgrn2_level_masks())

    def tile_index(lag):
        return lambda s: jnp.clip(s - lag, 0, n_tiles - 1)

    def tok(width, lag):
        index = tile_index(lag)
        return pl.BlockSpec((tile, width), lambda s: (index(s), 0))

    def const(shape):
        return pl.BlockSpec(shape, lambda s: (0,) * len(shape), pipeline_mode=pl.Buffered(1))

    tail_index = tile_index(2)
    per_seq = pl.BlockSpec((1, M, XA_WIDTH), lambda s: (tail_index(s) // tiles_per_seq, 0, 0))
    slab = pltpu.VMEM((N_SLABS, tile, LANES), F32)
    mix_buf = pltpu.VMEM((tile, MIX_WIDTH), BF16)
    x2 = x.reshape(B * S, D)
    out = pl.pallas_call(
        functools.partial(_layer_kernel, tiles_per_seq=tiles_per_seq),
        grid=(n_tiles + 2,),
        in_specs=[pl.BlockSpec(memory_space=pltpu.SMEM),
                  tok(D, 0), tok(D, 2), tok(LANES, 1), tok(LANES, 1),
                  const((1, D)), const(w_in.shape), const(hlb.shape), const((1, HG_DV)),
                  const((1, SW_WIDTH)), const((1, LANES)), const((SW_WIDTH, SW_WIDTH)),
                  const((N_LEVELS, HG_CHUNK, 2 * HG_CHUNK)),
                  const(w_out.shape), const((1, D)), const(wq.shape), const((1, XA_HEAD_DIM)),
                  per_seq, per_seq, const(wo.shape), const((1, D)), const(up.shape),
                  const(down.shape)],
        out_specs=tok(D, 2),
        out_shape=jax.ShapeDtypeStruct((B * S, D), F32),
        scratch_shapes=[slab, slab, mix_buf, mix_buf,
                        pltpu.VMEM((HG_HEADS, tile, LANES), F32),
                        pltpu.VMEM((HG_HEADS, HG_DV, HG_DK), F32),
                        pltpu.VMEM((2 * SW_KV_HEADS, SW_BLOCK, LANES), BF16),
                        pltpu.VMEM((2 * SW_KV_HEADS, SW_BLOCK, LANES), BF16)],
        compiler_params=pltpu.CompilerParams(
            dimension_semantics=("arbitrary",),
            vmem_limit_bytes=VMEM_LIMIT_BYTES),
        name="layer",
    )(sinks, x2, x2, ctab.reshape(B * S, LANES), stab.reshape(B * S, LANES), g1, w_in, hlb, hgn,
      gq, gk, seg_mean, masks, w_out, g2, wq, xgq, mem_k, mem_v, wo, g3, up, down)
    return out.reshape(B, S, D)


def _tail_stages(x_ref, mix_ref, wout_ref, g2_ref, wq_ref, gq_ref, k_ref, v_ref, wo_ref,
                 g3_ref, up_ref, down_ref, out_ref):
    h1 = x_ref[...] + _dot(mix_ref[...], wout_ref[...])
    yield

    q = _dot(_rms_norm(h1, g2_ref[...]).astype(BF16), wq_ref[...])
    yield

    lanes = [slice(h * XA_HEAD_DIM, (h + 1) * XA_HEAD_DIM) for h in range(XA_HEADS)]
    qn = [_rms_norm(q[:, sl], gq_ref[...]) * (XA_HEAD_DIM ** -0.5) for sl in lanes]
    s = [_dot_nt(a.astype(BF16), k_ref[:, sl]) for a, sl in zip(qn, lanes)]
    p = [jnp.exp(a - jnp.max(a, axis=-1, keepdims=True)) for a in s]
    inv = [1.0 / jnp.sum(a, axis=-1, keepdims=True) for a in p]
    heads = [_dot(a.astype(BF16), v_ref[:, sl]) * b for a, b, sl in zip(p, inv, lanes)]
    yield

    h2 = h1 + _dot(jnp.concatenate(heads, axis=1).astype(BF16), wo_ref[...])
    out_ref[...] = h2
    hn = _rms_norm(h2, g3_ref[...]).astype(BF16)
    for c0 in range(0, up_ref.shape[1], FF_CHUNK):
        yield
        a = jnp.maximum(_dot(hn, up_ref[:, c0:c0 + FF_CHUNK]), 0.0)
        out_ref[...] += _dot((a * a).astype(BF16), down_ref[c0:c0 + FF_CHUNK, :])


def kernel(x, mem, positions, norm1_g, w_in, hg_lower_bounds, hg_norm_g, sw_q_norm_g, sw_k_norm_g,
           sw_sinks, w_out, norm2_g, mem_norm_g, xa_wq, xa_wkv, xa_q_norm_g, xa_k_norm_g, xa_wo,
           norm3_g, mlp_up, mlp_down):
    depth = norm1_g.shape[0]
    assert depth == 1 and x.shape[1] % MIX_TILE == 0
    ctab, stab = _rope_tables(positions)
    h = x
    for l in range(depth):
        kmem, vmem = _mem_kv(mem, mem_norm_g[l][None], xa_wkv[l].astype(BF16), xa_k_norm_g[l][None])
        h = _layer(h, kmem, vmem, ctab, stab, sw_sinks[l], norm1_g[l][None], w_in[l].astype(BF16),
                   hg_lower_bounds, hg_norm_g[l][None], jnp.tile(sw_q_norm_g[l], SW_HEADS)[None],
                   jnp.tile(sw_k_norm_g[l], SW_KV_HEADS)[None], w_out[l].astype(BF16),
                   norm2_g[l][None], xa_wq[l].astype(BF16), xa_q_norm_g[l][None],
                   xa_wo[l].astype(BF16), norm3_g[l][None], mlp_up[l].astype(BF16),
                   mlp_down[l].astype(BF16))
    return h
```

```python
import functools

import numpy as np
import jax
import jax.numpy as jnp
from jax import lax
from jax.experimental import pallas as pl
from jax.experimental.pallas import tpu as pltpu

F32 = jnp.float32
BF16 = jnp.bfloat16

LANES = 128
VMEM_LIMIT_BYTES = 56 * 1024 * 1024

EPS = 1e-6

HG_HEADS = 4
HG_DK = 128
HG_DV = 128
HG_WIDTH = HG_HEADS * HG_DV
HG_KEY_WIDTH = HG_HEADS * HG_DK

SW_HEADS = 8
SW_KV_HEADS = 2
SW_HEAD_DIM = 64
SW_WIDTH = SW_HEADS * SW_HEAD_DIM
SW_KV_WIDTH = SW_KV_HEADS * SW_HEAD_DIM
WINDOW = 128
SW_BLOCK = 128
assert WINDOW == SW_BLOCK
ROPE_THETA = 500000.0
ROT_DIM = SW_HEAD_DIM // 4
ROT_HALF = ROT_DIM // 2

MIX_WIDTH = HG_WIDTH + SW_WIDTH
IN_WIDTH = 2 * HG_KEY_WIDTH + 2 * HG_WIDTH + SW_WIDTH + 2 * SW_KV_WIDTH

XA_HEADS = 4
XA_HEAD_DIM = 128
XA_WIDTH = XA_HEADS * XA_HEAD_DIM

N_SLABS = IN_WIDTH // LANES
SLAB_Q, SLAB_F, SLAB_I, SLAB_G = 0, 4, 8, 12
SLAB_SQ, SLAB_SK, SLAB_SV = 16, 20, 21

HG_CHUNK = 64
N_PLANES = 8
PLANE_ROWS = HG_CHUNK // N_PLANES
N_LEVELS = 3

MIX_TILE = 256
FF_CHUNK = 1024


def _mul(a, b):
    if a is None:
        return b
    if b is None:
        return a
    return a * b


def _rms_norm(x, g):
    return x * lax.rsqrt(jnp.mean(x * x, axis=-1, keepdims=True) + EPS) * g


def _dot(a, b):
    return jnp.dot(a, b, preferred_element_type=F32)


def _dot_nt(a, b):
    return lax.dot_general(a, b, (((1,), (1,)), ((), ())), preferred_element_type=F32)


def _dot_tn(a, b):
    return lax.dot_general(a, b, (((0,), (0,)), ((), ())), preferred_element_type=F32)


BF16_SPLIT_PARTS = 3


def _rope_table_kernel(pos_ref, invf_ref, ecos_ref, esin_ref, base_ref, ctab_ref, stab_ref):
    ang = pos_ref[0].astype(F32) * invf_ref[...]

    def spread(t, e_ref):
        parts, rest = [], t
        for _ in range(BF16_SPLIT_PARTS):
            piece = rest.astype(BF16).astype(F32)
            parts.append(piece)
            rest = rest - piece
        parts.append(jnp.zeros_like(t))
        return _dot_tn(jnp.concatenate(parts, axis=0).astype(BF16), e_ref[...])

    ctab_ref[0] = spread(jnp.cos(ang), ecos_ref) + base_ref[...]
    stab_ref[0] = spread(jnp.sin(ang), esin_ref)


def _rope_tables(positions):
    B, S = positions.shape
    inv_freq = ROPE_THETA ** (-(jnp.arange(ROT_HALF, dtype=F32) * 2.0 / ROT_DIM))
    dim = np.arange(LANES) % SW_HEAD_DIM
    freq = np.arange(ROT_HALF)[:, None]
    first, second = dim[None, :] == freq, dim[None, :] == freq + ROT_HALF
    pad = np.zeros((ROT_HALF, LANES))
    stack = lambda e: jnp.asarray(np.concatenate([e] * BF16_SPLIT_PARTS + [pad]), BF16)
    ecos = stack(first * 1.0 + second * 1.0)
    esin = stack(second * 1.0 - first * 1.0)
    base = jnp.asarray((dim >= ROT_DIM)[None, :], F32)
    rows = (BF16_SPLIT_PARTS + 1) * ROT_HALF
    const = lambda shape: pl.BlockSpec(shape, lambda b: (0, 0))
    return pl.pallas_call(
        _rope_table_kernel,
        grid=(B,),
        in_specs=[pl.BlockSpec((1, 1, S), lambda b: (b, 0, 0)), const((ROT_HALF, 1)),
                  const((rows, LANES)), const((rows, LANES)), const((1, LANES))],
        out_specs=[pl.BlockSpec((1, S, LANES), lambda b: (b, 0, 0))] * 2,
        out_shape=[jax.ShapeDtypeStruct((B, S, LANES), F32)] * 2,
        name="rope_tables",
    )(positions.reshape(B, 1, S), inv_freq.reshape(ROT_HALF, 1), ecos, esin, base)


def _rope(t, ctab, stab):
    width = t.shape[1]
    reps = width // LANES
    c = jnp.concatenate([ctab] * reps, axis=1) if reps > 1 else ctab
    s = jnp.concatenate([stab] * reps, axis=1) if reps > 1 else stab
    lane = lax.broadcasted_iota(jnp.int32, t.shape, 1)
    first_half = (lane % SW_HEAD_DIM) < ROT_HALF
    partner = jnp.where(first_half,
                        pltpu.roll(t, width - ROT_HALF, 1),
                        pltpu.roll(t, ROT_HALF, 1))
    return t * c + partner * s


def _mem_kv_kernel(mem_ref, g_ref, wkv_ref, gk_ref, k_ref, v_ref):
    mn = _rms_norm(mem_ref[0], g_ref[...]).astype(BF16)
    kv = _dot(mn, wkv_ref[...])
    for h in range(XA_HEADS):
        sl = slice(h * XA_HEAD_DIM, (h + 1) * XA_HEAD_DIM)
        k_ref[0, :, sl] = _rms_norm(kv[:, sl], gk_ref[...]).astype(BF16)
    v_ref[0] = kv[:, XA_WIDTH:].astype(BF16)


def _mem_kv(mem, g, wkv, gk):
    B, M, D = mem.shape
    return pl.pallas_call(
        _mem_kv_kernel,
        grid=(B,),
        in_specs=[pl.BlockSpec((1, M, D), lambda b: (b, 0, 0)),
                  pl.BlockSpec((1, D), lambda b: (0, 0)),
                  pl.BlockSpec((D, 2 * XA_WIDTH), lambda b: (0, 0)),
                  pl.BlockSpec((1, XA_HEAD_DIM), lambda b: (0, 0))],
        out_specs=[pl.BlockSpec((1, M, XA_WIDTH), lambda b: (b, 0, 0))] * 2,
        out_shape=[jax.ShapeDtypeStruct((B, M, XA_WIDTH), BF16)] * 2,
        name="mem_kv",
    )(mem, g, wkv, gk)


def _hgrn2_level_masks():
    p = np.arange(HG_CHUNK)
    tok = N_PLANES * (p % PLANE_ROWS) + p // PLANE_ROWS
    ti, tj = tok[:, None], tok[None, :]
    masks = []
    h = N_PLANES
    while h < HG_CHUNK:
        masks.append((ti // (2 * h) == tj // (2 * h)) & ((ti // h) % 2 == 1) & ((tj // h) % 2 == 0))
        h *= 2
    masks = np.stack(masks).astype(np.float32)
    assert masks.shape[0] == N_LEVELS
    in_group = (ti // N_PLANES == tj // N_PLANES) & (tj <= ti)
    assert (masks.sum(0) + in_group == (tj <= ti)).all()
    return np.tile(masks, (1, 1, 2))


def _plane_block_products(f):
    pre = {1: list(f)}
    suf = {1: [None] * N_PLANES}
    h = 1
    while h < N_PLANES:
        p, s = pre[h], suf[h]
        new_p, new_s = [], []
        for r in range(N_PLANES):
            blk = r // h
            if blk % 2 == 1:
                new_p.append(_mul(p[r], p[blk * h - 1]))
                new_s.append(s[r])
            else:
                new_p.append(p[r])
                new_s.append(_mul(s[r], p[(blk + 2) * h - 1]))
        pre[2 * h], suf[2 * h] = new_p, new_s
        h *= 2
    return pre, suf


def _group_block_products(total):
    rows = total.shape[0]
    m_idx = lax.broadcasted_iota(jnp.int32, total.shape, 0) % PLANE_ROWS
    wpre, wsuf = [None], [None]
    for d in range(1, PLANE_ROWS):
        wpre.append(_mul(wpre[-1], pltpu.roll(total, d, 0)))
        wsuf.append(_mul(wsuf[-1], pltpu.roll(total, rows - d, 0)))
    cpre, csuf = {1: None}, {1: None}
    u = 2
    while u <= PLANE_ROWS:
        off = m_idx % u
        a = jnp.ones_like(total)
        b = jnp.ones_like(total)
        for d in range(1, u):
            a = jnp.where(off == d, wpre[d], a)
            b = jnp.where(off == u - 1 - d, wsuf[d], b)
        cpre[u], csuf[u] = a, b
        u *= 2
    return cpre, csuf


def _block_diag(a, b):
    za, zb = jnp.zeros_like(a), jnp.zeros_like(b)
    return jnp.concatenate([jnp.concatenate([a, zb], axis=1),
                            jnp.concatenate([za, b], axis=1)], axis=0)


def _hgrn2_tile(slab_ref, lb, norm_g, masks_ref, state_ref, out_ref):
    tile = slab_ref.shape[1]
    n_chunks = tile // HG_CHUNK
    plane_len = tile // N_PLANES
    pair_w = 2 * HG_DK
    n_pairs = HG_HEADS // 2

    def planes(slab0):
        return [jnp.concatenate(
            [slab_ref[slab0 + h, pl.ds(r, plane_len, stride=N_PLANES), :] for h in range(HG_HEADS)],
            axis=1) for r in range(N_PLANES)]

    def chunk_rows(ps, c, lanes=slice(None)):
        return jnp.concatenate([p[c * PLANE_ROWS:(c + 1) * PLANE_ROWS, lanes] for p in ps], axis=0)

    def pair_lanes(p):
        return slice(p * pair_w, (p + 1) * pair_w)

    def heads_block_diag(x):
        return _block_diag(x[:, :HG_DK], x[:, HG_DK:])

    lb_gap = 1.0 - lb
    f = [lb + lb_gap * jax.nn.sigmoid(x) for x in planes(SLAB_F)]
    q = planes(SLAB_Q)
    k = [1.0 - x for x in f]
    v = planes(SLAB_I)

    pre, suf = _plane_block_products(f)
    pre8, suf8 = pre[N_PLANES], suf[N_PLANES]
    assert 2 ** (N_LEVELS - 1) * N_PLANES * 2 == HG_CHUNK
    cpre, csuf = _group_block_products(pre8[N_PLANES - 1])

    q8 = [_mul(a, b) for a, b in zip(q, pre8)]
    k8 = [_mul(a, b) for a, b in zip(k, suf8)]
    def per_head(x, fn):
        return jnp.concatenate(
            [fn(x[:, h * HG_DK:(h + 1) * HG_DK], h) for h in range(HG_HEADS)], axis=1)

    group_out = [None] * N_PLANES
    for rj in range(N_PLANES):
        decay = None
        for ri in range(rj, N_PLANES):
            if ri > rj:
                decay = _mul(decay, f[ri])
            weighted = _mul(q[ri] * k[rj], decay)
            part = per_head(weighted, lambda a, h: jnp.sum(a, axis=-1, keepdims=True)
                            * v[rj][:, h * HG_DV:(h + 1) * HG_DV])
            group_out[ri] = part if group_out[ri] is None else group_out[ri] + part
        if rj % 4 == 3:
            yield

    scores = [[None] * n_pairs for _ in range(n_chunks)]
    u = 1
    for lvl in range(N_LEVELS):
        ql = [_mul(x, cpre[u]) for x in q8]
        kl = [_mul(x, csuf[u]) for x in k8]
        for c in range(n_chunks):
            for p in range(n_pairs):
                lhs = chunk_rows(ql, c, pair_lanes(p)).astype(BF16)
                rhs = heads_block_diag(chunk_rows(kl, c, pair_lanes(p)).astype(BF16))
                s = _dot_nt(lhs, rhs) * masks_ref[lvl]
                scores[c][p] = s if scores[c][p] is None else scores[c][p] + s
        u *= 2
        if lvl != 1:
            yield

    q_full = [x * cpre[PLANE_ROWS] for x in q8]
    k_full = [x * csuf[PLANE_ROWS] for x in k8]
    decay = pre8[N_PLANES - 1] * cpre[PLANE_ROWS]

    states = [state_ref[h] for h in range(HG_HEADS)]
    outs = []
    for c in range(n_chunks):
        last = (c + 1) * PLANE_ROWS - 1
        o_pairs = []
        for p in range(n_pairs):
            lanes = pair_lanes(p)
            vc = chunk_rows(v, c, lanes).astype(BF16)
            st = _block_diag(states[2 * p], states[2 * p + 1]).astype(BF16)
            o = (_dot_nt(chunk_rows(q_full, c, lanes).astype(BF16), st)
                 + _dot(scores[c][p].astype(BF16), heads_block_diag(vc)))
            upd = _dot_tn(vc, chunk_rows(k_full, c, lanes).astype(BF16))
            for i in range(2):
                h = 2 * p + i
                blk = slice(i * HG_DK, (i + 1) * HG_DK)
                states[h] = (states[h] * decay[last:last + 1, h * HG_DK:(h + 1) * HG_DK]
                             + upd[blk, blk])
            o_pairs.append(o)
        outs.append(jnp.concatenate(o_pairs, axis=1))
    for h in range(HG_HEADS):
        state_ref[h] = states[h]
    yield

    gate = planes(SLAB_G)
    for r in range(N_PLANES):
        o_r = (jnp.concatenate([o[r * PLANE_ROWS:(r + 1) * PLANE_ROWS] for o in outs], axis=0)
               + group_out[r])
        g_r = gate[r]
        for h in range(HG_HEADS):
            lanes = slice(h * HG_DV, (h + 1) * HG_DV)
            y = _rms_norm(o_r[:, lanes], norm_g) * (g_r[:, lanes] * jax.nn.sigmoid(g_r[:, lanes]))
            out_ref[h, pl.ds(r, plane_len, stride=N_PLANES), :] = y


def _swa_tile(slab_ref, seq_start, ctab, stab, gq, gk, sinks_ref, kpad_ref, vpad_ref,
              out_ref):
    tile = slab_ref.shape[1]
    n_blocks = tile // SW_BLOCK
    n_pairs = SW_HEADS // 2
    q = jnp.concatenate([slab_ref[SLAB_SQ + s] for s in range(SW_WIDTH // LANES)], axis=1)
    k = slab_ref[SLAB_SK]
    v = slab_ref[SLAB_SV]

    lane = lax.broadcasted_iota(jnp.int32, (tile, LANES), 1)
    low = lane < SW_HEAD_DIM

    def head_norm(t, g):
        sq = t * t
        sums = []
        for p in range(t.shape[1] // LANES):
            pair = sq[:, p * LANES:(p + 1) * LANES]
            first = jnp.sum(jnp.where(low, pair, 0.0), axis=-1, keepdims=True)
            second = jnp.sum(jnp.where(low, 0.0, pair), axis=-1, keepdims=True)
            sums.append(jnp.where(low, first, second))
        ms = jnp.concatenate(sums, axis=1) * (1.0 / SW_HEAD_DIM)
        return t * lax.rsqrt(ms + EPS) * g

    qn = _rope(head_norm(q, gq), ctab, stab) * (SW_HEAD_DIM ** -0.5)
    kn = _rope(head_norm(k, gk), ctab, stab)

    def padded(t):
        rolled = pltpu.roll(t, SW_HEAD_DIM, 1)
        zero = jnp.zeros_like(t)
        out = []
        for kv_head in range(SW_KV_HEADS):
            src_even, src_odd = (t, rolled) if kv_head == 0 else (rolled, t)
            out.append(jnp.where(low, src_even, zero).astype(BF16))
            out.append(jnp.where(low, zero, src_odd).astype(BF16))
        return out

    kpads, vpads = padded(kn), padded(v)
    qb = qn.astype(BF16)
    yield

    def block_operand(pads, carry_ref, blk, kv):
        parts = []
        for i in (kv, kv + 1):
            prev = carry_ref[i] if blk == 0 else pads[i][(blk - 1) * SW_BLOCK:blk * SW_BLOCK]
            parts += [prev, pads[i][blk * SW_BLOCK:(blk + 1) * SW_BLOCK]]
        return jnp.concatenate(parts, axis=0)

    units = [(blk, pair) for blk in range(n_blocks) for pair in range(n_pairs)]
    heads = [(blk, pair, half) for blk, pair in units for half in range(2)]
    kv_of = lambda pair: 2 * (pair // (n_pairs // SW_KV_HEADS))

    scores = {}
    for blk, pair in units:
        rows = slice(blk * SW_BLOCK, (blk + 1) * SW_BLOCK)
        scores[blk, pair] = _dot_nt(qb[rows, pair * LANES:(pair + 1) * LANES],
                                    block_operand(kpads, kpad_ref, blk, kv_of(pair)))

    yield
    qi = lax.broadcasted_iota(jnp.int32, (SW_BLOCK, SW_BLOCK), 0)
    kj = lax.broadcasted_iota(jnp.int32, (SW_BLOCK, SW_BLOCK), 1)
    from_prev = kj > qi
    start_bias = jnp.where(seq_start, -jnp.inf, 0.0)

    merged, sink = {}, {}
    for blk, pair, half in heads:
        s = scores[blk, pair]
        s_prev = s[:, (2 * half) * SW_BLOCK:(2 * half + 1) * SW_BLOCK]
        s_cur = s[:, (2 * half + 1) * SW_BLOCK:(2 * half + 2) * SW_BLOCK]
        if blk == 0:
            s_prev = s_prev + start_bias
        merged[blk, pair, half] = jnp.where(from_prev, s_prev, s_cur)
        sink[blk, pair, half] = sinks_ref[2 * pair + half]
    top = {u: jnp.maximum(jnp.max(merged[u], axis=-1, keepdims=True), sink[u]) for u in heads}
    prob = {u: jnp.exp(merged[u] - top[u]) for u in heads}
    inv = {u: 1.0 / (jnp.sum(prob[u], axis=-1, keepdims=True) + jnp.exp(sink[u] - top[u]))
           for u in heads}

    yield
    low_o = lax.broadcasted_iota(jnp.int32, (SW_BLOCK, LANES), 1) < SW_HEAD_DIM
    for blk, pair in units:
        ps = []
        for half in range(2):
            p = prob[blk, pair, half]
            zero = jnp.zeros_like(p)
            ps += [jnp.where(from_prev, p, zero).astype(BF16),
                   jnp.where(from_prev, zero, p).astype(BF16)]
        o = _dot(jnp.concatenate(ps, axis=1), block_operand(vpads, vpad_ref, blk, kv_of(pair)))
        o = o * jnp.where(low_o, inv[blk, pair, 0], inv[blk, pair, 1])
        out_ref[blk * SW_BLOCK:(blk + 1) * SW_BLOCK, pair * LANES:(pair + 1) * LANES] = (
            o.astype(out_ref.dtype))

    last = slice((n_blocks - 1) * SW_BLOCK, n_blocks * SW_BLOCK)
    for i in range(2 * SW_KV_HEADS):
        kpad_ref[i] = kpads[i][last]
        vpad_ref[i] = vpads[i][last]


def _project_pieces(x_ref, g1_ref, win_ref, slab_ref):
    xn = _rms_norm(x_ref[...], g1_ref[...]).astype(BF16)
    col_step = 2 * LANES

    def piece(c0):
        c1 = min(c0 + col_step, IN_WIDTH)
        proj = _dot(xn, win_ref[:, c0:c1])
        for s in range((c1 - c0) // LANES):
            slab_ref[c0 // LANES + s] = proj[:, s * LANES:(s + 1) * LANES]

    return [functools.partial(piece, c0) for c0 in range(0, IN_WIDTH, col_step)]


def _mix_pieces(slab_ref, seq_start, lb, sinks_ref, ctab_ref, stab_ref, hgn_ref, gq_ref, gk_ref,
                masks_ref, mix_ref, hg_ref, state_ref, kpad_ref, vpad_ref):
    yield from _swa_tile(slab_ref, seq_start, ctab_ref[...], stab_ref[...], gq_ref[...],
                         gk_ref[...], sinks_ref, kpad_ref, vpad_ref,
                         mix_ref.at[:, pl.ds(HG_WIDTH, SW_WIDTH)])
    yield
    yield from _hgrn2_tile(slab_ref, lb, hgn_ref[...], masks_ref, state_ref, hg_ref)
    for h in range(HG_HEADS):
        mix_ref[:, h * HG_DV:(h + 1) * HG_DV] = hg_ref[h].astype(BF16)


STEP_ORDER = "TMPPP TMPPP TMPP TMPP TMMP TMM TM TM".replace(" ", "")


def _layer_kernel(sinks_ref, xp_ref, xt_ref, ctab_ref, stab_ref, g1_ref, win_ref, hlb_ref, hgn_ref,
                  gq_ref, gk_ref, masks_ref, wout_ref, g2_ref, wq_ref, xgq_ref, kmem_ref,
                  vmem_ref, wo_ref, g3_ref, up_ref, down_ref, out_ref,
                  slab_a, slab_b, mix_a, mix_b, hg_ref, state_ref, kpad_ref, vpad_ref,
                  *, tiles_per_seq):
    s = pl.program_id(0)
    seq_start = lax.rem(s + tiles_per_seq - 1, tiles_per_seq) == 0

    @pl.when(s == 0)
    def _():
        slab_b[...] = jnp.zeros_like(slab_b)
        mix_a[...] = jnp.zeros_like(mix_a)

    @pl.when(seq_start | (s == 0))
    def _():
        state_ref[...] = jnp.zeros_like(state_ref)
        kpad_ref[...] = jnp.zeros_like(kpad_ref)
        vpad_ref[...] = jnp.zeros_like(vpad_ref)

    hlb = hlb_ref[...]
    e = jnp.exp(hlb - jnp.max(hlb, axis=0, keepdims=True))
    lb = e[0:1, :] / jnp.sum(e, axis=0, keepdims=True)

    def step(project_slab, mix_slab, mix_out, tail_in):
        streams = {
            "P": iter(_project_pieces(xp_ref, g1_ref, win_ref, project_slab)),
            "M": _mix_pieces(mix_slab, seq_start, lb, sinks_ref, ctab_ref, stab_ref, hgn_ref,
                             gq_ref, gk_ref, masks_ref, mix_out, hg_ref, state_ref,
                             kpad_ref, vpad_ref),
            "T": _tail_stages(xt_ref, tail_in, wout_ref, g2_ref, wq_ref, xgq_ref, kmem_ref.at[0],
                              vmem_ref.at[0], wo_ref, g3_ref, up_ref, down_ref, out_ref),
        }
        for name in STEP_ORDER:
            piece = next(streams[name], None)
            if callable(piece):
                piece()
        for stream in streams.values():
            assert next(stream, "done") == "done"

    @pl.when(s % 2 == 0)
    def _():
        step(slab_a, slab_b, mix_b, mix_a)

    @pl.when(s % 2 == 1)
    def _():
        step(slab_b, slab_a, mix_a, mix_b)


def _layer(x, mem_k, mem_v, ctab, stab, sinks, g1, w_in, hlb, hgn, gq, gk, w_out, g2, wq, xgq, wo,
           g3, up, down):
    B, S, D = x.shape
    M = mem_k.shape[1]
    tile = MIX_TILE
    tiles_per_seq = S // tile
    n_tiles = B * tiles_per_seq
    masks = jnp.asarray(_hgrn2_level_masks())

    def tile_index(lag):
        return lambda s: jnp.clip(s - lag, 0, n_tiles - 1)

    def tok(width, lag):
        index = tile_index(lag)
        return pl.BlockSpec((tile, width), lambda s: (index(s), 0))

    def const(shape):
        return pl.BlockSpec(shape, lambda s: (0,) * len(shape), pipeline_mode=pl.Buffered(1))

    tail_index = tile_index(2)
    per_seq = pl.BlockSpec((1, M, XA_WIDTH), lambda s: (tail_index(s) // tiles_per_seq, 0, 0))
    slab = pltpu.VMEM((N_SLABS, tile, LANES), F32)
    mix_buf = pltpu.VMEM((tile, MIX_WIDTH), BF16)
    x2 = x.reshape(B * S, D)
    out = pl.pallas_call(
        functools.partial(_layer_kernel, tiles_per_seq=tiles_per_seq),
        grid=(n_tiles + 2,),
        in_specs=[pl.BlockSpec(memory_space=pltpu.SMEM),
                  tok(D, 0), tok(D, 2), tok(LANES, 1), tok(LANES, 1),
                  const((1, D)), const(w_in.shape), const(hlb.shape), const((1, HG_DV)),
                  const((1, SW_WIDTH)), const((1, LANES)),
                  const((N_LEVELS, HG_CHUNK, 2 * HG_CHUNK)),
                  const(w_out.shape), const((1, D)), const(wq.shape), const((1, XA_HEAD_DIM)),
                  per_seq, per_seq, const(wo.shape), const((1, D)), const(up.shape),
                  const(down.shape)],
        out_specs=tok(D, 2),
        out_shape=jax.ShapeDtypeStruct((B * S, D), F32),
        scratch_shapes=[slab, slab, mix_buf, mix_buf,
                        pltpu.VMEM((HG_HEADS, tile, LANES), F32),
                        pltpu.VMEM((HG_HEADS, HG_DV, HG_DK), F32),
                        pltpu.VMEM((2 * SW_KV_HEADS, SW_BLOCK, LANES), BF16),
                        pltpu.VMEM((2 * SW_KV_HEADS, SW_BLOCK, LANES), BF16)],
        compiler_params=pltpu.CompilerParams(
            dimension_semantics=("arbitrary",),
            vmem_limit_bytes=VMEM_LIMIT_BYTES),
        name="layer",
    )(sinks, x2, x2, ctab.reshape(B * S, LANES), stab.reshape(B * S, LANES), g1, w_in, hlb, hgn,
      gq, gk, masks, w_out, g2, wq, xgq, mem_k, mem_v, wo, g3, up, down)
    return out.reshape(B, S, D)


def _tail_stages(x_ref, mix_ref, wout_ref, g2_ref, wq_ref, gq_ref, k_ref, v_ref, wo_ref,
                 g3_ref, up_ref, down_ref, out_ref):
    h1 = x_ref[...] + _dot(mix_ref[...], wout_ref[...])
    yield

    q = _dot(_rms_norm(h1, g2_ref[...]).astype(BF16), wq_ref[...])
    yield

    lanes = [slice(h * XA_HEAD_DIM, (h + 1) * XA_HEAD_DIM) for h in range(XA_HEADS)]
    qn = [_rms_norm(q[:, sl], gq_ref[...]) * (XA_HEAD_DIM ** -0.5) for sl in lanes]
    s = [_dot_nt(a.astype(BF16), k_ref[:, sl]) for a, sl in zip(qn, lanes)]
    p = [jnp.exp(a - jnp.max(a, axis=-1, keepdims=True)) for a in s]
    inv = [1.0 / jnp.sum(a, axis=-1, keepdims=True) for a in p]
    heads = [_dot(a.astype(BF16), v_ref[:, sl]) * b for a, b, sl in zip(p, inv, lanes)]
    yield

    h2 = h1 + _dot(jnp.concatenate(heads, axis=1).astype(BF16), wo_ref[...])
    out_ref[...] = h2
    hn = _rms_norm(h2, g3_ref[...]).astype(BF16)
    for c0 in range(0, up_ref.shape[1], FF_CHUNK):
        yield
        a = jnp.maximum(_dot(hn, up_ref[:, c0:c0 + FF_CHUNK]), 0.0)
        out_ref[...] += _dot((a * a).astype(BF16), down_ref[c0:c0 + FF_CHUNK, :])


def kernel(x, mem, positions, norm1_g, w_in, hg_lower_bounds, hg_norm_g, sw_q_norm_g, sw_k_norm_g,
           sw_sinks, w_out, norm2_g, mem_norm_g, xa_wq, xa_wkv, xa_q_norm_g, xa_k_norm_g, xa_wo,
           norm3_g, mlp_up, mlp_down):
    depth = norm1_g.shape[0]
    assert depth == 1 and x.shape[1] % MIX_TILE == 0
    ctab, stab = _rope_tables(positions)
    h = x
    for l in range(depth):
        kmem, vmem = _mem_kv(mem, mem_norm_g[l][None], xa_wkv[l].astype(BF16), xa_k_norm_g[l][None])
        h = _layer(h, kmem, vmem, ctab, stab, sw_sinks[l], norm1_g[l][None], w_in[l].astype(BF16),
                   hg_lower_bounds, hg_norm_g[l][None], jnp.tile(sw_q_norm_g[l], SW_HEADS)[None],
                   jnp.tile(sw_k_norm_g[l], SW_KV_HEADS)[None], w_out[l].astype(BF16),
                   norm2_g[l][None], xa_wq[l].astype(BF16), xa_q_norm_g[l][None],
                   xa_wo[l].astype(BF16), norm3_g[l][None], mlp_up[l].astype(BF16),
                   mlp_down[l].astype(BF16))
    return h
```

```python
import functools

import numpy as np
import jax
import jax.numpy as jnp
from jax import lax
from jax.experimental import pallas as pl
from jax.experimental.pallas import tpu as pltpu

F32 = jnp.float32
BF16 = jnp.bfloat16

LANES = 128
VMEM_LIMIT_BYTES = 56 * 1024 * 1024

EPS = 1e-6

HG_HEADS = 4
HG_DK = 128
HG_DV = 128
HG_WIDTH = HG_HEADS * HG_DV
HG_KEY_WIDTH = HG_HEADS * HG_DK

SW_HEADS = 8
SW_KV_HEADS = 2
SW_HEAD_DIM = 64
SW_WIDTH = SW_HEADS * SW_HEAD_DIM
SW_KV_WIDTH = SW_KV_HEADS * SW_HEAD_DIM
WINDOW = 128
SW_BLOCK = 128
assert WINDOW == SW_BLOCK
ROPE_THETA = 500000.0
ROT_DIM = SW_HEAD_DIM // 4
ROT_HALF = ROT_DIM // 2

MIX_WIDTH = HG_WIDTH + SW_WIDTH
IN_WIDTH = 2 * HG_KEY_WIDTH + 2 * HG_WIDTH + SW_WIDTH + 2 * SW_KV_WIDTH

XA_HEADS = 4
XA_HEAD_DIM = 128
XA_WIDTH = XA_HEADS * XA_HEAD_DIM

N_SLABS = IN_WIDTH // LANES
SLAB_Q, SLAB_F, SLAB_I, SLAB_G = 0, 4, 8, 12
SLAB_SQ, SLAB_SK, SLAB_SV = 16, 20, 21

HG_CHUNK = 64
N_PLANES = 8
PLANE_ROWS = HG_CHUNK // N_PLANES
N_LEVELS = 3

MIX_TILE = 256
FF_CHUNK = 1024


def _mul(a, b):
    if a is None:
        return b
    if b is None:
        return a
    return a * b


def _rms_norm(x, g):
    return x * lax.rsqrt(jnp.mean(x * x, axis=-1, keepdims=True) + EPS) * g


def _dot(a, b):
    return jnp.dot(a, b, preferred_element_type=F32)


def _dot_nt(a, b):
    return lax.dot_general(a, b, (((1,), (1,)), ((), ())), preferred_element_type=F32)


def _dot_tn(a, b):
    return lax.dot_general(a, b, (((0,), (0,)), ((), ())), preferred_element_type=F32)


BF16_SPLIT_PARTS = 3
ROPE_POSITIONS_PER_STEP = 8192


def _rope_table_kernel(pos_ref, invf_ref, ecos_ref, esin_ref, base_ref, ctab_ref, stab_ref):
    ang = pos_ref[0].astype(F32) * invf_ref[...]

    def spread(t, e_ref):
        parts, rest = [], t
        for _ in range(BF16_SPLIT_PARTS):
            piece = rest.astype(BF16).astype(F32)
            parts.append(piece)
            rest = rest - piece
        parts.append(jnp.zeros_like(t))
        return _dot_tn(jnp.concatenate(parts, axis=0).astype(BF16), e_ref[...])

    ctab_ref[0] = spread(jnp.cos(ang), ecos_ref) + base_ref[...]
    stab_ref[0] = spread(jnp.sin(ang), esin_ref)


def _rope_tables(positions):
    B, S = positions.shape
    inv_freq = ROPE_THETA ** (-(jnp.arange(ROT_HALF, dtype=F32) * 2.0 / ROT_DIM))
    dim = np.arange(LANES) % SW_HEAD_DIM
    freq = np.arange(ROT_HALF)[:, None]
    first, second = dim[None, :] == freq, dim[None, :] == freq + ROT_HALF
    pad = np.zeros((ROT_HALF, LANES))
    stack = lambda e: jnp.asarray(np.concatenate([e] * BF16_SPLIT_PARTS + [pad]), BF16)
    ecos = stack(first * 1.0 + second * 1.0)
    esin = stack(second * 1.0 - first * 1.0)
    base = jnp.asarray((dim >= ROT_DIM)[None, :], F32)
    rows = (BF16_SPLIT_PARTS + 1) * ROT_HALF
    const = lambda shape: pl.BlockSpec(shape, lambda i: (0, 0))
    n_pos = min(ROPE_POSITIONS_PER_STEP, B * S)
    assert (B * S) % n_pos == 0
    ctab, stab = pl.pallas_call(
        _rope_table_kernel,
        grid=(B * S // n_pos,),
        in_specs=[pl.BlockSpec((1, 1, n_pos), lambda i: (i, 0, 0)), const((ROT_HALF, 1)),
                  const((rows, LANES)), const((rows, LANES)), const((1, LANES))],
        out_specs=[pl.BlockSpec((1, n_pos, LANES), lambda i: (i, 0, 0))] * 2,
        out_shape=[jax.ShapeDtypeStruct((B * S // n_pos, n_pos, LANES), F32)] * 2,
        compiler_params=pltpu.CompilerParams(vmem_limit_bytes=VMEM_LIMIT_BYTES),
        name="rope_tables",
    )(positions.reshape(B * S // n_pos, 1, n_pos), inv_freq.reshape(ROT_HALF, 1), ecos, esin, base)
    return ctab.reshape(B, S, LANES), stab.reshape(B, S, LANES)


def _rope(t, ctab, stab):
    width = t.shape[1]
    reps = width // LANES
    c = jnp.concatenate([ctab] * reps, axis=1) if reps > 1 else ctab
    s = jnp.concatenate([stab] * reps, axis=1) if reps > 1 else stab
    lane = lax.broadcasted_iota(jnp.int32, t.shape, 1)
    first_half = (lane % SW_HEAD_DIM) < ROT_HALF
    partner = jnp.where(first_half,
                        pltpu.roll(t, width - ROT_HALF, 1),
                        pltpu.roll(t, ROT_HALF, 1))
    return t * c + partner * s


MEM_ROWS_PER_STEP = 1024


def _mem_kv_kernel(mem_ref, g_ref, wkv_ref, gk_ref, k_ref, v_ref):
    mn = _rms_norm(mem_ref[...], g_ref[...]).astype(BF16)
    kv = _dot(mn, wkv_ref[...])
    for h in range(XA_HEADS):
        sl = slice(h * XA_HEAD_DIM, (h + 1) * XA_HEAD_DIM)
        k_ref[:, sl] = _rms_norm(kv[:, sl], gk_ref[...]).astype(BF16)
    v_ref[...] = kv[:, XA_WIDTH:].astype(BF16)


def _mem_kv(mem, g, wkv, gk):
    B, M, D = mem.shape
    rows = min(MEM_ROWS_PER_STEP, B * M)
    assert (B * M) % rows == 0
    k, v = pl.pallas_call(
        _mem_kv_kernel,
        grid=(B * M // rows,),
        in_specs=[pl.BlockSpec((rows, D), lambda i: (i, 0)),
                  pl.BlockSpec((1, D), lambda i: (0, 0)),
                  pl.BlockSpec((D, 2 * XA_WIDTH), lambda i: (0, 0)),
                  pl.BlockSpec((1, XA_HEAD_DIM), lambda i: (0, 0))],
        out_specs=[pl.BlockSpec((rows, XA_WIDTH), lambda i: (i, 0))] * 2,
        out_shape=[jax.ShapeDtypeStruct((B * M, XA_WIDTH), BF16)] * 2,
        compiler_params=pltpu.CompilerParams(vmem_limit_bytes=VMEM_LIMIT_BYTES),
        name="mem_kv",
    )(mem.reshape(B * M, D), g, wkv, gk)
    return k.reshape(B, M, XA_WIDTH), v.reshape(B, M, XA_WIDTH)


def _hgrn2_level_masks():
    p = np.arange(HG_CHUNK)
    tok = N_PLANES * (p % PLANE_ROWS) + p // PLANE_ROWS
    ti, tj = tok[:, None], tok[None, :]
    masks = []
    h = N_PLANES
    while h < HG_CHUNK:
        masks.append((ti // (2 * h) == tj // (2 * h)) & ((ti // h) % 2 == 1) & ((tj // h) % 2 == 0))
        h *= 2
    masks = np.stack(masks).astype(np.float32)
    assert masks.shape[0] == N_LEVELS
    in_group = (ti // N_PLANES == tj // N_PLANES) & (tj <= ti)
    assert (masks.sum(0) + in_group == (tj <= ti)).all()
    return np.tile(masks, (1, 1, 2))


def _plane_block_products(f):
    pre = {1: list(f)}
    suf = {1: [None] * N_PLANES}
    h = 1
    while h < N_PLANES:
        p, s = pre[h], suf[h]
        new_p, new_s = [], []
        for r in range(N_PLANES):
            blk = r // h
            if blk % 2 == 1:
                new_p.append(_mul(p[r], p[blk * h - 1]))
                new_s.append(s[r])
            else:
                new_p.append(p[r])
                new_s.append(_mul(s[r], p[(blk + 2) * h - 1]))
        pre[2 * h], suf[2 * h] = new_p, new_s
        h *= 2
    return pre, suf


def _group_block_products(total):
    rows = total.shape[0]
    m_idx = lax.broadcasted_iota(jnp.int32, total.shape, 0) % PLANE_ROWS
    wpre, wsuf = [None], [None]
    for d in range(1, PLANE_ROWS):
        wpre.append(_mul(wpre[-1], pltpu.roll(total, d, 0)))
        wsuf.append(_mul(wsuf[-1], pltpu.roll(total, rows - d, 0)))
    cpre, csuf = {1: None}, {1: None}
    u = 2
    while u <= PLANE_ROWS:
        off = m_idx % u
        a = jnp.ones_like(total)
        b = jnp.ones_like(total)
        for d in range(1, u):
            a = jnp.where(off == d, wpre[d], a)
            b = jnp.where(off == u - 1 - d, wsuf[d], b)
        cpre[u], csuf[u] = a, b
        u *= 2
    return cpre, csuf


def _block_diag(a, b):
    za, zb = jnp.zeros_like(a), jnp.zeros_like(b)
    return jnp.concatenate([jnp.concatenate([a, zb], axis=1),
                            jnp.concatenate([za, b], axis=1)], axis=0)


def _hgrn2_tile(slab_ref, lb, norm_g, masks_ref, state_ref, out_ref):
    tile = slab_ref.shape[1]
    n_chunks = tile // HG_CHUNK
    plane_len = tile // N_PLANES
    pair_w = 2 * HG_DK
    n_pairs = HG_HEADS // 2

    def planes(slab0):
        return [jnp.concatenate(
            [slab_ref[slab0 + h, pl.ds(r, plane_len, stride=N_PLANES), :] for h in range(HG_HEADS)],
            axis=1) for r in range(N_PLANES)]

    def chunk_rows(ps, c, lanes=slice(None)):
        return jnp.concatenate([p[c * PLANE_ROWS:(c + 1) * PLANE_ROWS, lanes] for p in ps], axis=0)

    def pair_lanes(p):
        return slice(p * pair_w, (p + 1) * pair_w)

    def heads_block_diag(x):
        return _block_diag(x[:, :HG_DK], x[:, HG_DK:])

    lb_gap = 1.0 - lb
    f = [lb + lb_gap * jax.nn.sigmoid(x) for x in planes(SLAB_F)]
    q = planes(SLAB_Q)
    k = [1.0 - x for x in f]
    v = planes(SLAB_I)

    pre, suf = _plane_block_products(f)
    pre8, suf8 = pre[N_PLANES], suf[N_PLANES]
    assert 2 ** (N_LEVELS - 1) * N_PLANES * 2 == HG_CHUNK
    cpre, csuf = _group_block_products(pre8[N_PLANES - 1])

    q8 = [_mul(a, b) for a, b in zip(q, pre8)]
    k8 = [_mul(a, b) for a, b in zip(k, suf8)]
    def per_head(x, fn):
        return jnp.concatenate(
            [fn(x[:, h * HG_DK:(h + 1) * HG_DK], h) for h in range(HG_HEADS)], axis=1)

    group_out = [None] * N_PLANES
    for rj in range(N_PLANES):
        decay = None
        for ri in range(rj, N_PLANES):
            if ri > rj:
                decay = _mul(decay, f[ri])
            weighted = _mul(q[ri] * k[rj], decay)
            part = per_head(weighted, lambda a, h: jnp.sum(a, axis=-1, keepdims=True)
                            * v[rj][:, h * HG_DV:(h + 1) * HG_DV])
            group_out[ri] = part if group_out[ri] is None else group_out[ri] + part
        if rj % 4 == 3:
            yield

    scores = [[None] * n_pairs for _ in range(n_chunks)]
    u = 1
    for lvl in range(N_LEVELS):
        ql = [_mul(x, cpre[u]) for x in q8]
        kl = [_mul(x, csuf[u]) for x in k8]
        for c in range(n_chunks):
            for p in range(n_pairs):
                lhs = chunk_rows(ql, c, pair_lanes(p)).astype(BF16)
                rhs = heads_block_diag(chunk_rows(kl, c, pair_lanes(p)).astype(BF16))
                s = _dot_nt(lhs, rhs) * masks_ref[lvl]
                scores[c][p] = s if scores[c][p] is None else scores[c][p] + s
        u *= 2
        if lvl != 1:
            yield

    q_full = [x * cpre[PLANE_ROWS] for x in q8]
    k_full = [x * csuf[PLANE_ROWS] for x in k8]
    decay = pre8[N_PLANES - 1] * cpre[PLANE_ROWS]

    states = [state_ref[h] for h in range(HG_HEADS)]
    outs = []
    for c in range(n_chunks):
        last = (c + 1) * PLANE_ROWS - 1
        o_pairs = []
        for p in range(n_pairs):
            lanes = pair_lanes(p)
            vc = chunk_rows(v, c, lanes).astype(BF16)
            st = _block_diag(states[2 * p], states[2 * p + 1]).astype(BF16)
            o = (_dot_nt(chunk_rows(q_full, c, lanes).astype(BF16), st)
                 + _dot(scores[c][p].astype(BF16), heads_block_diag(vc)))
            upd = _dot_tn(vc, chunk_rows(k_full, c, lanes).astype(BF16))
            for i in range(2):
                h = 2 * p + i
                blk = slice(i * HG_DK, (i + 1) * HG_DK)
                states[h] = (states[h] * decay[last:last + 1, h * HG_DK:(h + 1) * HG_DK]
                             + upd[blk, blk])
            o_pairs.append(o)
        outs.append(jnp.concatenate(o_pairs, axis=1))
    for h in range(HG_HEADS):
        state_ref[h] = states[h]
    yield

    gate = planes(SLAB_G)
    for r in range(N_PLANES):
        o_r = (jnp.concatenate([o[r * PLANE_ROWS:(r + 1) * PLANE_ROWS] for o in outs], axis=0)
               + group_out[r])
        g_r = gate[r]
        for h in range(HG_HEADS):
            lanes = slice(h * HG_DV, (h + 1) * HG_DV)
            y = _rms_norm(o_r[:, lanes], norm_g) * (g_r[:, lanes] * jax.nn.sigmoid(g_r[:, lanes]))
            out_ref[h, pl.ds(r, plane_len, stride=N_PLANES), :] = y


def _swa_tile(slab_ref, seq_start, ctab, stab, gq, gk, seg_mean, sinks_ref, kpad_ref, vpad_ref,
              out_ref):
    tile = slab_ref.shape[1]
    n_blocks = tile // SW_BLOCK
    n_pairs = SW_HEADS // 2
    q = jnp.concatenate([slab_ref[SLAB_SQ + s] for s in range(SW_WIDTH // LANES)], axis=1)
    k = slab_ref[SLAB_SK]
    v = slab_ref[SLAB_SV]

    def head_norm(t, g, mean_mat):
        ms = _dot((t * t).astype(BF16), mean_mat)
        return t * lax.rsqrt(ms + EPS) * g

    qn = _rope(head_norm(q, gq, seg_mean[...]), ctab, stab) * (SW_HEAD_DIM ** -0.5)
    kn = _rope(head_norm(k, gk, seg_mean[:LANES, :LANES]), ctab, stab)

    lane = lax.broadcasted_iota(jnp.int32, (tile, LANES), 1)
    low = lane < SW_HEAD_DIM

    def padded(t):
        rolled = pltpu.roll(t, SW_HEAD_DIM, 1)
        zero = jnp.zeros_like(t)
        out = []
        for kv_head in range(SW_KV_HEADS):
            src_even, src_odd = (t, rolled) if kv_head == 0 else (rolled, t)
            out.append(jnp.where(low, src_even, zero).astype(BF16))
            out.append(jnp.where(low, zero, src_odd).astype(BF16))
        return out

    kpads, vpads = padded(kn), padded(v)
    qb = qn.astype(BF16)
    yield

    def block_operand(pads, carry_ref, blk, kv):
        parts = []
        for i in (kv, kv + 1):
            prev = carry_ref[i] if blk == 0 else pads[i][(blk - 1) * SW_BLOCK:blk * SW_BLOCK]
            parts += [prev, pads[i][blk * SW_BLOCK:(blk + 1) * SW_BLOCK]]
        return jnp.concatenate(parts, axis=0)

    units = [(blk, pair) for blk in range(n_blocks) for pair in range(n_pairs)]
    heads = [(blk, pair, half) for blk, pair in units for half in range(2)]
    kv_of = lambda pair: 2 * (pair // (n_pairs // SW_KV_HEADS))

    scores = {}
    for blk, pair in units:
        rows = slice(blk * SW_BLOCK, (blk + 1) * SW_BLOCK)
        scores[blk, pair] = _dot_nt(qb[rows, pair * LANES:(pair + 1) * LANES],
                                    block_operand(kpads, kpad_ref, blk, kv_of(pair)))

    yield
    qi = lax.broadcasted_iota(jnp.int32, (SW_BLOCK, SW_BLOCK), 0)
    kj = lax.broadcasted_iota(jnp.int32, (SW_BLOCK, SW_BLOCK), 1)
    from_prev = kj > qi
    start_bias = jnp.where(seq_start, -jnp.inf, 0.0)

    merged, sink = {}, {}
    for blk, pair, half in heads:
        s = scores[blk, pair]
        s_prev = s[:, (2 * half) * SW_BLOCK:(2 * half + 1) * SW_BLOCK]
        s_cur = s[:, (2 * half + 1) * SW_BLOCK:(2 * half + 2) * SW_BLOCK]
        if blk == 0:
            s_prev = s_prev + start_bias
        merged[blk, pair, half] = jnp.where(from_prev, s_prev, s_cur)
        sink[blk, pair, half] = sinks_ref[2 * pair + half]
    top = {u: jnp.maximum(jnp.max(merged[u], axis=-1, keepdims=True), sink[u]) for u in heads}
    prob = {u: jnp.exp(merged[u] - top[u]) for u in heads}
    inv = {u: 1.0 / (jnp.sum(prob[u], axis=-1, keepdims=True) + jnp.exp(sink[u] - top[u]))
           for u in heads}

    yield
    low_o = lax.broadcasted_iota(jnp.int32, (SW_BLOCK, LANES), 1) < SW_HEAD_DIM
    for blk, pair in units:
        ps = []
        for half in range(2):
            p = prob[blk, pair, half]
            zero = jnp.zeros_like(p)
            ps += [jnp.where(from_prev, p, zero).astype(BF16),
                   jnp.where(from_prev, zero, p).astype(BF16)]
        o = _dot(jnp.concatenate(ps, axis=1), block_operand(vpads, vpad_ref, blk, kv_of(pair)))
        o = o * jnp.where(low_o, inv[blk, pair, 0], inv[blk, pair, 1])
        out_ref[blk * SW_BLOCK:(blk + 1) * SW_BLOCK, pair * LANES:(pair + 1) * LANES] = (
            o.astype(out_ref.dtype))

    last = slice((n_blocks - 1) * SW_BLOCK, n_blocks * SW_BLOCK)
    for i in range(2 * SW_KV_HEADS):
        kpad_ref[i] = kpads[i][last]
        vpad_ref[i] = vpads[i][last]


def _project_pieces(x_ref, g1_ref, win_ref, slab_ref):
    xn = _rms_norm(x_ref[...], g1_ref[...]).astype(BF16)
    col_step = 2 * LANES

    def piece(c0):
        c1 = min(c0 + col_step, IN_WIDTH)
        proj = _dot(xn, win_ref[:, c0:c1])
        for s in range((c1 - c0) // LANES):
            slab_ref[c0 // LANES + s] = proj[:, s * LANES:(s + 1) * LANES]

    return [functools.partial(piece, c0) for c0 in range(0, IN_WIDTH, col_step)]


def _mix_pieces(slab_ref, seq_start, lb, sinks_ref, ctab_ref, stab_ref, hgn_ref, gq_ref, gk_ref,
                segm_ref, masks_ref, mix_ref, hg_ref, state_ref, kpad_ref, vpad_ref):
    yield from _swa_tile(slab_ref, seq_start, ctab_ref[...], stab_ref[...], gq_ref[...],
                         gk_ref[...], segm_ref, sinks_ref, kpad_ref, vpad_ref,
                         mix_ref.at[:, pl.ds(HG_WIDTH, SW_WIDTH)])
    yield
    yield from _hgrn2_tile(slab_ref, lb, hgn_ref[...], masks_ref, state_ref, hg_ref)
    for h in range(HG_HEADS):
        mix_ref[:, h * HG_DV:(h + 1) * HG_DV] = hg_ref[h].astype(BF16)


STEP_ORDER = "TMPPP TMPPP TMPP TMPP TMMP TMM TM TM".replace(" ", "")


def _layer_kernel(sinks_ref, xp_ref, xt_ref, ctab_ref, stab_ref, g1_ref, win_ref, hlb_ref, hgn_ref,
                  gq_ref, gk_ref, segm_ref, masks_ref, wout_ref, g2_ref, wq_ref, xgq_ref, kmem_ref,
                  vmem_ref, wo_ref, g3_ref, up_ref, down_ref, out_ref,
                  slab_a, slab_b, mix_a, mix_b, hg_ref, state_ref, kpad_ref, vpad_ref,
                  *, tiles_per_seq):
    s = pl.program_id(0)
    seq_start = lax.rem(s + tiles_per_seq - 1, tiles_per_seq) == 0

    @pl.when(s == 0)
    def _():
        slab_b[...] = jnp.zeros_like(slab_b)
        mix_a[...] = jnp.zeros_like(mix_a)

    @pl.when(seq_start | (s == 0))
    def _():
        state_ref[...] = jnp.zeros_like(state_ref)
        kpad_ref[...] = jnp.zeros_like(kpad_ref)
        vpad_ref[...] = jnp.zeros_like(vpad_ref)

    hlb = hlb_ref[...]
    e = jnp.exp(hlb - jnp.max(hlb, axis=0, keepdims=True))
    lb = e[0:1, :] / jnp.sum(e, axis=0, keepdims=True)

    def step(project_slab, mix_slab, mix_out, tail_in):
        streams = {
            "P": iter(_project_pieces(xp_ref, g1_ref, win_ref, project_slab)),
            "M": _mix_pieces(mix_slab, seq_start, lb, sinks_ref, ctab_ref, stab_ref, hgn_ref,
                             gq_ref, gk_ref, segm_ref, masks_ref, mix_out, hg_ref, state_ref,
                             kpad_ref, vpad_ref),
            "T": _tail_stages(xt_ref, tail_in, wout_ref, g2_ref, wq_ref, xgq_ref, kmem_ref.at[0],
                              vmem_ref.at[0], wo_ref, g3_ref, up_ref, down_ref, out_ref),
        }
        for name in STEP_ORDER:
            piece = next(streams[name], None)
            if callable(piece):
                piece()
        for stream in streams.values():
            assert next(stream, "done") == "done"

    @pl.when(s % 2 == 0)
    def _():
        step(slab_a, slab_b, mix_b, mix_a)

    @pl.when(s % 2 == 1)
    def _():
        step(slab_b, slab_a, mix_a, mix_b)


def _layer(x, mem_k, mem_v, ctab, stab, sinks, g1, w_in, hlb, hgn, gq, gk, w_out, g2, wq, xgq, wo,
           g3, up, down):
    B, S, D = x.shape
    M = mem_k.shape[1]
    tile = MIX_TILE
    tiles_per_seq = S // tile
    n_tiles = B * tiles_per_seq
    seg_mean = jnp.asarray(np.kron(np.eye(SW_HEADS), np.full((SW_HEAD_DIM, SW_HEAD_DIM),
                                                              1.0 / SW_HEAD_DIM)), BF16)
    masks = jnp.asarray(_hgrn2_level_masks())

    def tile_index(lag):
        return lambda s: jnp.clip(s - lag, 0, n_tiles - 1)

    def tok(width, lag):
        index = tile_index(lag)
        return pl.BlockSpec((tile, width), lambda s: (index(s), 0))

    def const(shape):
        return pl.BlockSpec(shape, lambda s: (0,) * len(shape), pipeline_mode=pl.Buffered(1))

    tail_index = tile_index(2)
    per_seq = pl.BlockSpec((1, M, XA_WIDTH), lambda s: (tail_index(s) // tiles_per_seq, 0, 0))
    slab = pltpu.VMEM((N_SLABS, tile, LANES), F32)
    mix_buf = pltpu.VMEM((tile, MIX_WIDTH), BF16)
    x2 = x.reshape(B * S, D)
    out = pl.pallas_call(
        functools.partial(_layer_kernel, tiles_per_seq=tiles_per_seq),
        grid=(n_tiles + 2,),
        in_specs=[pl.BlockSpec(memory_space=pltpu.SMEM),
                  tok(D, 0), tok(D, 2), tok(LANES, 1), tok(LANES, 1),
                  const((1, D)), const(w_in.shape), const(hlb.shape), const((1, HG_DV)),
                  const((1, SW_WIDTH)), const((1, LANES)), const((SW_WIDTH, SW_WIDTH)),
                  const((N_LEVELS, HG_CHUNK, 2 * HG_CHUNK)),
                  const(w_out.shape), const((1, D)), const(wq.shape), const((1, XA_HEAD_DIM)),
                  per_seq, per_seq, const(wo.shape), const((1, D)), const(up.shape),
                  const(down.shape)],
        out_specs=tok(D, 2),
        out_shape=jax.ShapeDtypeStruct((B * S, D), F32),
        scratch_shapes=[slab, slab, mix_buf, mix_buf,
                        pltpu.VMEM((HG_HEADS, tile, LANES), F32),
                        pltpu.VMEM((HG_HEADS, HG_DV, HG_DK), F32),
                        pltpu.VMEM((2 * SW_KV_HEADS, SW_BLOCK, LANES), BF16),
                        pltpu.VMEM((2 * SW_KV_HEADS, SW_BLOCK, LANES), BF16)],
        compiler_params=pltpu.CompilerParams(
            dimension_semantics=("arbitrary",),
            vmem_limit_bytes=VMEM_LIMIT_BYTES),
        name="layer",
    )(sinks, x2, x2, ctab.reshape(B * S, LANES), stab.reshape(B * S, LANES), g1, w_in, hlb, hgn,
      gq, gk, seg_mean, masks, w_out, g2, wq, xgq, mem_k, mem_v, wo, g3, up, down)
    return out.reshape(B, S, D)


def _tail_stages(x_ref, mix_ref, wout_ref, g2_ref, wq_ref, gq_ref, k_ref, v_ref, wo_ref,
                 g3_ref, up_ref, down_ref, out_ref):
    h1 = x_ref[...] + _dot(mix_ref[...], wout_ref[...])
    yield

    q = _dot(_rms_norm(h1, g2_ref[...]).astype(BF16), wq_ref[...])
    yield

    lanes = [slice(h * XA_HEAD_DIM, (h + 1) * XA_HEAD_DIM) for h in range(XA_HEADS)]
    qn = [_rms_norm(q[:, sl], gq_ref[...]) * (XA_HEAD_DIM ** -0.5) for sl in lanes]
    s = [_dot_nt(a.astype(BF16), k_ref[:, sl]) for a, sl in zip(qn, lanes)]
    p = [jnp.exp(a - jnp.max(a, axis=-1, keepdims=True)) for a in s]
    inv = [1.0 / jnp.sum(a, axis=-1, keepdims=True) for a in p]
    heads = [_dot(a.astype(BF16), v_ref[:, sl]) * b for a, b, sl in zip(p, inv, lanes)]
    yield

    h2 = h1 + _dot(jnp.concatenate(heads, axis=1).astype(BF16), wo_ref[...])
    out_ref[...] = h2
    hn = _rms_norm(h2, g3_ref[...]).astype(BF16)
    for c0 in range(0, up_ref.shape[1], FF_CHUNK):
        yield
        a = jnp.maximum(_dot(hn, up_ref[:, c0:c0 + FF_CHUNK]), 0.0)
        out_ref[...] += _dot((a * a).astype(BF16), down_ref[c0:c0 + FF_CHUNK, :])


def kernel(x, mem, positions, norm1_g, w_in, hg_lower_bounds, hg_norm_g, sw_q_norm_g, sw_k_norm_g,
           sw_sinks, w_out, norm2_g, mem_norm_g, xa_wq, xa_wkv, xa_q_norm_g, xa_k_norm_g, xa_wo,
           norm3_g, mlp_up, mlp_down):
    depth = norm1_g.shape[0]
    assert depth == 1 and x.shape[1] % MIX_TILE == 0
    ctab, stab = _rope_tables(positions)
    h = x
    for l in range(depth):
        kmem, vmem = _mem_kv(mem, mem_norm_g[l][None], xa_wkv[l].astype(BF16), xa_k_norm_g[l][None])
        h = _layer(h, kmem, vmem, ctab, stab, sw_sinks[l], norm1_g[l][None], w_in[l].astype(BF16),
                   hg_lower_bounds, hg_norm_g[l][None], jnp.tile(sw_q_norm_g[l], SW_HEADS)[None],
                   jnp.tile(sw_k_norm_g[l], SW_KV_HEADS)[None], w_out[l].astype(BF16),
                   norm2_g[l][None], xa_wq[l].astype(BF16), xa_q_norm_g[l][None],
                   xa_wo[l].astype(BF16), norm3_g[l][None], mlp_up[l].astype(BF16),
                   mlp_down[l].astype(BF16))
    return h
```

```python
import functools

import numpy as np
import jax
import jax.numpy as jnp
from jax import lax
from jax.experimental import pallas as pl
from jax.experimental.pallas import tpu as pltpu

F32 = jnp.float32
BF16 = jnp.bfloat16

LANES = 128
VMEM_LIMIT_BYTES = 56 * 1024 * 1024

EPS = 1e-6

HG_HEADS = 4
HG_DK = 128
HG_DV = 128
HG_WIDTH = HG_HEADS * HG_DV
HG_KEY_WIDTH = HG_HEADS * HG_DK

SW_HEADS = 8
SW_KV_HEADS = 2
SW_HEAD_DIM = 64
SW_WIDTH = SW_HEADS * SW_HEAD_DIM
SW_KV_WIDTH = SW_KV_HEADS * SW_HEAD_DIM
WINDOW = 128
SW_BLOCK = 128
assert WINDOW == SW_BLOCK
ROPE_THETA = 500000.0
ROT_DIM = SW_HEAD_DIM // 4
ROT_HALF = ROT_DIM // 2

MIX_WIDTH = HG_WIDTH + SW_WIDTH
IN_WIDTH = 2 * HG_KEY_WIDTH + 2 * HG_WIDTH + SW_WIDTH + 2 * SW_KV_WIDTH

XA_HEADS = 4
XA_HEAD_DIM = 128
XA_WIDTH = XA_HEADS * XA_HEAD_DIM

N_SLABS = IN_WIDTH // LANES
SLAB_Q, SLAB_F, SLAB_I, SLAB_G = 0, 4, 8, 12
SLAB_SQ, SLAB_SK, SLAB_SV = 16, 20, 21

HG_CHUNK = 64
N_PLANES = 8
PLANE_ROWS = HG_CHUNK // N_PLANES
N_LEVELS = 3

MIX_TILE = 256
FF_CHUNK = 1024


def _mul(a, b):
    if a is None:
        return b
    if b is None:
        return a
    return a * b


def _rms_norm(x, g):
    return x * lax.rsqrt(jnp.mean(x * x, axis=-1, keepdims=True) + EPS) * g


def _dot(a, b):
    return jnp.dot(a, b, preferred_element_type=F32)


def _dot_nt(a, b):
    return lax.dot_general(a, b, (((1,), (1,)), ((), ())), preferred_element_type=F32)


def _dot_tn(a, b):
    return lax.dot_general(a, b, (((0,), (0,)), ((), ())), preferred_element_type=F32)


BF16_SPLIT_PARTS = 3
ROPE_POSITIONS_PER_STEP = 8192


def _rope_table_kernel(pos_ref, invf_ref, ecos_ref, esin_ref, base_ref, ctab_ref, stab_ref):
    ang = pos_ref[0].astype(F32) * invf_ref[...]

    def spread(t, e_ref):
        parts, rest = [], t
        for _ in range(BF16_SPLIT_PARTS):
            piece = rest.astype(BF16).astype(F32)
            parts.append(piece)
            rest = rest - piece
        parts.append(jnp.zeros_like(t))
        return _dot_tn(jnp.concatenate(parts, axis=0).astype(BF16), e_ref[...])

    ctab_ref[0] = spread(jnp.cos(ang), ecos_ref) + base_ref[...]
    stab_ref[0] = spread(jnp.sin(ang), esin_ref)


def _rope_tables(positions):
    B, S = positions.shape
    inv_freq = ROPE_THETA ** (-(jnp.arange(ROT_HALF, dtype=F32) * 2.0 / ROT_DIM))
    dim = np.arange(LANES) % SW_HEAD_DIM
    freq = np.arange(ROT_HALF)[:, None]
    first, second = dim[None, :] == freq, dim[None, :] == freq + ROT_HALF
    pad = np.zeros((ROT_HALF, LANES))
    stack = lambda e: jnp.asarray(np.concatenate([e] * BF16_SPLIT_PARTS + [pad]), BF16)
    ecos = stack(first * 1.0 + second * 1.0)
    esin = stack(second * 1.0 - first * 1.0)
    base = jnp.asarray((dim >= ROT_DIM)[None, :], F32)
    rows = (BF16_SPLIT_PARTS + 1) * ROT_HALF
    const = lambda shape: pl.BlockSpec(shape, lambda i: (0, 0))
    n_pos = min(ROPE_POSITIONS_PER_STEP, B * S)
    assert (B * S) % n_pos == 0
    ctab, stab = pl.pallas_call(
        _rope_table_kernel,
        grid=(B * S // n_pos,),
        in_specs=[pl.BlockSpec((1, 1, n_pos), lambda i: (i, 0, 0)), const((ROT_HALF, 1)),
                  const((rows, LANES)), const((rows, LANES)), const((1, LANES))],
        out_specs=[pl.BlockSpec((1, n_pos, LANES), lambda i: (i, 0, 0))] * 2,
        out_shape=[jax.ShapeDtypeStruct((B * S // n_pos, n_pos, LANES), F32)] * 2,
        compiler_params=pltpu.CompilerParams(vmem_limit_bytes=VMEM_LIMIT_BYTES),
        name="rope_tables",
    )(positions.reshape(B * S // n_pos, 1, n_pos), inv_freq.reshape(ROT_HALF, 1), ecos, esin, base)
    return ctab.reshape(B, S, LANES), stab.reshape(B, S, LANES)


def _rope(t, ctab, stab):
    width = t.shape[1]
    reps = width // LANES
    c = jnp.concatenate([ctab] * reps, axis=1) if reps > 1 else ctab
    s = jnp.concatenate([stab] * reps, axis=1) if reps > 1 else stab
    lane = lax.broadcasted_iota(jnp.int32, t.shape, 1)
    first_half = (lane % SW_HEAD_DIM) < ROT_HALF
    partner = jnp.where(first_half,
                        pltpu.roll(t, width - ROT_HALF, 1),
                        pltpu.roll(t, ROT_HALF, 1))
    return t * c + partner * s


MEM_ROWS_PER_STEP = 1024


def _mem_kv_kernel(mem_ref, g_ref, wkv_ref, gk_ref, k_ref, v_ref):
    mn = _rms_norm(mem_ref[...], g_ref[...]).astype(BF16)
    kv = _dot(mn, wkv_ref[...])
    for h in range(XA_HEADS):
        sl = slice(h * XA_HEAD_DIM, (h + 1) * XA_HEAD_DIM)
        k_ref[:, sl] = _rms_norm(kv[:, sl], gk_ref[...]).astype(BF16)
    v_ref[...] = kv[:, XA_WIDTH:].astype(BF16)


def _mem_kv(mem, g, wkv, gk):
    B, M, D = mem.shape
    rows = min(MEM_ROWS_PER_STEP, B * M)
    assert (B * M) % rows == 0
    k, v = pl.pallas_call(
        _mem_kv_kernel,
        grid=(B * M // rows,),
        in_specs=[pl.BlockSpec((rows, D), lambda i: (i, 0)),
                  pl.BlockSpec((1, D), lambda i: (0, 0)),
                  pl.BlockSpec((D, 2 * XA_WIDTH), lambda i: (0, 0)),
                  pl.BlockSpec((1, XA_HEAD_DIM), lambda i: (0, 0))],
        out_specs=[pl.BlockSpec((rows, XA_WIDTH), lambda i: (i, 0))] * 2,
        out_shape=[jax.ShapeDtypeStruct((B * M, XA_WIDTH), BF16)] * 2,
        compiler_params=pltpu.CompilerParams(vmem_limit_bytes=VMEM_LIMIT_BYTES),
        name="mem_kv",
    )(mem.reshape(B * M, D), g, wkv, gk)
    return k.reshape(B, M, XA_WIDTH), v.reshape(B, M, XA_WIDTH)


def _hgrn2_level_masks():
    p = np.arange(HG_CHUNK)
    tok = N_PLANES * (p % PLANE_ROWS) + p // PLANE_ROWS
    ti, tj = tok[:, None], tok[None, :]
    masks = []
    h = N_PLANES
    while h < HG_CHUNK:
        masks.append((ti // (2 * h) == tj // (2 * h)) & ((ti // h) % 2 == 1) & ((tj // h) % 2 == 0))
        h *= 2
    masks = np.stack(masks).astype(np.float32)
    assert masks.shape[0] == N_LEVELS
    in_group = (ti // N_PLANES == tj // N_PLANES) & (tj <= ti)
    assert (masks.sum(0) + in_group == (tj <= ti)).all()
    return np.tile(masks, (1, 1, 2))


def _plane_block_products(f):
    pre = {1: list(f)}
    suf = {1: [None] * N_PLANES}
    h = 1
    while h < N_PLANES:
        p, s = pre[h], suf[h]
        new_p, new_s = [], []
        for r in range(N_PLANES):
            blk = r // h
            if blk % 2 == 1:
                new_p.append(_mul(p[r], p[blk * h - 1]))
                new_s.append(s[r])
            else:
                new_p.append(p[r])
                new_s.append(_mul(s[r], p[(blk + 2) * h - 1]))
        pre[2 * h], suf[2 * h] = new_p, new_s
        h *= 2
    return pre, suf


def _group_block_products(total):
    rows = total.shape[0]
    m_idx = lax.broadcasted_iota(jnp.int32, total.shape, 0) % PLANE_ROWS
    wpre, wsuf = [None], [None]
    for d in range(1, PLANE_ROWS):
        wpre.append(_mul(wpre[-1], pltpu.roll(total, d, 0)))
        wsuf.append(_mul(wsuf[-1], pltpu.roll(total, rows - d, 0)))
    cpre, csuf = {1: None}, {1: None}
    u = 2
    while u <= PLANE_ROWS:
        off = m_idx % u
        a = jnp.ones_like(total)
        b = jnp.ones_like(total)
        for d in range(1, u):
            a = jnp.where(off == d, wpre[d], a)
            b = jnp.where(off == u - 1 - d, wsuf[d], b)
        cpre[u], csuf[u] = a, b
        u *= 2
    return cpre, csuf


def _block_diag(a, b):
    za, zb = jnp.zeros_like(a), jnp.zeros_like(b)
    return jnp.concatenate([jnp.concatenate([a, zb], axis=1),
                            jnp.concatenate([za, b], axis=1)], axis=0)


def _hgrn2_tile(slab_ref, lb, norm_g, masks_ref, state_ref, out_ref):
    tile = slab_ref.shape[1]
    n_chunks = tile // HG_CHUNK
    plane_len = tile // N_PLANES
    pair_w = 2 * HG_DK
    n_pairs = HG_HEADS // 2

    def planes(slab0):
        return [jnp.concatenate(
            [slab_ref[slab0 + h, pl.ds(r, plane_len, stride=N_PLANES), :] for h in range(HG_HEADS)],
            axis=1) for r in range(N_PLANES)]

    def chunk_rows(ps, c, lanes=slice(None)):
        return jnp.concatenate([p[c * PLANE_ROWS:(c + 1) * PLANE_ROWS, lanes] for p in ps], axis=0)

    def pair_lanes(p):
        return slice(p * pair_w, (p + 1) * pair_w)

    def heads_block_diag(x):
        return _block_diag(x[:, :HG_DK], x[:, HG_DK:])

    lb_gap = 1.0 - lb
    f = [lb + lb_gap * jax.nn.sigmoid(x) for x in planes(SLAB_F)]
    q = planes(SLAB_Q)
    k = [1.0 - x for x in f]
    v = planes(SLAB_I)

    pre, suf = _plane_block_products(f)
    pre8, suf8 = pre[N_PLANES], suf[N_PLANES]
    assert 2 ** (N_LEVELS - 1) * N_PLANES * 2 == HG_CHUNK
    cpre, csuf = _group_block_products(pre8[N_PLANES - 1])

    q8 = [_mul(a, b) for a, b in zip(q, pre8)]
    k8 = [_mul(a, b) for a, b in zip(k, suf8)]
    def per_head(x, fn):
        return jnp.concatenate(
            [fn(x[:, h * HG_DK:(h + 1) * HG_DK], h) for h in range(HG_HEADS)], axis=1)

    group_out = [None] * N_PLANES
    for rj in range(N_PLANES):
        decay = None
        for ri in range(rj, N_PLANES):
            if ri > rj:
                decay = _mul(decay, f[ri])
            weighted = _mul(q[ri] * k[rj], decay)
            part = per_head(weighted, lambda a, h: jnp.sum(a, axis=-1, keepdims=True)
                            * v[rj][:, h * HG_DV:(h + 1) * HG_DV])
            group_out[ri] = part if group_out[ri] is None else group_out[ri] + part
        if rj % 4 == 3:
            yield

    scores = [[None] * n_pairs for _ in range(n_chunks)]
    u = 1
    for lvl in range(N_LEVELS):
        ql = [_mul(x, cpre[u]) for x in q8]
        kl = [_mul(x, csuf[u]) for x in k8]
        for c in range(n_chunks):
            for p in range(n_pairs):
                lhs = chunk_rows(ql, c, pair_lanes(p)).astype(BF16)
                rhs = heads_block_diag(chunk_rows(kl, c, pair_lanes(p)).astype(BF16))
                s = _dot_nt(lhs, rhs) * masks_ref[lvl]
                scores[c][p] = s if scores[c][p] is None else scores[c][p] + s
        u *= 2
        if lvl != 1:
            yield

    q_full = [x * cpre[PLANE_ROWS] for x in q8]
    k_full = [x * csuf[PLANE_ROWS] for x in k8]
    decay = pre8[N_PLANES - 1] * cpre[PLANE_ROWS]

    states = [state_ref[h] for h in range(HG_HEADS)]
    outs = []
    for c in range(n_chunks):
        last = (c + 1) * PLANE_ROWS - 1
        o_pairs = []
        for p in range(n_pairs):
            lanes = pair_lanes(p)
            vc = chunk_rows(v, c, lanes).astype(BF16)
            st = _block_diag(states[2 * p], states[2 * p + 1]).astype(BF16)
            o = (_dot_nt(chunk_rows(q_full, c, lanes).astype(BF16), st)
                 + _dot(scores[c][p].astype(BF16), heads_block_diag(vc)))
            upd = _dot_tn(vc, chunk_rows(k_full, c, lanes).astype(BF16))
            for i in range(2):
                h = 2 * p + i
                blk = slice(i * HG_DK, (i + 1) * HG_DK)
                states[h] = (states[h] * decay[last:last + 1, h * HG_DK:(h + 1) * HG_DK]
                             + upd[blk, blk])
            o_pairs.append(o)
        outs.append(jnp.concatenate(o_pairs, axis=1))
    for h in range(HG_HEADS):
        state_ref[h] = states[h]
    yield

    gate = planes(SLAB_G)
    for r in range(N_PLANES):
        o_r = (jnp.concatenate([o[r * PLANE_ROWS:(r + 1) * PLANE_ROWS] for o in outs], axis=0)
               + group_out[r])
        g_r = gate[r]
        for h in range(HG_HEADS):
            lanes = slice(h * HG_DV, (h + 1) * HG_DV)
            y = _rms_norm(o_r[:, lanes], norm_g) * (g_r[:, lanes] * jax.nn.sigmoid(g_r[:, lanes]))
            out_ref[h, pl.ds(r, plane_len, stride=N_PLANES), :] = y


def _swa_tile(slab_ref, seq_start, ctab, stab, gq, gk, sinks_ref, kpad_ref, vpad_ref,
              out_ref):
    tile = slab_ref.shape[1]
    n_blocks = tile // SW_BLOCK
    n_pairs = SW_HEADS // 2
    q = jnp.concatenate([slab_ref[SLAB_SQ + s] for s in range(SW_WIDTH // LANES)], axis=1)
    k = slab_ref[SLAB_SK]
    v = slab_ref[SLAB_SV]

    lane = lax.broadcasted_iota(jnp.int32, (tile, LANES), 1)
    low = lane < SW_HEAD_DIM

    def head_norm(t, g):
        sq = t * t
        sums = []
        for p in range(t.shape[1] // LANES):
            pair = sq[:, p * LANES:(p + 1) * LANES]
            first = jnp.sum(jnp.where(low, pair, 0.0), axis=-1, keepdims=True)
            second = jnp.sum(jnp.where(low, 0.0, pair), axis=-1, keepdims=True)
            sums.append(jnp.where(low, first, second))
        ms = jnp.concatenate(sums, axis=1) * (1.0 / SW_HEAD_DIM)
        return t * lax.rsqrt(ms + EPS) * g

    qn = _rope(head_norm(q, gq), ctab, stab) * (SW_HEAD_DIM ** -0.5)
    kn = _rope(head_norm(k, gk), ctab, stab)

    def padded(t):
        rolled = pltpu.roll(t, SW_HEAD_DIM, 1)
        zero = jnp.zeros_like(t)
        out = []
        for kv_head in range(SW_KV_HEADS):
            src_even, src_odd = (t, rolled) if kv_head == 0 else (rolled, t)
            out.append(jnp.where(low, src_even, zero).astype(BF16))
            out.append(jnp.where(low, zero, src_odd).astype(BF16))
        return out

    kpads, vpads = padded(kn), padded(v)
    qb = qn.astype(BF16)
    yield

    def block_operand(pads, carry_ref, blk, kv):
        parts = []
        for i in (kv, kv + 1):
            prev = carry_ref[i] if blk == 0 else pads[i][(blk - 1) * SW_BLOCK:blk * SW_BLOCK]
            parts += [prev, pads[i][blk * SW_BLOCK:(blk + 1) * SW_BLOCK]]
        return jnp.concatenate(parts, axis=0)

    units = [(blk, pair) for blk in range(n_blocks) for pair in range(n_pairs)]
    heads = [(blk, pair, half) for blk, pair in units for half in range(2)]
    kv_of = lambda pair: 2 * (pair // (n_pairs // SW_KV_HEADS))

    scores = {}
    for blk, pair in units:
        rows = slice(blk * SW_BLOCK, (blk + 1) * SW_BLOCK)
        scores[blk, pair] = _dot_nt(qb[rows, pair * LANES:(pair + 1) * LANES],
                                    block_operand(kpads, kpad_ref, blk, kv_of(pair)))

    yield
    qi = lax.broadcasted_iota(jnp.int32, (SW_BLOCK, SW_BLOCK), 0)
    kj = lax.broadcasted_iota(jnp.int32, (SW_BLOCK, SW_BLOCK), 1)
    from_prev = kj > qi
    start_bias = jnp.where(seq_start, -jnp.inf, 0.0)

    merged, sink = {}, {}
    for blk, pair, half in heads:
        s = scores[blk, pair]
        s_prev = s[:, (2 * half) * SW_BLOCK:(2 * half + 1) * SW_BLOCK]
        s_cur = s[:, (2 * half + 1) * SW_BLOCK:(2 * half + 2) * SW_BLOCK]
        if blk == 0:
            s_prev = s_prev + start_bias
        merged[blk, pair, half] = jnp.where(from_prev, s_prev, s_cur)
        sink[blk, pair, half] = sinks_ref[2 * pair + half]
    top = {u: jnp.maximum(jnp.max(merged[u], axis=-1, keepdims=True), sink[u]) for u in heads}
    prob = {u: jnp.exp(merged[u] - top[u]) for u in heads}
    inv = {u: 1.0 / (jnp.sum(prob[u], axis=-1, keepdims=True) + jnp.exp(sink[u] - top[u]))
           for u in heads}

    yield
    low_o = lax.broadcasted_iota(jnp.int32, (SW_BLOCK, LANES), 1) < SW_HEAD_DIM
    for blk, pair in units:
        ps = []
        for half in range(2):
            p = prob[blk, pair, half]
            zero = jnp.zeros_like(p)
            ps += [jnp.where(from_prev, p, zero).astype(BF16),
                   jnp.where(from_prev, zero, p).astype(BF16)]
        o = _dot(jnp.concatenate(ps, axis=1), block_operand(vpads, vpad_ref, blk, kv_of(pair)))
        o = o * jnp.where(low_o, inv[blk, pair, 0], inv[blk, pair, 1])
        out_ref[blk * SW_BLOCK:(blk + 1) * SW_BLOCK, pair * LANES:(pair + 1) * LANES] = (
            o.astype(out_ref.dtype))

    last = slice((n_blocks - 1) * SW_BLOCK, n_blocks * SW_BLOCK)
    for i in range(2 * SW_KV_HEADS):
        kpad_ref[i] = kpads[i][last]
        vpad_ref[i] = vpads[i][last]


def _project_pieces(x_ref, g1_ref, win_ref, slab_ref):
    xn = _rms_norm(x_ref[...], g1_ref[...]).astype(BF16)
    col_step = 2 * LANES

    def piece(c0):
        c1 = min(c0 + col_step, IN_WIDTH)
        proj = _dot(xn, win_ref[:, c0:c1])
        for s in range((c1 - c0) // LANES):
            slab_ref[c0 // LANES + s] = proj[:, s * LANES:(s + 1) * LANES]

    return [functools.partial(piece, c0) for c0 in range(0, IN_WIDTH, col_step)]


def _mix_pieces(slab_ref, seq_start, lb, sinks_ref, ctab_ref, stab_ref, hgn_ref, gq_ref, gk_ref,
                masks_ref, mix_ref, hg_ref, state_ref, kpad_ref, vpad_ref):
    yield from _swa_tile(slab_ref, seq_start, ctab_ref[...], stab_ref[...], gq_ref[...],
                         gk_ref[...], sinks_ref, kpad_ref, vpad_ref,
                         mix_ref.at[:, pl.ds(HG_WIDTH, SW_WIDTH)])
    yield
    yield from _hgrn2_tile(slab_ref, lb, hgn_ref[...], masks_ref, state_ref, hg_ref)
    for h in range(HG_HEADS):
        mix_ref[:, h * HG_DV:(h + 1) * HG_DV] = hg_ref[h].astype(BF16)


STEP_ORDER = "TMPPP TMPPP TMPP TMPP TMMP TMM TM TM".replace(" ", "")


def _layer_kernel(sinks_ref, xp_ref, xt_ref, ctab_ref, stab_ref, g1_ref, win_ref, hlb_ref, hgn_ref,
                  gq_ref, gk_ref, masks_ref, wout_ref, g2_ref, wq_ref, xgq_ref, kmem_ref,
                  vmem_ref, wo_ref, g3_ref, up_ref, down_ref, out_ref,
                  slab_a, slab_b, mix_a, mix_b, hg_ref, state_ref, kpad_ref, vpad_ref,
                  *, tiles_per_seq):
    s = pl.program_id(0)
    seq_start = lax.rem(s + tiles_per_seq - 1, tiles_per_seq) == 0

    @pl.when(s == 0)
    def _():
        slab_b[...] = jnp.zeros_like(slab_b)
        mix_a[...] = jnp.zeros_like(mix_a)

    @pl.when(seq_start | (s == 0))
    def _():
        state_ref[...] = jnp.zeros_like(state_ref)
        kpad_ref[...] = jnp.zeros_like(kpad_ref)
        vpad_ref[...] = jnp.zeros_like(vpad_ref)

    hlb = hlb_ref[...]
    e = jnp.exp(hlb - jnp.max(hlb, axis=0, keepdims=True))
    lb = e[0:1, :] / jnp.sum(e, axis=0, keepdims=True)

    def step(project_slab, mix_slab, mix_out, tail_in):
        streams = {
            "P": iter(_project_pieces(xp_ref, g1_ref, win_ref, project_slab)),
            "M": _mix_pieces(mix_slab, seq_start, lb, sinks_ref, ctab_ref, stab_ref, hgn_ref,
                             gq_ref, gk_ref, masks_ref, mix_out, hg_ref, state_ref,
                             kpad_ref, vpad_ref),
            "T": _tail_stages(xt_ref, tail_in, wout_ref, g2_ref, wq_ref, xgq_ref, kmem_ref.at[0],
                              vmem_ref.at[0], wo_ref, g3_ref, up_ref, down_ref, out_ref),
        }
        for name in STEP_ORDER:
            piece = next(streams[name], None)
            if callable(piece):
                piece()
        for stream in streams.values():
            assert next(stream, "done") == "done"

    @pl.when(s % 2 == 0)
    def _():
        step(slab_a, slab_b, mix_b, mix_a)

    @pl.when(s % 2 == 1)
    def _():
        step(slab_b, slab_a, mix_a, mix_b)


def _layer(x, mem_k, mem_v, ctab, stab, sinks, g1, w_in, hlb, hgn, gq, gk, w_out, g2, wq, xgq, wo,
           g3, up, down):
    B, S, D = x.shape
    M = mem_k.shape[1]
    tile = MIX_TILE
    tiles_per_seq = S // tile
    n_tiles = B * tiles_per_seq
    masks = jnp.asarray(_hgrn2_level_masks())

    def tile_index(lag):
        return lambda s: jnp.clip(s - lag, 0, n_tiles - 1)

    def tok(width, lag):
        index = tile_index(lag)
        return pl.BlockSpec((tile, width), lambda s: (index(s), 0))

    def const(shape):
        return pl.BlockSpec(shape, lambda s: (0,) * len(shape), pipeline_mode=pl.Buffered(1))

    tail_index = tile_index(2)
    per_seq = pl.BlockSpec((1, M, XA_WIDTH), lambda s: (tail_index(s) // tiles_per_seq, 0, 0))
    slab = pltpu.VMEM((N_SLABS, tile, LANES), F32)
    mix_buf = pltpu.VMEM((tile, MIX_WIDTH), BF16)
    x2 = x.reshape(B * S, D)
    out = pl.pallas_call(
        functools.partial(_layer_kernel, tiles_per_seq=tiles_per_seq),
        grid=(n_tiles + 2,),
        in_specs=[pl.BlockSpec(memory_space=pltpu.SMEM),
                  tok(D, 0), tok(D, 2), tok(LANES, 1), tok(LANES, 1),
                  const((1, D)), const(w_in.shape), const(hlb.shape), const((1, HG_DV)),
                  const((1, SW_WIDTH)), const((1, LANES)),
                  const((N_LEVELS, HG_CHUNK, 2 * HG_CHUNK)),
                  const(w_out.shape), const((1, D)), const(wq.shape), const((1, XA_HEAD_DIM)),
                  per_seq, per_seq, const(wo.shape), const((1, D)), const(up.shape),
                  const(down.shape)],
        out_specs=tok(D, 2),
        out_shape=jax.ShapeDtypeStruct((B * S, D), F32),
        scratch_shapes=[slab, slab, mix_buf, mix_buf,
                        pltpu.VMEM((HG_HEADS, tile, LANES), F32),
                        pltpu.VMEM((HG_HEADS, HG_DV, HG_DK), F32),
                        pltpu.VMEM((2 * SW_KV_HEADS, SW_BLOCK, LANES), BF16),
                        pltpu.VMEM((2 * SW_KV_HEADS, SW_BLOCK, LANES), BF16)],
        compiler_params=pltpu.CompilerParams(
            dimension_semantics=("arbitrary",),
            vmem_limit_bytes=VMEM_LIMIT_BYTES),
        name="layer",
    )(sinks, x2, x2, ctab.reshape(B * S, LANES), stab.reshape(B * S, LANES), g1, w_in, hlb, hgn,
      gq, gk, masks, w_out, g2, wq, xgq, mem_k, mem_v, wo, g3, up, down)
    return out.reshape(B, S, D)


def _tail_stages(x_ref, mix_ref, wout_ref, g2_ref, wq_ref, gq_ref, k_ref, v_ref, wo_ref,
                 g3_ref, up_ref, down_ref, out_ref):
    h1 = x_ref[...] + _dot(mix_ref[...], wout_ref[...])
    yield

    q = _dot(_rms_norm(h1, g2_ref[...]).astype(BF16), wq_ref[...])
    yield

    lanes = [slice(h * XA_HEAD_DIM, (h + 1) * XA_HEAD_DIM) for h in range(XA_HEADS)]
    qn = [_rms_norm(q[:, sl], gq_ref[...]) * (XA_HEAD_DIM ** -0.5) for sl in lanes]
    s = [_dot_nt(a.astype(BF16), k_ref[:, sl]) for a, sl in zip(qn, lanes)]
    p = [jnp.exp(a - jnp.max(a, axis=-1, keepdims=True)) for a in s]
    inv = [1.0 / jnp.sum(a, axis=-1, keepdims=True) for a in p]
    heads = [_dot(a.astype(BF16), v_ref[:, sl]) * b for a, b, sl in zip(p, inv, lanes)]
    yield

    h2 = h1 + _dot(jnp.concatenate(heads, axis=1).astype(BF16), wo_ref[...])
    out_ref[...] = h2
    hn = _rms_norm(h2, g3_ref[...]).astype(BF16)
    for c0 in range(0, up_ref.shape[1], FF_CHUNK):
        yield
        a = jnp.maximum(_dot(hn, up_ref[:, c0:c0 + FF_CHUNK]), 0.0)
        out_ref[...] += _dot((a * a).astype(BF16), down_ref[c0:c0 + FF_CHUNK, :])


def kernel(x, mem, positions, norm1_g, w_in, hg_lower_bounds, hg_norm_g, sw_q_norm_g, sw_k_norm_g,
           sw_sinks, w_out, norm2_g, mem_norm_g, xa_wq, xa_wkv, xa_q_norm_g, xa_k_norm_g, xa_wo,
           norm3_g, mlp_up, mlp_down):
    depth = norm1_g.shape[0]
    assert depth == 1 and x.shape[1] % MIX_TILE == 0
    ctab, stab = _rope_tables(positions)
    h = x
    for l in range(depth):
        kmem, vmem = _mem_kv(mem, mem_norm_g[l][None], xa_wkv[l].astype(BF16), xa_k_norm_g[l][None])
        h = _layer(h, kmem, vmem, ctab, stab, sw_sinks[l], norm1_g[l][None], w_in[l].astype(BF16),
                   hg_lower_bounds, hg_norm_g[l][None], jnp.tile(sw_q_norm_g[l], SW_HEADS)[None],
                   jnp.tile(sw_k_norm_g[l], SW_KV_HEADS)[None], w_out[l].astype(BF16),
                   norm2_g[l][None], xa_wq[l].astype(BF16), xa_q_norm_g[l][None],
                   xa_wo[l].astype(BF16), norm3_g[l][None], mlp_up[l].astype(BF16),
                   mlp_down[l].astype(BF16))
    return h
```

```python
import functools

import numpy as np
import jax
import jax.numpy as jnp
from jax import lax
from jax.experimental import pallas as pl
from jax.experimental.pallas import tpu as pltpu

F32 = jnp.float32
BF16 = jnp.bfloat16

LANES = 128
VMEM_LIMIT_BYTES = 56 * 1024 * 1024

EPS = 1e-6

HG_HEADS = 4
HG_DK = 128
HG_DV = 128
HG_WIDTH = HG_HEADS * HG_DV
HG_KEY_WIDTH = HG_HEADS * HG_DK

SW_HEADS = 8
SW_KV_HEADS = 2
SW_HEAD_DIM = 64
SW_WIDTH = SW_HEADS * SW_HEAD_DIM
SW_KV_WIDTH = SW_KV_HEADS * SW_HEAD_DIM
WINDOW = 128
SW_BLOCK = 128
assert WINDOW == SW_BLOCK
ROPE_THETA = 500000.0
ROT_DIM = SW_HEAD_DIM // 4
ROT_HALF = ROT_DIM // 2

MIX_WIDTH = HG_WIDTH + SW_WIDTH
IN_WIDTH = 2 * HG_KEY_WIDTH + 2 * HG_WIDTH + SW_WIDTH + 2 * SW_KV_WIDTH

XA_HEADS = 4
XA_HEAD_DIM = 128
XA_WIDTH = XA_HEADS * XA_HEAD_DIM

N_SLABS = IN_WIDTH // LANES
SLAB_Q, SLAB_F, SLAB_I, SLAB_G = 0, 4, 8, 12
SLAB_SQ, SLAB_SK, SLAB_SV = 16, 20, 21

HG_CHUNK = 64
N_PLANES = 8
PLANE_ROWS = HG_CHUNK // N_PLANES
N_LEVELS = 3

MIX_TILE = 256
FF_CHUNK = 1024


def _mul(a, b):
    if a is None:
        return b
    if b is None:
        return a
    return a * b


def _rms_norm(x, g):
    return x * lax.rsqrt(jnp.mean(x * x, axis=-1, keepdims=True) + EPS) * g


def _dot(a, b):
    return jnp.dot(a, b, preferred_element_type=F32)


def _dot_nt(a, b):
    return lax.dot_general(a, b, (((1,), (1,)), ((), ())), preferred_element_type=F32)


def _dot_tn(a, b):
    return lax.dot_general(a, b, (((0,), (0,)), ((), ())), preferred_element_type=F32)


BF16_SPLIT_PARTS = 3
ROPE_POSITIONS_PER_STEP = 8192


def _rope_table_kernel(pos_ref, invf_ref, ecos_ref, esin_ref, base_ref, ctab_ref, stab_ref):
    ang = pos_ref[0].astype(F32) * invf_ref[...]

    def spread(t, e_ref):
        parts, rest = [], t
        for _ in range(BF16_SPLIT_PARTS):
            piece = rest.astype(BF16).astype(F32)
            parts.append(piece)
            rest = rest - piece
        parts.append(jnp.zeros_like(t))
        return _dot_tn(jnp.concatenate(parts, axis=0).astype(BF16), e_ref[...])

    ctab_ref[0] = spread(jnp.cos(ang), ecos_ref) + base_ref[...]
    stab_ref[0] = spread(jnp.sin(ang), esin_ref)


def _rope_tables(positions):
    B, S = positions.shape
    inv_freq = ROPE_THETA ** (-(jnp.arange(ROT_HALF, dtype=F32) * 2.0 / ROT_DIM))
    dim = np.arange(LANES) % SW_HEAD_DIM
    freq = np.arange(ROT_HALF)[:, None]
    first, second = dim[None, :] == freq, dim[None, :] == freq + ROT_HALF
    pad = np.zeros((ROT_HALF, LANES))
    stack = lambda e: jnp.asarray(np.concatenate([e] * BF16_SPLIT_PARTS + [pad]), BF16)
    ecos = stack(first * 1.0 + second * 1.0)
    esin = stack(second * 1.0 - first * 1.0)
    base = jnp.asarray((dim >= ROT_DIM)[None, :], F32)
    rows = (BF16_SPLIT_PARTS + 1) * ROT_HALF
    const = lambda shape: pl.BlockSpec(shape, lambda i: (0, 0))
    n_pos = min(ROPE_POSITIONS_PER_STEP, B * S)
    assert (B * S) % n_pos == 0
    ctab, stab = pl.pallas_call(
        _rope_table_kernel,
        grid=(B * S // n_pos,),
        in_specs=[pl.BlockSpec((1, 1, n_pos), lambda i: (i, 0, 0)), const((ROT_HALF, 1)),
                  const((rows, LANES)), const((rows, LANES)), const((1, LANES))],
        out_specs=[pl.BlockSpec((1, n_pos, LANES), lambda i: (i, 0, 0))] * 2,
        out_shape=[jax.ShapeDtypeStruct((B * S // n_pos, n_pos, LANES), F32)] * 2,
        compiler_params=pltpu.CompilerParams(vmem_limit_bytes=VMEM_LIMIT_BYTES),
        name="rope_tables",
    )(positions.reshape(B * S // n_pos, 1, n_pos), inv_freq.reshape(ROT_HALF, 1), ecos, esin, base)
    return ctab.reshape(B, S, LANES), stab.reshape(B, S, LANES)


def _rope(t, ctab, stab):
    width = t.shape[1]
    reps = width // LANES
    c = jnp.concatenate([ctab] * reps, axis=1) if reps > 1 else ctab
    s = jnp.concatenate([stab] * reps, axis=1) if reps > 1 else stab
    lane = lax.broadcasted_iota(jnp.int32, t.shape, 1)
    first_half = (lane % SW_HEAD_DIM) < ROT_HALF
    partner = jnp.where(first_half,
                        pltpu.roll(t, width - ROT_HALF, 1),
                        pltpu.roll(t, ROT_HALF, 1))
    return t * c + partner * s


MEM_ROWS_PER_STEP = 1024


def _mem_kv_kernel(mem_ref, g_ref, wkv_ref, gk_ref, k_ref, v_ref):
    mn = _rms_norm(mem_ref[...], g_ref[...]).astype(BF16)
    kv = _dot(mn, wkv_ref[...])
    for h in range(XA_HEADS):
        sl = slice(h * XA_HEAD_DIM, (h + 1) * XA_HEAD_DIM)
        k_ref[:, sl] = _rms_norm(kv[:, sl], gk_ref[...]).astype(BF16)
    v_ref[...] = kv[:, XA_WIDTH:].astype(BF16)


def _mem_kv(mem, g, wkv, gk):
    B, M, D = mem.shape
    rows = min(MEM_ROWS_PER_STEP, B * M)
    assert (B * M) % rows == 0
    k, v = pl.pallas_call(
        _mem_kv_kernel,
        grid=(B * M // rows,),
        in_specs=[pl.BlockSpec((rows, D), lambda i: (i, 0)),
                  pl.BlockSpec((1, D), lambda i: (0, 0)),
                  pl.BlockSpec((D, 2 * XA_WIDTH), lambda i: (0, 0)),
                  pl.BlockSpec((1, XA_HEAD_DIM), lambda i: (0, 0))],
        out_specs=[pl.BlockSpec((rows, XA_WIDTH), lambda i: (i, 0))] * 2,
        out_shape=[jax.ShapeDtypeStruct((B * M, XA_WIDTH), BF16)] * 2,
        compiler_params=pltpu.CompilerParams(vmem_limit_bytes=VMEM_LIMIT_BYTES),
        name="mem_kv",
    )(mem.reshape(B * M, D), g, wkv, gk)
    return k.reshape(B, M, XA_WIDTH), v.reshape(B, M, XA_WIDTH)


def _hgrn2_level_masks():
    p = np.arange(HG_CHUNK)
    tok = N_PLANES * (p % PLANE_ROWS) + p // PLANE_ROWS
    ti, tj = tok[:, None], tok[None, :]
    masks = []
    h = N_PLANES
    while h < HG_CHUNK:
        masks.append((ti // (2 * h) == tj // (2 * h)) & ((ti // h) % 2 == 1) & ((tj // h) % 2 == 0))
        h *= 2
    masks = np.stack(masks).astype(np.float32)
    assert masks.shape[0] == N_LEVELS
    in_group = (ti // N_PLANES == tj // N_PLANES) & (tj <= ti)
    assert (masks.sum(0) + in_group == (tj <= ti)).all()
    return np.tile(masks, (1, 1, 2))


def _plane_block_products(f):
    pre = {1: list(f)}
    suf = {1: [None] * N_PLANES}
    h = 1
    while h < N_PLANES:
        p, s = pre[h], suf[h]
        new_p, new_s = [], []
        for r in range(N_PLANES):
            blk = r // h
            if blk % 2 == 1:
                new_p.append(_mul(p[r], p[blk * h - 1]))
                new_s.append(s[r])
            else:
                new_p.append(p[r])
                new_s.append(_mul(s[r], p[(blk + 2) * h - 1]))
        pre[2 * h], suf[2 * h] = new_p, new_s
        h *= 2
    return pre, suf


def _group_block_products(total):
    rows = total.shape[0]
    m_idx = lax.broadcasted_iota(jnp.int32, total.shape, 0) % PLANE_ROWS
    wpre, wsuf = [None], [None]
    for d in range(1, PLANE_ROWS):
        wpre.append(_mul(wpre[-1], pltpu.roll(total, d, 0)))
        wsuf.append(_mul(wsuf[-1], pltpu.roll(total, rows - d, 0)))
    cpre, csuf = {1: None}, {1: None}
    u = 2
    while u <= PLANE_ROWS:
        off = m_idx % u
        a = jnp.ones_like(total)
        b = jnp.ones_like(total)
        for d in range(1, u):
            a = jnp.where(off == d, wpre[d], a)
            b = jnp.where(off == u - 1 - d, wsuf[d], b)
        cpre[u], csuf[u] = a, b
        u *= 2
    return cpre, csuf


def _block_diag(a, b):
    za, zb = jnp.zeros_like(a), jnp.zeros_like(b)
    return jnp.concatenate([jnp.concatenate([a, zb], axis=1),
                            jnp.concatenate([za, b], axis=1)], axis=0)


def _hgrn2_tile(slab_ref, lb, norm_g, masks_ref, state_ref, out_ref):
    tile = slab_ref.shape[1]
    n_chunks = tile // HG_CHUNK
    plane_len = tile // N_PLANES
    pair_w = 2 * HG_DK
    n_pairs = HG_HEADS // 2

    def planes(slab0):
        return [jnp.concatenate(
            [slab_ref[slab0 + h, pl.ds(r, plane_len, stride=N_PLANES), :] for h in range(HG_HEADS)],
            axis=1) for r in range(N_PLANES)]

    def chunk_rows(ps, c, lanes=slice(None)):
        return jnp.concatenate([p[c * PLANE_ROWS:(c + 1) * PLANE_ROWS, lanes] for p in ps], axis=0)

    def pair_lanes(p):
        return slice(p * pair_w, (p + 1) * pair_w)

    def heads_block_diag(x):
        return _block_diag(x[:, :HG_DK], x[:, HG_DK:])

    lb_gap = 1.0 - lb
    f = [lb + lb_gap * jax.nn.sigmoid(x) for x in planes(SLAB_F)]
    q = planes(SLAB_Q)
    k = [1.0 - x for x in f]
    v = planes(SLAB_I)

    pre, suf = _plane_block_products(f)
    pre8, suf8 = pre[N_PLANES], suf[N_PLANES]
    assert 2 ** (N_LEVELS - 1) * N_PLANES * 2 == HG_CHUNK
    cpre, csuf = _group_block_products(pre8[N_PLANES - 1])

    q8 = [_mul(a, b) for a, b in zip(q, pre8)]
    k8 = [_mul(a, b) for a, b in zip(k, suf8)]
    def per_head(x, fn):
        return jnp.concatenate(
            [fn(x[:, h * HG_DK:(h + 1) * HG_DK], h) for h in range(HG_HEADS)], axis=1)

    group_out = [None] * N_PLANES
    for rj in range(N_PLANES):
        decay = None
        for ri in range(rj, N_PLANES):
            if ri > rj:
                decay = _mul(decay, f[ri])
            weighted = _mul(q[ri] * k[rj], decay)
            part = per_head(weighted, lambda a, h: jnp.sum(a, axis=-1, keepdims=True)
                            * v[rj][:, h * HG_DV:(h + 1) * HG_DV])
            group_out[ri] = part if group_out[ri] is None else group_out[ri] + part
        if rj % 4 == 3:
            yield

    scores = [[None] * n_pairs for _ in range(n_chunks)]
    u = 1
    for lvl in range(N_LEVELS):
        ql = [_mul(x, cpre[u]) for x in q8]
        kl = [_mul(x, csuf[u]) for x in k8]
        for c in range(n_chunks):
            for p in range(n_pairs):
                lhs = chunk_rows(ql, c, pair_lanes(p)).astype(BF16)
                rhs = heads_block_diag(chunk_rows(kl, c, pair_lanes(p)).astype(BF16))
                s = _dot_nt(lhs, rhs) * masks_ref[lvl]
                scores[c][p] = s if scores[c][p] is None else scores[c][p] + s
        u *= 2
        if lvl != 1:
            yield

    q_full = [x * cpre[PLANE_ROWS] for x in q8]
    k_full = [x * csuf[PLANE_ROWS] for x in k8]
    decay = pre8[N_PLANES - 1] * cpre[PLANE_ROWS]

    states = [state_ref[h] for h in range(HG_HEADS)]
    outs = []
    for c in range(n_chunks):
        last = (c + 1) * PLANE_ROWS - 1
        o_pairs = []
        for p in range(n_pairs):
            lanes = pair_lanes(p)
            vc = chunk_rows(v, c, lanes).astype(BF16)
            st = _block_diag(states[2 * p], states[2 * p + 1]).astype(BF16)
            o = (_dot_nt(chunk_rows(q_full, c, lanes).astype(BF16), st)
                 + _dot(scores[c][p].astype(BF16), heads_block_diag(vc)))
            upd = _dot_tn(vc, chunk_rows(k_full, c, lanes).astype(BF16))
            for i in range(2):
                h = 2 * p + i
                blk = slice(i * HG_DK, (i + 1) * HG_DK)
                states[h] = (states[h] * decay[last:last + 1, h * HG_DK:(h + 1) * HG_DK]
                             + upd[blk, blk])
            o_pairs.append(o)
        outs.append(jnp.concatenate(o_pairs, axis=1))
    for h in range(HG_HEADS):
        state_ref[h] = states[h]
    yield

    gate = planes(SLAB_G)
    for r in range(N_PLANES):
        o_r = (jnp.concatenate([o[r * PLANE_ROWS:(r + 1) * PLANE_ROWS] for o in outs], axis=0)
               + group_out[r])
        g_r = gate[r]
        for h in range(HG_HEADS):
            lanes = slice(h * HG_DV, (h + 1) * HG_DV)
            y = _rms_norm(o_r[:, lanes], norm_g) * (g_r[:, lanes] * jax.nn.sigmoid(g_r[:, lanes]))
            out_ref[h, pl.ds(r, plane_len, stride=N_PLANES), :] = y


def _swa_tile(slab_ref, seq_start, ctab, stab, gq, gk, seg_mean, sinks_ref, kpad_ref, vpad_ref,
              out_ref):
    tile = slab_ref.shape[1]
    n_blocks = tile // SW_BLOCK
    n_pairs = SW_HEADS // 2
    q = jnp.concatenate([slab_ref[SLAB_SQ + s] for s in range(SW_WIDTH // LANES)], axis=1)
    k = slab_ref[SLAB_SK]
    v = slab_ref[SLAB_SV]

    def head_norm(t, g, mean_mat):
        ms = _dot((t * t).astype(BF16), mean_mat)
        return t * lax.rsqrt(ms + EPS) * g

    qn = _rope(head_norm(q, gq, seg_mean[...]), ctab, stab) * (SW_HEAD_DIM ** -0.5)
    kn = _rope(head_norm(k, gk, seg_mean[:LANES, :LANES]), ctab, stab)

    lane = lax.broadcasted_iota(jnp.int32, (tile, LANES), 1)
    low = lane < SW_HEAD_DIM

    def padded(t):
        rolled = pltpu.roll(t, SW_HEAD_DIM, 1)
        zero = jnp.zeros_like(t)
        out = []
        for kv_head in range(SW_KV_HEADS):
            src_even, src_odd = (t, rolled) if kv_head == 0 else (rolled, t)
            out.append(jnp.where(low, src_even, zero).astype(BF16))
            out.append(jnp.where(low, zero, src_odd).astype(BF16))
        return out

    kpads, vpads = padded(kn), padded(v)
    qb = qn.astype(BF16)
    yield

    def block_operand(pads, carry_ref, blk, kv):
        parts = []
        for i in (kv, kv + 1):
            prev = carry_ref[i] if blk == 0 else pads[i][(blk - 1) * SW_BLOCK:blk * SW_BLOCK]
            parts += [prev, pads[i][blk * SW_BLOCK:(blk + 1) * SW_BLOCK]]
        return jnp.concatenate(parts, axis=0)

    units = [(blk, pair) for blk in range(n_blocks) for pair in range(n_pairs)]
    heads = [(blk, pair, half) for blk, pair in units for half in range(2)]
    kv_of = lambda pair: 2 * (pair // (n_pairs // SW_KV_HEADS))

    scores = {}
    for blk, pair in units:
        rows = slice(blk * SW_BLOCK, (blk + 1) * SW_BLOCK)
        scores[blk, pair] = _dot_nt(qb[rows, pair * LANES:(pair + 1) * LANES],
                                    block_operand(kpads, kpad_ref, blk, kv_of(pair)))

    yield
    qi = lax.broadcasted_iota(jnp.int32, (SW_BLOCK, SW_BLOCK), 0)
    kj = lax.broadcasted_iota(jnp.int32, (SW_BLOCK, SW_BLOCK), 1)
    from_prev = kj > qi
    start_bias = jnp.where(seq_start, -jnp.inf, 0.0)

    merged, sink = {}, {}
    for blk, pair, half in heads:
        s = scores[blk, pair]
        s_prev = s[:, (2 * half) * SW_BLOCK:(2 * half + 1) * SW_BLOCK]
        s_cur = s[:, (2 * half + 1) * SW_BLOCK:(2 * half + 2) * SW_BLOCK]
        if blk == 0:
            s_prev = s_prev + start_bias
        merged[blk, pair, half] = jnp.where(from_prev, s_prev, s_cur)
        sink[blk, pair, half] = sinks_ref[2 * pair + half]
    top = {u: jnp.maximum(jnp.max(merged[u], axis=-1, keepdims=True), sink[u]) for u in heads}
    prob = {u: jnp.exp(merged[u] - top[u]) for u in heads}
    inv = {u: 1.0 / (jnp.sum(prob[u], axis=-1, keepdims=True) + jnp.exp(sink[u] - top[u]))
           for u in heads}

    yield
    low_o = lax.broadcasted_iota(jnp.int32, (SW_BLOCK, LANES), 1) < SW_HEAD_DIM
    for blk, pair in units:
        ps = []
        for half in range(2):
            p = prob[blk, pair, half]
            zero = jnp.zeros_like(p)
            ps += [jnp.where(from_prev, p, zero).astype(BF16),
                   jnp.where(from_prev, zero, p).astype(BF16)]
        o = _dot(jnp.concatenate(ps, axis=1), block_operand(vpads, vpad_ref, blk, kv_of(pair)))
        o = o * jnp.where(low_o, inv[blk, pair, 0], inv[blk, pair, 1])
        out_ref[blk * SW_BLOCK:(blk + 1) * SW_BLOCK, pair * LANES:(pair + 1) * LANES] = (
            o.astype(out_ref.dtype))

    last = slice((n_blocks - 1) * SW_BLOCK, n_blocks * SW_BLOCK)
    for i in range(2 * SW_KV_HEADS):
        kpad_ref[i] = kpads[i][last]
        vpad_ref[i] = vpads[i][last]


def _project_pieces(x_ref, g1_ref, win_ref, slab_ref):
    xn = _rms_norm(x_ref[...], g1_ref[...]).astype(BF16)
    col_step = 2 * LANES

    def piece(c0):
        c1 = min(c0 + col_step, IN_WIDTH)
        proj = _dot(xn, win_ref[:, c0:c1])
        for s in range((c1 - c0) // LANES):
            slab_ref[c0 // LANES + s] = proj[:, s * LANES:(s + 1) * LANES]

    return [functools.partial(piece, c0) for c0 in range(0, IN_WIDTH, col_step)]


def _mix_pieces(slab_ref, seq_start, lb, sinks_ref, ctab_ref, stab_ref, hgn_ref, gq_ref, gk_ref,
                segm_ref, masks_ref, mix_ref, hg_ref, state_ref, kpad_ref, vpad_ref):
    yield from _swa_tile(slab_ref, seq_start, ctab_ref[...], stab_ref[...], gq_ref[...],
                         gk_ref[...], segm_ref, sinks_ref, kpad_ref, vpad_ref,
                         mix_ref.at[:, pl.ds(HG_WIDTH, SW_WIDTH)])
    yield
    yield from _hgrn2_tile(slab_ref, lb, hgn_ref[...], masks_ref, state_ref, hg_ref)
    for h in range(HG_HEADS):
        mix_ref[:, h * HG_DV:(h + 1) * HG_DV] = hg_ref[h].astype(BF16)


STEP_ORDER = "TMPPP TMPPP TMPP TMPP TMMP TMM TM TM".replace(" ", "")


WEIGHT_CHUNK_ROWS = 128
WEIGHT_STAGE_WIDTH = 4096


def _load_weights(pairs, stage_ref, sem):
    jobs = [(src, dst, r0) for src, dst in pairs for r0 in range(0, src.shape[0], WEIGHT_CHUNK_ROWS)]

    def copy(i):
        src, _, r0 = jobs[i]
        return pltpu.make_async_copy(src.at[pl.ds(r0, WEIGHT_CHUNK_ROWS), :],
                                     stage_ref.at[i % 2, :, pl.ds(0, src.shape[1])],
                                     sem.at[i % 2])

    copy(0).start()
    for i, (src, dst, r0) in enumerate(jobs):
        if i + 1 < len(jobs):
            copy(i + 1).start()
        copy(i).wait()
        dst[pl.ds(r0, WEIGHT_CHUNK_ROWS), :] = (
            stage_ref[i % 2, :, pl.ds(0, src.shape[1])].astype(BF16))


def _layer_kernel(sinks_ref, xp_ref, xt_ref, ctab_ref, stab_ref, g1_ref, win_hbm, hlb_ref, hgn_ref,
                  gq_ref, gk_ref, segm_ref, masks_ref, wout_hbm, g2_ref, wq_hbm, xgq_ref, kmem_ref,
                  vmem_ref, wo_hbm, g3_ref, up_hbm, down_hbm, out_ref,
                  slab_a, slab_b, mix_a, mix_b, hg_ref, state_ref, kpad_ref, vpad_ref,
                  win_ref, wout_ref, wq_ref, wo_ref, up_ref, down_ref, stage_ref, stage_sem,
                  *, tiles_per_seq):
    s = pl.program_id(0)
    seq_start = lax.rem(s + tiles_per_seq - 1, tiles_per_seq) == 0

    @pl.when(s == 0)
    def _():
        slab_b[...] = jnp.zeros_like(slab_b)
        mix_a[...] = jnp.zeros_like(mix_a)
        _load_weights([(win_hbm, win_ref), (wout_hbm, wout_ref), (wq_hbm, wq_ref),
                       (wo_hbm, wo_ref), (up_hbm, up_ref), (down_hbm, down_ref)],
                      stage_ref, stage_sem)

    @pl.when(seq_start | (s == 0))
    def _():
        state_ref[...] = jnp.zeros_like(state_ref)
        kpad_ref[...] = jnp.zeros_like(kpad_ref)
        vpad_ref[...] = jnp.zeros_like(vpad_ref)

    hlb = hlb_ref[...]
    e = jnp.exp(hlb - jnp.max(hlb, axis=0, keepdims=True))
    lb = e[0:1, :] / jnp.sum(e, axis=0, keepdims=True)

    def step(project_slab, mix_slab, mix_out, tail_in):
        streams = {
            "P": iter(_project_pieces(xp_ref, g1_ref, win_ref, project_slab)),
            "M": _mix_pieces(mix_slab, seq_start, lb, sinks_ref, ctab_ref, stab_ref, hgn_ref,
                             gq_ref, gk_ref, segm_ref, masks_ref, mix_out, hg_ref, state_ref,
                             kpad_ref, vpad_ref),
            "T": _tail_stages(xt_ref, tail_in, wout_ref, g2_ref, wq_ref, xgq_ref, kmem_ref.at[0],
                              vmem_ref.at[0], wo_ref, g3_ref, up_ref, down_ref, out_ref),
        }
        for name in STEP_ORDER:
            piece = next(streams[name], None)
            if callable(piece):
                piece()
        for stream in streams.values():
            assert next(stream, "done") == "done"

    @pl.when(s % 2 == 0)
    def _():
        step(slab_a, slab_b, mix_b, mix_a)

    @pl.when(s % 2 == 1)
    def _():
        step(slab_b, slab_a, mix_a, mix_b)


def _layer(x, mem_k, mem_v, ctab, stab, sinks, g1, w_in, hlb, hgn, gq, gk, w_out, g2, wq, xgq, wo,
           g3, up, down):
    B, S, D = x.shape
    M = mem_k.shape[1]
    tile = MIX_TILE
    tiles_per_seq = S // tile
    n_tiles = B * tiles_per_seq
    seg_mean = jnp.asarray(np.kron(np.eye(SW_HEADS), np.full((SW_HEAD_DIM, SW_HEAD_DIM),
                                                              1.0 / SW_HEAD_DIM)), BF16)
    masks = jnp.asarray(_hgrn2_level_masks())

    def tile_index(lag):
        return lambda s: jnp.clip(s - lag, 0, n_tiles - 1)

    def tok(width, lag):
        index = tile_index(lag)
        return pl.BlockSpec((tile, width), lambda s: (index(s), 0))

    def const(shape):
        return pl.BlockSpec(shape, lambda s: (0,) * len(shape), pipeline_mode=pl.Buffered(1))

    tail_index = tile_index(2)
    per_seq = pl.BlockSpec((1, M, XA_WIDTH), lambda s: (tail_index(s) // tiles_per_seq, 0, 0))
    slab = pltpu.VMEM((N_SLABS, tile, LANES), F32)
    mix_buf = pltpu.VMEM((tile, MIX_WIDTH), BF16)
    x2 = x.reshape(B * S, D)
    weights = (w_in, w_out, wq, wo, up, down)
    for w in weights:
        assert w.dtype == F32 and w.shape[0] % WEIGHT_CHUNK_ROWS == 0
        assert w.shape[1] % LANES == 0 and w.shape[1] <= WEIGHT_STAGE_WIDTH
    in_hbm = pl.BlockSpec(memory_space=pl.ANY)
    out = pl.pallas_call(
        functools.partial(_layer_kernel, tiles_per_seq=tiles_per_seq),
        grid=(n_tiles + 2,),
        in_specs=[pl.BlockSpec(memory_space=pltpu.SMEM),
                  tok(D, 0), tok(D, 2), tok(LANES, 1), tok(LANES, 1),
                  const((1, D)), in_hbm, const(hlb.shape), const((1, HG_DV)),
                  const((1, SW_WIDTH)), const((1, LANES)), const((SW_WIDTH, SW_WIDTH)),
                  const((N_LEVELS, HG_CHUNK, 2 * HG_CHUNK)),
                  in_hbm, const((1, D)), in_hbm, const((1, XA_HEAD_DIM)),
                  per_seq, per_seq, in_hbm, const((1, D)), in_hbm, in_hbm],
        out_specs=tok(D, 2),
        out_shape=jax.ShapeDtypeStruct((B * S, D), F32),
        scratch_shapes=[slab, slab, mix_buf, mix_buf,
                        pltpu.VMEM((HG_HEADS, tile, LANES), F32),
                        pltpu.VMEM((HG_HEADS, HG_DV, HG_DK), F32),
                        pltpu.VMEM((2 * SW_KV_HEADS, SW_BLOCK, LANES), BF16),
                        pltpu.VMEM((2 * SW_KV_HEADS, SW_BLOCK, LANES), BF16)]
                       + [pltpu.VMEM(w.shape, BF16) for w in weights]
                       + [pltpu.VMEM((2, WEIGHT_CHUNK_ROWS, WEIGHT_STAGE_WIDTH), F32),
                          pltpu.SemaphoreType.DMA((2,))],
        compiler_params=pltpu.CompilerParams(
            dimension_semantics=("arbitrary",),
            vmem_limit_bytes=VMEM_LIMIT_BYTES),
        name="layer",
    )(sinks, x2, x2, ctab.reshape(B * S, LANES), stab.reshape(B * S, LANES), g1, w_in, hlb, hgn,
      gq, gk, seg_mean, masks, w_out, g2, wq, xgq, mem_k, mem_v, wo, g3, up, down)
    return out.reshape(B, S, D)


def _tail_stages(x_ref, mix_ref, wout_ref, g2_ref, wq_ref, gq_ref, k_ref, v_ref, wo_ref,
                 g3_ref, up_ref, down_ref, out_ref):
    h1 = x_ref[...] + _dot(mix_ref[...], wout_ref[...])
    yield

    q = _dot(_rms_norm(h1, g2_ref[...]).astype(BF16), wq_ref[...])
    yield

    lanes = [slice(h * XA_HEAD_DIM, (h + 1) * XA_HEAD_DIM) for h in range(XA_HEADS)]
    qn = [_rms_norm(q[:, sl], gq_ref[...]) * (XA_HEAD_DIM ** -0.5) for sl in lanes]
    s = [_dot_nt(a.astype(BF16), k_ref[:, sl]) for a, sl in zip(qn, lanes)]
    p = [jnp.exp(a - jnp.max(a, axis=-1, keepdims=True)) for a in s]
    inv = [1.0 / jnp.sum(a, axis=-1, keepdims=True) for a in p]
    heads = [_dot(a.astype(BF16), v_ref[:, sl]) * b for a, b, sl in zip(p, inv, lanes)]
    yield

    h2 = h1 + _dot(jnp.concatenate(heads, axis=1).astype(BF16), wo_ref[...])
    out_ref[...] = h2
    hn = _rms_norm(h2, g3_ref[...]).astype(BF16)
    for c0 in range(0, up_ref.shape[1], FF_CHUNK):
        yield
        a = jnp.maximum(_dot(hn, up_ref[:, c0:c0 + FF_CHUNK]), 0.0)
        out_ref[...] += _dot((a * a).astype(BF16), down_ref[c0:c0 + FF_CHUNK, :])


def kernel(x, mem, positions, norm1_g, w_in, hg_lower_bounds, hg_norm_g, sw_q_norm_g, sw_k_norm_g,
           sw_sinks, w_out, norm2_g, mem_norm_g, xa_wq, xa_wkv, xa_q_norm_g, xa_k_norm_g, xa_wo,
           norm3_g, mlp_up, mlp_down):
    depth = norm1_g.shape[0]
    assert depth == 1 and x.shape[1] % MIX_TILE == 0
    ctab, stab = _rope_tables(positions)
    h = x
    for l in range(depth):
        kmem, vmem = _mem_kv(mem, mem_norm_g[l][None], xa_wkv[l].astype(BF16), xa_k_norm_g[l][None])
        h = _layer(h, kmem, vmem, ctab, stab, sw_sinks[l], norm1_g[l][None], w_in[l],
                   hg_lower_bounds, hg_norm_g[l][None], jnp.tile(sw_q_norm_g[l], SW_HEADS)[None],
                   jnp.tile(sw_k_norm_g[l], SW_KV_HEADS)[None], w_out[l],
                   norm2_g[l][None], xa_wq[l], xa_q_norm_g[l][None],
                   xa_wo[l], norm3_g[l][None], mlp_up[l], mlp_down[l])
    return h
```

```python
import functools

import numpy as np
import jax
import jax.numpy as jnp
from jax import lax
from jax.experimental import pallas as pl
from jax.experimental.pallas import tpu as pltpu

F32 = jnp.float32
BF16 = jnp.bfloat16

LANES = 128
VMEM_LIMIT_BYTES = 56 * 1024 * 1024

EPS = 1e-6

HG_HEADS = 4
HG_DK = 128
HG_DV = 128
HG_WIDTH = HG_HEADS * HG_DV
HG_KEY_WIDTH = HG_HEADS * HG_DK

SW_HEADS = 8
SW_KV_HEADS = 2
SW_HEAD_DIM = 64
SW_WIDTH = SW_HEADS * SW_HEAD_DIM
SW_KV_WIDTH = SW_KV_HEADS * SW_HEAD_DIM
WINDOW = 128
SW_BLOCK = 128
assert WINDOW == SW_BLOCK
ROPE_THETA = 500000.0
ROT_DIM = SW_HEAD_DIM // 4
ROT_HALF = ROT_DIM // 2

MIX_WIDTH = HG_WIDTH + SW_WIDTH
IN_WIDTH = 2 * HG_KEY_WIDTH + 2 * HG_WIDTH + SW_WIDTH + 2 * SW_KV_WIDTH

XA_HEADS = 4
XA_HEAD_DIM = 128
XA_WIDTH = XA_HEADS * XA_HEAD_DIM

N_SLABS = IN_WIDTH // LANES
SLAB_Q, SLAB_F, SLAB_I, SLAB_G = 0, 4, 8, 12
SLAB_SQ, SLAB_SK, SLAB_SV = 16, 20, 21

HG_CHUNK = 64
N_PLANES = 8
PLANE_ROWS = HG_CHUNK // N_PLANES
N_LEVELS = 3

MIX_TILE = 256
FF_CHUNK = 1024


def _mul(a, b):
    if a is None:
        return b
    if b is None:
        return a
    return a * b


def _rms_norm(x, g):
    return x * lax.rsqrt(jnp.mean(x * x, axis=-1, keepdims=True) + EPS) * g


def _dot(a, b):
    return jnp.dot(a, b, preferred_element_type=F32)


def _dot_nt(a, b):
    return lax.dot_general(a, b, (((1,), (1,)), ((), ())), preferred_element_type=F32)


def _dot_tn(a, b):
    return lax.dot_general(a, b, (((0,), (0,)), ((), ())), preferred_element_type=F32)


BF16_SPLIT_PARTS = 3
ROPE_POSITIONS_PER_STEP = 8192


def _rope_table_kernel(pos_ref, invf_ref, ecos_ref, esin_ref, base_ref, ctab_ref, stab_ref):
    ang = pos_ref[0].astype(F32) * invf_ref[...]

    def spread(t, e_ref):
        parts, rest = [], t
        for _ in range(BF16_SPLIT_PARTS):
            piece = rest.astype(BF16).astype(F32)
            parts.append(piece)
            rest = rest - piece
        parts.append(jnp.zeros_like(t))
        return _dot_tn(jnp.concatenate(parts, axis=0).astype(BF16), e_ref[...])

    ctab_ref[0] = spread(jnp.cos(ang), ecos_ref) + base_ref[...]
    stab_ref[0] = spread(jnp.sin(ang), esin_ref)


def _rope_tables(positions):
    B, S = positions.shape
    inv_freq = ROPE_THETA ** (-(jnp.arange(ROT_HALF, dtype=F32) * 2.0 / ROT_DIM))
    dim = np.arange(LANES) % SW_HEAD_DIM
    freq = np.arange(ROT_HALF)[:, None]
    first, second = dim[None, :] == freq, dim[None, :] == freq + ROT_HALF
    pad = np.zeros((ROT_HALF, LANES))
    stack = lambda e: jnp.asarray(np.concatenate([e] * BF16_SPLIT_PARTS + [pad]), BF16)
    ecos = stack(first * 1.0 + second * 1.0)
    esin = stack(second * 1.0 - first * 1.0)
    base = jnp.asarray((dim >= ROT_DIM)[None, :], F32)
    rows = (BF16_SPLIT_PARTS + 1) * ROT_HALF
    const = lambda shape: pl.BlockSpec(shape, lambda i: (0, 0))
    n_pos = min(ROPE_POSITIONS_PER_STEP, B * S)
    assert (B * S) % n_pos == 0
    ctab, stab = pl.pallas_call(
        _rope_table_kernel,
        grid=(B * S // n_pos,),
        in_specs=[pl.BlockSpec((1, 1, n_pos), lambda i: (i, 0, 0)), const((ROT_HALF, 1)),
                  const((rows, LANES)), const((rows, LANES)), const((1, LANES))],
        out_specs=[pl.BlockSpec((1, n_pos, LANES), lambda i: (i, 0, 0))] * 2,
        out_shape=[jax.ShapeDtypeStruct((B * S // n_pos, n_pos, LANES), F32)] * 2,
        compiler_params=pltpu.CompilerParams(vmem_limit_bytes=VMEM_LIMIT_BYTES),
        name="rope_tables",
    )(positions.reshape(B * S // n_pos, 1, n_pos), inv_freq.reshape(ROT_HALF, 1), ecos, esin, base)
    return ctab.reshape(B, S, LANES), stab.reshape(B, S, LANES)


def _rope(t, ctab, stab):
    width = t.shape[1]
    reps = width // LANES
    c = jnp.concatenate([ctab] * reps, axis=1) if reps > 1 else ctab
    s = jnp.concatenate([stab] * reps, axis=1) if reps > 1 else stab
    lane = lax.broadcasted_iota(jnp.int32, t.shape, 1)
    first_half = (lane % SW_HEAD_DIM) < ROT_HALF
    partner = jnp.where(first_half,
                        pltpu.roll(t, width - ROT_HALF, 1),
                        pltpu.roll(t, ROT_HALF, 1))
    return t * c + partner * s


MEM_ROWS_PER_STEP = 1024


def _mem_kv_kernel(mem_ref, g_ref, wkv_ref, gk_ref, k_ref, v_ref):
    mn = _rms_norm(mem_ref[...], g_ref[...]).astype(BF16)
    kv = _dot(mn, wkv_ref[...])
    for h in range(XA_HEADS):
        sl = slice(h * XA_HEAD_DIM, (h + 1) * XA_HEAD_DIM)
        k_ref[:, sl] = _rms_norm(kv[:, sl], gk_ref[...]).astype(BF16)
    v_ref[...] = kv[:, XA_WIDTH:].astype(BF16)


def _mem_kv(mem, g, wkv, gk):
    B, M, D = mem.shape
    rows = min(MEM_ROWS_PER_STEP, B * M)
    assert (B * M) % rows == 0
    k, v = pl.pallas_call(
        _mem_kv_kernel,
        grid=(B * M // rows,),
        in_specs=[pl.BlockSpec((rows, D), lambda i: (i, 0)),
                  pl.BlockSpec((1, D), lambda i: (0, 0)),
                  pl.BlockSpec((D, 2 * XA_WIDTH), lambda i: (0, 0)),
                  pl.BlockSpec((1, XA_HEAD_DIM), lambda i: (0, 0))],
        out_specs=[pl.BlockSpec((rows, XA_WIDTH), lambda i: (i, 0))] * 2,
        out_shape=[jax.ShapeDtypeStruct((B * M, XA_WIDTH), BF16)] * 2,
        compiler_params=pltpu.CompilerParams(vmem_limit_bytes=VMEM_LIMIT_BYTES),
        name="mem_kv",
    )(mem.reshape(B * M, D), g, wkv, gk)
    return k.reshape(B, M, XA_WIDTH), v.reshape(B, M, XA_WIDTH)


def _hgrn2_level_masks():
    p = np.arange(HG_CHUNK)
    tok = N_PLANES * (p % PLANE_ROWS) + p // PLANE_ROWS
    ti, tj = tok[:, None], tok[None, :]
    masks = []
    h = N_PLANES
    while h < HG_CHUNK:
        masks.append((ti // (2 * h) == tj // (2 * h)) & ((ti // h) % 2 == 1) & ((tj // h) % 2 == 0))
        h *= 2
    masks = np.stack(masks).astype(np.float32)
    assert masks.shape[0] == N_LEVELS
    in_group = (ti // N_PLANES == tj // N_PLANES) & (tj <= ti)
    assert (masks.sum(0) + in_group == (tj <= ti)).all()
    return np.tile(masks, (1, 1, 2))


def _plane_block_products(f):
    pre = {1: list(f)}
    suf = {1: [None] * N_PLANES}
    h = 1
    while h < N_PLANES:
        p, s = pre[h], suf[h]
        new_p, new_s = [], []
        for r in range(N_PLANES):
            blk = r // h
            if blk % 2 == 1:
                new_p.append(_mul(p[r], p[blk * h - 1]))
                new_s.append(s[r])
            else:
                new_p.append(p[r])
                new_s.append(_mul(s[r], p[(blk + 2) * h - 1]))
        pre[2 * h], suf[2 * h] = new_p, new_s
        h *= 2
    return pre, suf


def _group_block_products(total):
    rows = total.shape[0]
    m_idx = lax.broadcasted_iota(jnp.int32, total.shape, 0) % PLANE_ROWS
    wpre, wsuf = [None], [None]
    for d in range(1, PLANE_ROWS):
        wpre.append(_mul(wpre[-1], pltpu.roll(total, d, 0)))
        wsuf.append(_mul(wsuf[-1], pltpu.roll(total, rows - d, 0)))
    cpre, csuf = {1: None}, {1: None}
    u = 2
    while u <= PLANE_ROWS:
        off = m_idx % u
        a = jnp.ones_like(total)
        b = jnp.ones_like(total)
        for d in range(1, u):
            a = jnp.where(off == d, wpre[d], a)
            b = jnp.where(off == u - 1 - d, wsuf[d], b)
        cpre[u], csuf[u] = a, b
        u *= 2
    return cpre, csuf


def _block_diag(a, b):
    za, zb = jnp.zeros_like(a), jnp.zeros_like(b)
    return jnp.concatenate([jnp.concatenate([a, zb], axis=1),
                            jnp.concatenate([za, b], axis=1)], axis=0)


def _hgrn2_tile(slab_ref, lb, norm_g, masks_ref, state_ref, out_ref):
    tile = slab_ref.shape[1]
    n_chunks = tile // HG_CHUNK
    plane_len = tile // N_PLANES
    pair_w = 2 * HG_DK
    n_pairs = HG_HEADS // 2

    def planes(slab0):
        return [jnp.concatenate(
            [slab_ref[slab0 + h, pl.ds(r, plane_len, stride=N_PLANES), :] for h in range(HG_HEADS)],
            axis=1) for r in range(N_PLANES)]

    def chunk_rows(ps, c, lanes=slice(None)):
        return jnp.concatenate([p[c * PLANE_ROWS:(c + 1) * PLANE_ROWS, lanes] for p in ps], axis=0)

    def pair_lanes(p):
        return slice(p * pair_w, (p + 1) * pair_w)

    def heads_block_diag(x):
        return _block_diag(x[:, :HG_DK], x[:, HG_DK:])

    lb_gap = 1.0 - lb
    f = [lb + lb_gap * jax.nn.sigmoid(x) for x in planes(SLAB_F)]
    q = planes(SLAB_Q)
    k = [1.0 - x for x in f]
    v = planes(SLAB_I)

    pre, suf = _plane_block_products(f)
    pre8, suf8 = pre[N_PLANES], suf[N_PLANES]
    assert 2 ** (N_LEVELS - 1) * N_PLANES * 2 == HG_CHUNK
    cpre, csuf = _group_block_products(pre8[N_PLANES - 1])

    q8 = [_mul(a, b) for a, b in zip(q, pre8)]
    k8 = [_mul(a, b) for a, b in zip(k, suf8)]
    def per_head(x, fn):
        return jnp.concatenate(
            [fn(x[:, h * HG_DK:(h + 1) * HG_DK], h) for h in range(HG_HEADS)], axis=1)

    group_out = [None] * N_PLANES
    for rj in range(N_PLANES):
        decay = None
        for ri in range(rj, N_PLANES):
            if ri > rj:
                decay = _mul(decay, f[ri])
            weighted = _mul(q[ri] * k[rj], decay)
            part = per_head(weighted, lambda a, h: jnp.sum(a, axis=-1, keepdims=True)
                            * v[rj][:, h * HG_DV:(h + 1) * HG_DV])
            group_out[ri] = part if group_out[ri] is None else group_out[ri] + part
        if rj % 4 == 3:
            yield

    scores = [[None] * n_pairs for _ in range(n_chunks)]
    u = 1
    for lvl in range(N_LEVELS):
        ql = [_mul(x, cpre[u]) for x in q8]
        kl = [_mul(x, csuf[u]) for x in k8]
        for c in range(n_chunks):
            for p in range(n_pairs):
                lhs = chunk_rows(ql, c, pair_lanes(p)).astype(BF16)
                rhs = heads_block_diag(chunk_rows(kl, c, pair_lanes(p)).astype(BF16))
                s = _dot_nt(lhs, rhs) * masks_ref[lvl]
                scores[c][p] = s if scores[c][p] is None else scores[c][p] + s
        u *= 2
        if lvl != 1:
            yield

    q_full = [x * cpre[PLANE_ROWS] for x in q8]
    k_full = [x * csuf[PLANE_ROWS] for x in k8]
    decay = pre8[N_PLANES - 1] * cpre[PLANE_ROWS]

    states = [state_ref[h] for h in range(HG_HEADS)]
    outs = []
    for c in range(n_chunks):
        last = (c + 1) * PLANE_ROWS - 1
        o_pairs = []
        for p in range(n_pairs):
            lanes = pair_lanes(p)
            vc = chunk_rows(v, c, lanes).astype(BF16)
            st = _block_diag(states[2 * p], states[2 * p + 1]).astype(BF16)
            o = (_dot_nt(chunk_rows(q_full, c, lanes).astype(BF16), st)
                 + _dot(scores[c][p].astype(BF16), heads_block_diag(vc)))
            upd = _dot_tn(vc, chunk_rows(k_full, c, lanes).astype(BF16))
            for i in range(2):
                h = 2 * p + i
                blk = slice(i * HG_DK, (i + 1) * HG_DK)
                states[h] = (states[h] * decay[last:last + 1, h * HG_DK:(h + 1) * HG_DK]
                             + upd[blk, blk])
            o_pairs.append(o)
        outs.append(jnp.concatenate(o_pairs, axis=1))
    for h in range(HG_HEADS):
        state_ref[h] = states[h]
    yield

    gate = planes(SLAB_G)
    for r in range(N_PLANES):
        o_r = (jnp.concatenate([o[r * PLANE_ROWS:(r + 1) * PLANE_ROWS] for o in outs], axis=0)
               + group_out[r])
        g_r = gate[r]
        for h in range(HG_HEADS):
            lanes = slice(h * HG_DV, (h + 1) * HG_DV)
            y = _rms_norm(o_r[:, lanes], norm_g) * (g_r[:, lanes] * jax.nn.sigmoid(g_r[:, lanes]))
            out_ref[h, pl.ds(r, plane_len, stride=N_PLANES), :] = y


def _swa_tile(slab_ref, seq_start, ctab, stab, gq, gk, seg_mean, sinks_ref, kpad_ref, vpad_ref,
              out_ref):
    tile = slab_ref.shape[1]
    n_blocks = tile // SW_BLOCK
    n_pairs = SW_HEADS // 2
    q = jnp.concatenate([slab_ref[SLAB_SQ + s] for s in range(SW_WIDTH // LANES)], axis=1)
    k = slab_ref[SLAB_SK]
    v = slab_ref[SLAB_SV]

    def head_norm(t, g, mean_mat):
        ms = _dot((t * t).astype(BF16), mean_mat)
        return t * lax.rsqrt(ms + EPS) * g

    qn = _rope(head_norm(q, gq, seg_mean[...]), ctab, stab) * (SW_HEAD_DIM ** -0.5)
    kn = _rope(head_norm(k, gk, seg_mean[:LANES, :LANES]), ctab, stab)

    lane = lax.broadcasted_iota(jnp.int32, (tile, LANES), 1)
    low = lane < SW_HEAD_DIM

    def padded(t):
        rolled = pltpu.roll(t, SW_HEAD_DIM, 1)
        zero = jnp.zeros_like(t)
        out = []
        for kv_head in range(SW_KV_HEADS):
            src_even, src_odd = (t, rolled) if kv_head == 0 else (rolled, t)
            out.append(jnp.where(low, src_even, zero).astype(BF16))
            out.append(jnp.where(low, zero, src_odd).astype(BF16))
        return out

    kpads, vpads = padded(kn), padded(v)
    qb = qn.astype(BF16)
    yield

    def block_operand(pads, carry_ref, blk, kv):
        parts = []
        for i in (kv, kv + 1):
            prev = carry_ref[i] if blk == 0 else pads[i][(blk - 1) * SW_BLOCK:blk * SW_BLOCK]
            parts += [prev, pads[i][blk * SW_BLOCK:(blk + 1) * SW_BLOCK]]
        return jnp.concatenate(parts, axis=0)

    units = [(blk, pair) for blk in range(n_blocks) for pair in range(n_pairs)]
    heads = [(blk, pair, half) for blk, pair in units for half in range(2)]
    kv_of = lambda pair: 2 * (pair // (n_pairs // SW_KV_HEADS))

    scores = {}
    for blk, pair in units:
        rows = slice(blk * SW_BLOCK, (blk + 1) * SW_BLOCK)
        scores[blk, pair] = _dot_nt(qb[rows, pair * LANES:(pair + 1) * LANES],
                                    block_operand(kpads, kpad_ref, blk, kv_of(pair)))

    yield
    qi = lax.broadcasted_iota(jnp.int32, (SW_BLOCK, SW_BLOCK), 0)
    kj = lax.broadcasted_iota(jnp.int32, (SW_BLOCK, SW_BLOCK), 1)
    from_prev = kj > qi
    start_bias = jnp.where(seq_start, -jnp.inf, 0.0)

    merged, sink = {}, {}
    for blk, pair, half in heads:
        s = scores[blk, pair]
        s_prev = s[:, (2 * half) * SW_BLOCK:(2 * half + 1) * SW_BLOCK]
        s_cur = s[:, (2 * half + 1) * SW_BLOCK:(2 * half + 2) * SW_BLOCK]
        if blk == 0:
            s_prev = s_prev + start_bias
        merged[blk, pair, half] = jnp.where(from_prev, s_prev, s_cur)
        sink[blk, pair, half] = sinks_ref[2 * pair + half]
    top = {u: jnp.maximum(jnp.max(merged[u], axis=-1, keepdims=True), sink[u]) for u in heads}
    prob = {u: jnp.exp(merged[u] - top[u]) for u in heads}
    inv = {u: 1.0 / (jnp.sum(prob[u], axis=-1, keepdims=True) + jnp.exp(sink[u] - top[u]))
           for u in heads}

    yield
    low_o = lax.broadcasted_iota(jnp.int32, (SW_BLOCK, LANES), 1) < SW_HEAD_DIM
    for blk, pair in units:
        ps = []
        for half in range(2):
            p = prob[blk, pair, half]
            zero = jnp.zeros_like(p)
            ps += [jnp.where(from_prev, p, zero).astype(BF16),
                   jnp.where(from_prev, zero, p).astype(BF16)]
        o = _dot(jnp.concatenate(ps, axis=1), block_operand(vpads, vpad_ref, blk, kv_of(pair)))
        o = o * jnp.where(low_o, inv[blk, pair, 0], inv[blk, pair, 1])
        out_ref[blk * SW_BLOCK:(blk + 1) * SW_BLOCK, pair * LANES:(pair + 1) * LANES] = (
            o.astype(out_ref.dtype))

    last = slice((n_blocks - 1) * SW_BLOCK, n_blocks * SW_BLOCK)
    for i in range(2 * SW_KV_HEADS):
        kpad_ref[i] = kpads[i][last]
        vpad_ref[i] = vpads[i][last]


def _project_pieces(x_ref, g1_ref, win_ref, slab_ref):
    xn = _rms_norm(x_ref[...], g1_ref[...]).astype(BF16)
    col_step = 2 * LANES

    def piece(c0):
        c1 = min(c0 + col_step, IN_WIDTH)
        proj = _dot(xn, win_ref[:, c0:c1])
        for s in range((c1 - c0) // LANES):
            slab_ref[c0 // LANES + s] = proj[:, s * LANES:(s + 1) * LANES]

    return [functools.partial(piece, c0) for c0 in range(0, IN_WIDTH, col_step)]


def _mix_pieces(slab_ref, seq_start, lb, sinks_ref, ctab_ref, stab_ref, hgn_ref, gq_ref, gk_ref,
                segm_ref, masks_ref, mix_ref, hg_ref, state_ref, kpad_ref, vpad_ref):
    yield from _swa_tile(slab_ref, seq_start, ctab_ref[...], stab_ref[...], gq_ref[...],
                         gk_ref[...], segm_ref, sinks_ref, kpad_ref, vpad_ref,
                         mix_ref.at[:, pl.ds(HG_WIDTH, SW_WIDTH)])
    yield
    yield from _hgrn2_tile(slab_ref, lb, hgn_ref[...], masks_ref, state_ref, hg_ref)
    for h in range(HG_HEADS):
        mix_ref[:, h * HG_DV:(h + 1) * HG_DV] = hg_ref[h].astype(BF16)


STEP_ORDER = "TMPPP TMPPP TMPP TMPP TMMP TMM TM TM".replace(" ", "")


WEIGHT_CHUNK_ROWS = 512
WEIGHT_STAGE_WIDTH = 1024


def _load_weights(pairs, stage_ref, sem):
    jobs = [(src, dst, r0, c0, min(WEIGHT_STAGE_WIDTH, src.shape[1] - c0))
            for src, dst in pairs
            for r0 in range(0, src.shape[0], WEIGHT_CHUNK_ROWS)
            for c0 in range(0, src.shape[1], WEIGHT_STAGE_WIDTH)]

    def copy(i):
        src, _, r0, c0, width = jobs[i]
        return pltpu.make_async_copy(src.at[pl.ds(r0, WEIGHT_CHUNK_ROWS), pl.ds(c0, width)],
                                     stage_ref.at[i % 2, :, pl.ds(0, width)],
                                     sem.at[i % 2])

    copy(0).start()
    for i, (_, dst, r0, c0, width) in enumerate(jobs):
        if i + 1 < len(jobs):
            copy(i + 1).start()
        copy(i).wait()
        dst[pl.ds(r0, WEIGHT_CHUNK_ROWS), pl.ds(c0, width)] = (
            stage_ref[i % 2, :, pl.ds(0, width)].astype(BF16))


def _layer_kernel(sinks_ref, xp_ref, xt_ref, ctab_ref, stab_ref, g1_ref, win_hbm, hlb_ref, hgn_ref,
                  gq_ref, gk_ref, segm_ref, masks_ref, wout_hbm, g2_ref, wq_hbm, xgq_ref, kmem_ref,
                  vmem_ref, wo_hbm, g3_ref, up_hbm, down_hbm, out_ref,
                  slab_a, slab_b, mix_a, mix_b, hg_ref, state_ref, kpad_ref, vpad_ref,
                  win_ref, wout_ref, wq_ref, wo_ref, up_ref, down_ref, stage_ref, stage_sem,
                  *, tiles_per_seq):
    s = pl.program_id(0)
    seq_start = lax.rem(s + tiles_per_seq - 1, tiles_per_seq) == 0

    @pl.when(s == 0)
    def _():
        slab_b[...] = jnp.zeros_like(slab_b)
        mix_a[...] = jnp.zeros_like(mix_a)
        _load_weights([(win_hbm, win_ref), (wout_hbm, wout_ref), (wq_hbm, wq_ref),
                       (wo_hbm, wo_ref), (up_hbm, up_ref), (down_hbm, down_ref)],
                      stage_ref, stage_sem)

    @pl.when(seq_start | (s == 0))
    def _():
        state_ref[...] = jnp.zeros_like(state_ref)
        kpad_ref[...] = jnp.zeros_like(kpad_ref)
        vpad_ref[...] = jnp.zeros_like(vpad_ref)

    hlb = hlb_ref[...]
    e = jnp.exp(hlb - jnp.max(hlb, axis=0, keepdims=True))
    lb = e[0:1, :] / jnp.sum(e, axis=0, keepdims=True)

    def step(project_slab, mix_slab, mix_out, tail_in):
        streams = {
            "P": iter(_project_pieces(xp_ref, g1_ref, win_ref, project_slab)),
            "M": _mix_pieces(mix_slab, seq_start, lb, sinks_ref, ctab_ref, stab_ref, hgn_ref,
                             gq_ref, gk_ref, segm_ref, masks_ref, mix_out, hg_ref, state_ref,
                             kpad_ref, vpad_ref),
            "T": _tail_stages(xt_ref, tail_in, wout_ref, g2_ref, wq_ref, xgq_ref, kmem_ref.at[0],
                              vmem_ref.at[0], wo_ref, g3_ref, up_ref, down_ref, out_ref),
        }
        for name in STEP_ORDER:
            piece = next(streams[name], None)
            if callable(piece):
                piece()
        for stream in streams.values():
            assert next(stream, "done") == "done"

    @pl.when(s % 2 == 0)
    def _():
        step(slab_a, slab_b, mix_b, mix_a)

    @pl.when(s % 2 == 1)
    def _():
        step(slab_b, slab_a, mix_a, mix_b)


def _layer(x, mem_k, mem_v, ctab, stab, sinks, g1, w_in, hlb, hgn, gq, gk, w_out, g2, wq, xgq, wo,
           g3, up, down):
    B, S, D = x.shape
    M = mem_k.shape[1]
    tile = MIX_TILE
    tiles_per_seq = S // tile
    n_tiles = B * tiles_per_seq
    seg_mean = jnp.asarray(np.kron(np.eye(SW_HEADS), np.full((SW_HEAD_DIM, SW_HEAD_DIM),
                                                              1.0 / SW_HEAD_DIM)), BF16)
    masks = jnp.asarray(_hgrn2_level_masks())

    def tile_index(lag):
        return lambda s: jnp.clip(s - lag, 0, n_tiles - 1)

    def tok(width, lag):
        index = tile_index(lag)
        return pl.BlockSpec((tile, width), lambda s: (index(s), 0))

    def const(shape):
        return pl.BlockSpec(shape, lambda s: (0,) * len(shape), pipeline_mode=pl.Buffered(1))

    tail_index = tile_index(2)
    per_seq = pl.BlockSpec((1, M, XA_WIDTH), lambda s: (tail_index(s) // tiles_per_seq, 0, 0))
    slab = pltpu.VMEM((N_SLABS, tile, LANES), F32)
    mix_buf = pltpu.VMEM((tile, MIX_WIDTH), BF16)
    x2 = x.reshape(B * S, D)
    weights = (w_in, w_out, wq, wo, up, down)
    for w in weights:
        assert w.dtype == F32 and w.shape[0] % WEIGHT_CHUNK_ROWS == 0
        assert w.shape[1] % LANES == 0
    in_hbm = pl.BlockSpec(memory_space=pl.ANY)
    out = pl.pallas_call(
        functools.partial(_layer_kernel, tiles_per_seq=tiles_per_seq),
        grid=(n_tiles + 2,),
        in_specs=[pl.BlockSpec(memory_space=pltpu.SMEM),
                  tok(D, 0), tok(D, 2), tok(LANES, 1), tok(LANES, 1),
                  const((1, D)), in_hbm, const(hlb.shape), const((1, HG_DV)),
                  const((1, SW_WIDTH)), const((1, LANES)), const((SW_WIDTH, SW_WIDTH)),
                  const((N_LEVELS, HG_CHUNK, 2 * HG_CHUNK)),
                  in_hbm, const((1, D)), in_hbm, const((1, XA_HEAD_DIM)),
                  per_seq, per_seq, in_hbm, const((1, D)), in_hbm, in_hbm],
        out_specs=tok(D, 2),
        out_shape=jax.ShapeDtypeStruct((B * S, D), F32),
        scratch_shapes=[slab, slab, mix_buf, mix_buf,
                        pltpu.VMEM((HG_HEADS, tile, LANES), F32),
                        pltpu.VMEM((HG_HEADS, HG_DV, HG_DK), F32),
                        pltpu.VMEM((2 * SW_KV_HEADS, SW_BLOCK, LANES), BF16),
                        pltpu.VMEM((2 * SW_KV_HEADS, SW_BLOCK, LANES), BF16)]
                       + [pltpu.VMEM(w.shape, BF16) for w in weights]
                       + [pltpu.VMEM((2, WEIGHT_CHUNK_ROWS, WEIGHT_STAGE_WIDTH), F32),
                          pltpu.SemaphoreType.DMA((2,))],
        compiler_params=pltpu.CompilerParams(
            dimension_semantics=("arbitrary",),
            vmem_limit_bytes=VMEM_LIMIT_BYTES),
        name="layer",
    )(sinks, x2, x2, ctab.reshape(B * S, LANES), stab.reshape(B * S, LANES), g1, w_in, hlb, hgn,
      gq, gk, seg_mean, masks, w_out, g2, wq, xgq, mem_k, mem_v, wo, g3, up, down)
    return out.reshape(B, S, D)


def _tail_stages(x_ref, mix_ref, wout_ref, g2_ref, wq_ref, gq_ref, k_ref, v_ref, wo_ref,
                 g3_ref, up_ref, down_ref, out_ref):
    h1 = x_ref[...] + _dot(mix_ref[...], wout_ref[...])
    yield

    q = _dot(_rms_norm(h1, g2_ref[...]).astype(BF16), wq_ref[...])
    yield

    lanes = [slice(h * XA_HEAD_DIM, (h + 1) * XA_HEAD_DIM) for h in range(XA_HEADS)]
    qn = [_rms_norm(q[:, sl], gq_ref[...]) * (XA_HEAD_DIM ** -0.5) for sl in lanes]
    s = [_dot_nt(a.astype(BF16), k_ref[:, sl]) for a, sl in zip(qn, lanes)]
    p = [jnp.exp(a - jnp.max(a, axis=-1, keepdims=True)) for a in s]
    inv = [1.0 / jnp.sum(a, axis=-1, keepdims=True) for a in p]
    heads = [_dot(a.astype(BF16), v_ref[:, sl]) * b for a, b, sl in zip(p, inv, lanes)]
    yield

    h2 = h1 + _dot(jnp.concatenate(heads, axis=1).astype(BF16), wo_ref[...])
    out_ref[...] = h2
    hn = _rms_norm(h2, g3_ref[...]).astype(BF16)
    for c0 in range(0, up_ref.shape[1], FF_CHUNK):
        yield
        a = jnp.maximum(_dot(hn, up_ref[:, c0:c0 + FF_CHUNK]), 0.0)
        out_ref[...] += _dot((a * a).astype(BF16), down_ref[c0:c0 + FF_CHUNK, :])


def kernel(x, mem, positions, norm1_g, w_in, hg_lower_bounds, hg_norm_g, sw_q_norm_g, sw_k_norm_g,
           sw_sinks, w_out, norm2_g, mem_norm_g, xa_wq, xa_wkv, xa_q_norm_g, xa_k_norm_g, xa_wo,
           norm3_g, mlp_up, mlp_down):
    depth = norm1_g.shape[0]
    assert depth == 1 and x.shape[1] % MIX_TILE == 0
    ctab, stab = _rope_tables(positions)
    h = x
    for l in range(depth):
        kmem, vmem = _mem_kv(mem, mem_norm_g[l][None], xa_wkv[l].astype(BF16), xa_k_norm_g[l][None])
        h = _layer(h, kmem, vmem, ctab, stab, sw_sinks[l], norm1_g[l][None], w_in[l],
                   hg_lower_bounds, hg_norm_g[l][None], jnp.tile(sw_q_norm_g[l], SW_HEADS)[None],
                   jnp.tile(sw_k_norm_g[l], SW_KV_HEADS)[None], w_out[l],
                   norm2_g[l][None], xa_wq[l], xa_q_norm_g[l][None],
                   xa_wo[l], norm3_g[l][None], mlp_up[l], mlp_down[l])
    return h
```

```python
import functools

import numpy as np
import jax
import jax.numpy as jnp
from jax import lax
from jax.experimental import pallas as pl
from jax.experimental.pallas import tpu as pltpu

F32 = jnp.float32
BF16 = jnp.bfloat16

LANES = 128
VMEM_LIMIT_BYTES = 56 * 1024 * 1024

EPS = 1e-6

HG_HEADS = 4
HG_DK = 128
HG_DV = 128
HG_WIDTH = HG_HEADS * HG_DV
HG_KEY_WIDTH = HG_HEADS * HG_DK

SW_HEADS = 8
SW_KV_HEADS = 2
SW_HEAD_DIM = 64
SW_WIDTH = SW_HEADS * SW_HEAD_DIM
SW_KV_WIDTH = SW_KV_HEADS * SW_HEAD_DIM
WINDOW = 128
SW_BLOCK = 128
assert WINDOW == SW_BLOCK
ROPE_THETA = 500000.0
ROT_DIM = SW_HEAD_DIM // 4
ROT_HALF = ROT_DIM // 2

MIX_WIDTH = HG_WIDTH + SW_WIDTH
IN_WIDTH = 2 * HG_KEY_WIDTH + 2 * HG_WIDTH + SW_WIDTH + 2 * SW_KV_WIDTH

XA_HEADS = 4
XA_HEAD_DIM = 128
XA_WIDTH = XA_HEADS * XA_HEAD_DIM

N_SLABS = IN_WIDTH // LANES
SLAB_Q, SLAB_F, SLAB_I, SLAB_G = 0, 4, 8, 12
SLAB_SQ, SLAB_SK, SLAB_SV = 16, 20, 21

HG_CHUNK = 64
N_PLANES = 8
PLANE_ROWS = HG_CHUNK // N_PLANES
N_LEVELS = 3

MIX_TILE = 256
FF_CHUNK = 1024


def _mul(a, b):
    if a is None:
        return b
    if b is None:
        return a
    return a * b


def _rms_norm(x, g):
    return x * lax.rsqrt(jnp.mean(x * x, axis=-1, keepdims=True) + EPS) * g


def _dot(a, b):
    return jnp.dot(a, b, preferred_element_type=F32)


def _dot_nt(a, b):
    return lax.dot_general(a, b, (((1,), (1,)), ((), ())), preferred_element_type=F32)


def _dot_tn(a, b):
    return lax.dot_general(a, b, (((0,), (0,)), ((), ())), preferred_element_type=F32)


BF16_SPLIT_PARTS = 3
ROPE_POSITIONS_PER_STEP = 8192


def _rope_table_kernel(pos_ref, invf_ref, ecos_ref, esin_ref, base_ref, ctab_ref, stab_ref):
    ang = pos_ref[0].astype(F32) * invf_ref[...]

    def spread(t, e_ref):
        parts, rest = [], t
        for _ in range(BF16_SPLIT_PARTS):
            piece = rest.astype(BF16).astype(F32)
            parts.append(piece)
            rest = rest - piece
        parts.append(jnp.zeros_like(t))
        return _dot_tn(jnp.concatenate(parts, axis=0).astype(BF16), e_ref[...])

    ctab_ref[0] = spread(jnp.cos(ang), ecos_ref) + base_ref[...]
    stab_ref[0] = spread(jnp.sin(ang), esin_ref)


def _rope_tables(positions):
    B, S = positions.shape
    inv_freq = ROPE_THETA ** (-(jnp.arange(ROT_HALF, dtype=F32) * 2.0 / ROT_DIM))
    dim = np.arange(LANES) % SW_HEAD_DIM
    freq = np.arange(ROT_HALF)[:, None]
    first, second = dim[None, :] == freq, dim[None, :] == freq + ROT_HALF
    pad = np.zeros((ROT_HALF, LANES))
    stack = lambda e: jnp.asarray(np.concatenate([e] * BF16_SPLIT_PARTS + [pad]), BF16)
    ecos = stack(first * 1.0 + second * 1.0)
    esin = stack(second * 1.0 - first * 1.0)
    base = jnp.asarray((dim >= ROT_DIM)[None, :], F32)
    rows = (BF16_SPLIT_PARTS + 1) * ROT_HALF
    const = lambda shape: pl.BlockSpec(shape, lambda i: (0, 0))
    n_pos = min(ROPE_POSITIONS_PER_STEP, B * S)
    assert (B * S) % n_pos == 0
    ctab, stab = pl.pallas_call(
        _rope_table_kernel,
        grid=(B * S // n_pos,),
        in_specs=[pl.BlockSpec((1, 1, n_pos), lambda i: (i, 0, 0)), const((ROT_HALF, 1)),
                  const((rows, LANES)), const((rows, LANES)), const((1, LANES))],
        out_specs=[pl.BlockSpec((1, n_pos, LANES), lambda i: (i, 0, 0))] * 2,
        out_shape=[jax.ShapeDtypeStruct((B * S // n_pos, n_pos, LANES), F32)] * 2,
        compiler_params=pltpu.CompilerParams(vmem_limit_bytes=VMEM_LIMIT_BYTES),
        name="rope_tables",
    )(positions.reshape(B * S // n_pos, 1, n_pos), inv_freq.reshape(ROT_HALF, 1), ecos, esin, base)
    return ctab.reshape(B, S, LANES), stab.reshape(B, S, LANES)


def _rope(t, ctab, stab):
    width = t.shape[1]
    reps = width // LANES
    c = jnp.concatenate([ctab] * reps, axis=1) if reps > 1 else ctab
    s = jnp.concatenate([stab] * reps, axis=1) if reps > 1 else stab
    lane = lax.broadcasted_iota(jnp.int32, t.shape, 1)
    first_half = (lane % SW_HEAD_DIM) < ROT_HALF
    partner = jnp.where(first_half,
                        pltpu.roll(t, width - ROT_HALF, 1),
                        pltpu.roll(t, ROT_HALF, 1))
    return t * c + partner * s


MEM_ROWS_PER_STEP = 1024


def _mem_kv_kernel(mem_ref, g_ref, wkv_ref, gk_ref, k_ref, v_ref):
    mn = _rms_norm(mem_ref[...], g_ref[...]).astype(BF16)
    kv = _dot(mn, wkv_ref[...])
    for h in range(XA_HEADS):
        sl = slice(h * XA_HEAD_DIM, (h + 1) * XA_HEAD_DIM)
        k_ref[:, sl] = _rms_norm(kv[:, sl], gk_ref[...]).astype(BF16)
    v_ref[...] = kv[:, XA_WIDTH:].astype(BF16)


def _mem_kv(mem, g, wkv, gk):
    B, M, D = mem.shape
    rows = min(MEM_ROWS_PER_STEP, B * M)
    assert (B * M) % rows == 0
    k, v = pl.pallas_call(
        _mem_kv_kernel,
        grid=(B * M // rows,),
        in_specs=[pl.BlockSpec((rows, D), lambda i: (i, 0)),
                  pl.BlockSpec((1, D), lambda i: (0, 0)),
                  pl.BlockSpec((D, 2 * XA_WIDTH), lambda i: (0, 0)),
                  pl.BlockSpec((1, XA_HEAD_DIM), lambda i: (0, 0))],
        out_specs=[pl.BlockSpec((rows, XA_WIDTH), lambda i: (i, 0))] * 2,
        out_shape=[jax.ShapeDtypeStruct((B * M, XA_WIDTH), BF16)] * 2,
        compiler_params=pltpu.CompilerParams(vmem_limit_bytes=VMEM_LIMIT_BYTES),
        name="mem_kv",
    )(mem.reshape(B * M, D), g, wkv, gk)
    return k.reshape(B, M, XA_WIDTH), v.reshape(B, M, XA_WIDTH)


def _hgrn2_level_masks():
    p = np.arange(HG_CHUNK)
    tok = N_PLANES * (p % PLANE_ROWS) + p // PLANE_ROWS
    ti, tj = tok[:, None], tok[None, :]
    masks = []
    h = N_PLANES
    while h < HG_CHUNK:
        masks.append((ti // (2 * h) == tj // (2 * h)) & ((ti // h) % 2 == 1) & ((tj // h) % 2 == 0))
        h *= 2
    masks = np.stack(masks).astype(np.float32)
    assert masks.shape[0] == N_LEVELS
    in_group = (ti // N_PLANES == tj // N_PLANES) & (tj <= ti)
    assert (masks.sum(0) + in_group == (tj <= ti)).all()
    return np.tile(masks, (1, 1, 2))


def _plane_block_products(f):
    pre = {1: list(f)}
    suf = {1: [None] * N_PLANES}
    h = 1
    while h < N_PLANES:
        p, s = pre[h], suf[h]
        new_p, new_s = [], []
        for r in range(N_PLANES):
            blk = r // h
            if blk % 2 == 1:
                new_p.append(_mul(p[r], p[blk * h - 1]))
                new_s.append(s[r])
            else:
                new_p.append(p[r])
                new_s.append(_mul(s[r], p[(blk + 2) * h - 1]))
        pre[2 * h], suf[2 * h] = new_p, new_s
        h *= 2
    return pre, suf


def _group_block_products(total):
    rows = total.shape[0]
    m_idx = lax.broadcasted_iota(jnp.int32, total.shape, 0) % PLANE_ROWS
    wpre, wsuf = [None], [None]
    for d in range(1, PLANE_ROWS):
        wpre.append(_mul(wpre[-1], pltpu.roll(total, d, 0)))
        wsuf.append(_mul(wsuf[-1], pltpu.roll(total, rows - d, 0)))
    cpre, csuf = {1: None}, {1: None}
    u = 2
    while u <= PLANE_ROWS:
        off = m_idx % u
        a = jnp.ones_like(total)
        b = jnp.ones_like(total)
        for d in range(1, u):
            a = jnp.where(off == d, wpre[d], a)
            b = jnp.where(off == u - 1 - d, wsuf[d], b)
        cpre[u], csuf[u] = a, b
        u *= 2
    return cpre, csuf


def _block_diag(a, b):
    za, zb = jnp.zeros_like(a), jnp.zeros_like(b)
    return jnp.concatenate([jnp.concatenate([a, zb], axis=1),
                            jnp.concatenate([za, b], axis=1)], axis=0)


def _hgrn2_tile(slab_ref, lb, norm_g, masks_ref, state_ref, out_ref):
    tile = slab_ref.shape[1]
    n_chunks = tile // HG_CHUNK
    plane_len = tile // N_PLANES
    pair_w = 2 * HG_DK
    n_pairs = HG_HEADS // 2

    def planes(slab0):
        return [jnp.concatenate(
            [slab_ref[slab0 + h, pl.ds(r, plane_len, stride=N_PLANES), :] for h in range(HG_HEADS)],
            axis=1) for r in range(N_PLANES)]

    def chunk_rows(ps, c, lanes=slice(None)):
        return jnp.concatenate([p[c * PLANE_ROWS:(c + 1) * PLANE_ROWS, lanes] for p in ps], axis=0)

    def pair_lanes(p):
        return slice(p * pair_w, (p + 1) * pair_w)

    def heads_block_diag(x):
        return _block_diag(x[:, :HG_DK], x[:, HG_DK:])

    lb_gap = 1.0 - lb
    f = [lb + lb_gap * jax.nn.sigmoid(x) for x in planes(SLAB_F)]
    q = planes(SLAB_Q)
    k = [1.0 - x for x in f]
    v = planes(SLAB_I)

    pre, suf = _plane_block_products(f)
    pre8, suf8 = pre[N_PLANES], suf[N_PLANES]
    assert 2 ** (N_LEVELS - 1) * N_PLANES * 2 == HG_CHUNK
    cpre, csuf = _group_block_products(pre8[N_PLANES - 1])

    q8 = [_mul(a, b) for a, b in zip(q, pre8)]
    k8 = [_mul(a, b) for a, b in zip(k, suf8)]
    def per_head(x, fn):
        return jnp.concatenate(
            [fn(x[:, h * HG_DK:(h + 1) * HG_DK], h) for h in range(HG_HEADS)], axis=1)

    group_out = [None] * N_PLANES
    for rj in range(N_PLANES):
        decay = None
        for ri in range(rj, N_PLANES):
            if ri > rj:
                decay = _mul(decay, f[ri])
            weighted = _mul(q[ri] * k[rj], decay)
            part = per_head(weighted, lambda a, h: jnp.sum(a, axis=-1, keepdims=True)
                            * v[rj][:, h * HG_DV:(h + 1) * HG_DV])
            group_out[ri] = part if group_out[ri] is None else group_out[ri] + part
        if rj % 4 == 3:
            yield

    scores = [[None] * n_pairs for _ in range(n_chunks)]
    u = 1
    for lvl in range(N_LEVELS):
        ql = [_mul(x, cpre[u]) for x in q8]
        kl = [_mul(x, csuf[u]) for x in k8]
        for c in range(n_chunks):
            for p in range(n_pairs):
                lhs = chunk_rows(ql, c, pair_lanes(p)).astype(BF16)
                rhs = heads_block_diag(chunk_rows(kl, c, pair_lanes(p)).astype(BF16))
                s = _dot_nt(lhs, rhs) * masks_ref[lvl]
                scores[c][p] = s if scores[c][p] is None else scores[c][p] + s
        u *= 2
        if lvl != 1:
            yield

    q_full = [x * cpre[PLANE_ROWS] for x in q8]
    k_full = [x * csuf[PLANE_ROWS] for x in k8]
    decay = pre8[N_PLANES - 1] * cpre[PLANE_ROWS]

    states = [state_ref[h] for h in range(HG_HEADS)]
    outs = []
    for c in range(n_chunks):
        last = (c + 1) * PLANE_ROWS - 1
        o_pairs = []
        for p in range(n_pairs):
            lanes = pair_lanes(p)
            vc = chunk_rows(v, c, lanes).astype(BF16)
            st = _block_diag(states[2 * p], states[2 * p + 1]).astype(BF16)
            o = (_dot_nt(chunk_rows(q_full, c, lanes).astype(BF16), st)
                 + _dot(scores[c][p].astype(BF16), heads_block_diag(vc)))
            upd = _dot_tn(vc, chunk_rows(k_full, c, lanes).astype(BF16))
            for i in range(2):
                h = 2 * p + i
                blk = slice(i * HG_DK, (i + 1) * HG_DK)
                states[h] = (states[h] * decay[last:last + 1, h * HG_DK:(h + 1) * HG_DK]
                             + upd[blk, blk])
            o_pairs.append(o)
        outs.append(jnp.concatenate(o_pairs, axis=1))
    for h in range(HG_HEADS):
        state_ref[h] = states[h]
    yield

    gate = planes(SLAB_G)
    for r in range(N_PLANES):
        o_r = (jnp.concatenate([o[r * PLANE_ROWS:(r + 1) * PLANE_ROWS] for o in outs], axis=0)
               + group_out[r])
        g_r = gate[r]
        for h in range(HG_HEADS):
            lanes = slice(h * HG_DV, (h + 1) * HG_DV)
            y = _rms_norm(o_r[:, lanes], norm_g) * (g_r[:, lanes] * jax.nn.sigmoid(g_r[:, lanes]))
            out_ref[h, pl.ds(r, plane_len, stride=N_PLANES), :] = y


def _swa_tile(slab_ref, seq_start, ctab, stab, gq, gk, seg_mean, sinks_ref, kpad_ref, vpad_ref,
              out_ref):
    tile = slab_ref.shape[1]
    n_blocks = tile // SW_BLOCK
    n_pairs = SW_HEADS // 2
    q = jnp.concatenate([slab_ref[SLAB_SQ + s] for s in range(SW_WIDTH // LANES)], axis=1)
    k = slab_ref[SLAB_SK]
    v = slab_ref[SLAB_SV]

    def head_norm(t, g, mean_mat):
        ms = _dot((t * t).astype(BF16), mean_mat)
        return t * lax.rsqrt(ms + EPS) * g

    qn = _rope(head_norm(q, gq, seg_mean[...]), ctab, stab) * (SW_HEAD_DIM ** -0.5)
    kn = _rope(head_norm(k, gk, seg_mean[:LANES, :LANES]), ctab, stab)

    lane = lax.broadcasted_iota(jnp.int32, (tile, LANES), 1)
    low = lane < SW_HEAD_DIM

    def padded(t):
        rolled = pltpu.roll(t, SW_HEAD_DIM, 1)
        zero = jnp.zeros_like(t)
        out = []
        for kv_head in range(SW_KV_HEADS):
            src_even, src_odd = (t, rolled) if kv_head == 0 else (rolled, t)
            out.append(jnp.where(low, src_even, zero).astype(BF16))
            out.append(jnp.where(low, zero, src_odd).astype(BF16))
        return out

    kpads, vpads = padded(kn), padded(v)
    qb = qn.astype(BF16)
    yield

    def block_operand(pads, carry_ref, blk, kv):
        parts = []
        for i in (kv, kv + 1):
            prev = carry_ref[i] if blk == 0 else pads[i][(blk - 1) * SW_BLOCK:blk * SW_BLOCK]
            parts += [prev, pads[i][blk * SW_BLOCK:(blk + 1) * SW_BLOCK]]
        return jnp.concatenate(parts, axis=0)

    units = [(blk, pair) for blk in range(n_blocks) for pair in range(n_pairs)]
    heads = [(blk, pair, half) for blk, pair in units for half in range(2)]
    kv_of = lambda pair: 2 * (pair // (n_pairs // SW_KV_HEADS))

    scores = {}
    for blk, pair in units:
        rows = slice(blk * SW_BLOCK, (blk + 1) * SW_BLOCK)
        scores[blk, pair] = _dot_nt(qb[rows, pair * LANES:(pair + 1) * LANES],
                                    block_operand(kpads, kpad_ref, blk, kv_of(pair)))

    yield
    qi = lax.broadcasted_iota(jnp.int32, (SW_BLOCK, SW_BLOCK), 0)
    kj = lax.broadcasted_iota(jnp.int32, (SW_BLOCK, SW_BLOCK), 1)
    from_prev = kj > qi
    start_bias = jnp.where(seq_start, -jnp.inf, 0.0)

    merged, sink = {}, {}
    for blk, pair, half in heads:
        s = scores[blk, pair]
        s_prev = s[:, (2 * half) * SW_BLOCK:(2 * half + 1) * SW_BLOCK]
        s_cur = s[:, (2 * half + 1) * SW_BLOCK:(2 * half + 2) * SW_BLOCK]
        if blk == 0:
            s_prev = s_prev + start_bias
        merged[blk, pair, half] = jnp.where(from_prev, s_prev, s_cur)
        sink[blk, pair, half] = sinks_ref[2 * pair + half]
    top = {u: jnp.maximum(jnp.max(merged[u], axis=-1, keepdims=True), sink[u]) for u in heads}
    prob = {u: jnp.exp(merged[u] - top[u]) for u in heads}
    inv = {u: 1.0 / (jnp.sum(prob[u], axis=-1, keepdims=True) + jnp.exp(sink[u] - top[u]))
           for u in heads}

    yield
    low_o = lax.broadcasted_iota(jnp.int32, (SW_BLOCK, LANES), 1) < SW_HEAD_DIM
    for blk, pair in units:
        ps = []
        for half in range(2):
            p = prob[blk, pair, half]
            zero = jnp.zeros_like(p)
            ps += [jnp.where(from_prev, p, zero).astype(BF16),
                   jnp.where(from_prev, zero, p).astype(BF16)]
        o = _dot(jnp.concatenate(ps, axis=1), block_operand(vpads, vpad_ref, blk, kv_of(pair)))
        o = o * jnp.where(low_o, inv[blk, pair, 0], inv[blk, pair, 1])
        out_ref[blk * SW_BLOCK:(blk + 1) * SW_BLOCK, pair * LANES:(pair + 1) * LANES] = (
            o.astype(out_ref.dtype))

    last = slice((n_blocks - 1) * SW_BLOCK, n_blocks * SW_BLOCK)
    for i in range(2 * SW_KV_HEADS):
        kpad_ref[i] = kpads[i][last]
        vpad_ref[i] = vpads[i][last]


def _project_pieces(x_ref, g1_ref, win_ref, slab_ref):
    xn = _rms_norm(x_ref[...], g1_ref[...]).astype(BF16)
    col_step = 2 * LANES

    def piece(c0):
        c1 = min(c0 + col_step, IN_WIDTH)
        proj = _dot(xn, win_ref[:, c0:c1])
        for s in range((c1 - c0) // LANES):
            slab_ref[c0 // LANES + s] = proj[:, s * LANES:(s + 1) * LANES]

    return [functools.partial(piece, c0) for c0 in range(0, IN_WIDTH, col_step)]


def _mix_pieces(slab_ref, seq_start, lb, sinks_ref, ctab_ref, stab_ref, hgn_ref, gq_ref, gk_ref,
                segm_ref, masks_ref, mix_ref, hg_ref, state_ref, kpad_ref, vpad_ref):
    yield from _swa_tile(slab_ref, seq_start, ctab_ref[...], stab_ref[...], gq_ref[...],
                         gk_ref[...], segm_ref, sinks_ref, kpad_ref, vpad_ref,
                         mix_ref.at[:, pl.ds(HG_WIDTH, SW_WIDTH)])
    yield
    yield from _hgrn2_tile(slab_ref, lb, hgn_ref[...], masks_ref, state_ref, hg_ref)
    for h in range(HG_HEADS):
        mix_ref[:, h * HG_DV:(h + 1) * HG_DV] = hg_ref[h].astype(BF16)


STEP_ORDER = "TMPPP TMPPP TMPP TMPP TMMP TMM TM TM".replace(" ", "")


WEIGHT_CHUNK_ROWS = 512
WEIGHT_STAGE_WIDTH = 1024
WEIGHT_STAGE_SLOTS = 3


def _load_weights(pairs, stage_ref, sem):
    jobs = [(src, dst, r0, c0, min(WEIGHT_STAGE_WIDTH, src.shape[1] - c0))
            for src, dst in pairs
            for r0 in range(0, src.shape[0], WEIGHT_CHUNK_ROWS)
            for c0 in range(0, src.shape[1], WEIGHT_STAGE_WIDTH)]

    def copy(i):
        src, _, r0, c0, width = jobs[i]
        return pltpu.make_async_copy(src.at[pl.ds(r0, WEIGHT_CHUNK_ROWS), pl.ds(c0, width)],
                                     stage_ref.at[i % WEIGHT_STAGE_SLOTS, :, pl.ds(0, width)],
                                     sem.at[i % WEIGHT_STAGE_SLOTS])

    ahead = WEIGHT_STAGE_SLOTS - 1
    for i in range(min(ahead, len(jobs))):
        copy(i).start()
    for i, (_, dst, r0, c0, width) in enumerate(jobs):
        if i + ahead < len(jobs):
            copy(i + ahead).start()
        copy(i).wait()
        dst[pl.ds(r0, WEIGHT_CHUNK_ROWS), pl.ds(c0, width)] = (
            stage_ref[i % WEIGHT_STAGE_SLOTS, :, pl.ds(0, width)].astype(BF16))


def _layer_kernel(sinks_ref, xp_ref, xt_ref, ctab_ref, stab_ref, g1_ref, win_hbm, hlb_ref, hgn_ref,
                  gq_ref, gk_ref, segm_ref, masks_ref, wout_hbm, g2_ref, wq_hbm, xgq_ref, kmem_ref,
                  vmem_ref, wo_hbm, g3_ref, up_hbm, down_hbm, out_ref,
                  slab_a, slab_b, mix_a, mix_b, hg_ref, state_ref, kpad_ref, vpad_ref,
                  win_ref, wout_ref, wq_ref, wo_ref, up_ref, down_ref, stage_ref, stage_sem,
                  *, tiles_per_seq):
    s = pl.program_id(0)
    seq_start = lax.rem(s + tiles_per_seq - 1, tiles_per_seq) == 0

    @pl.when(s == 0)
    def _():
        slab_b[...] = jnp.zeros_like(slab_b)
        mix_a[...] = jnp.zeros_like(mix_a)
        _load_weights([(win_hbm, win_ref), (wout_hbm, wout_ref), (wq_hbm, wq_ref),
                       (wo_hbm, wo_ref), (up_hbm, up_ref), (down_hbm, down_ref)],
                      stage_ref, stage_sem)

    @pl.when(seq_start | (s == 0))
    def _():
        state_ref[...] = jnp.zeros_like(state_ref)
        kpad_ref[...] = jnp.zeros_like(kpad_ref)
        vpad_ref[...] = jnp.zeros_like(vpad_ref)

    hlb = hlb_ref[...]
    e = jnp.exp(hlb - jnp.max(hlb, axis=0, keepdims=True))
    lb = e[0:1, :] / jnp.sum(e, axis=0, keepdims=True)

    def step(project_slab, mix_slab, mix_out, tail_in):
        streams = {
            "P": iter(_project_pieces(xp_ref, g1_ref, win_ref, project_slab)),
            "M": _mix_pieces(mix_slab, seq_start, lb, sinks_ref, ctab_ref, stab_ref, hgn_ref,
                             gq_ref, gk_ref, segm_ref, masks_ref, mix_out, hg_ref, state_ref,
                             kpad_ref, vpad_ref),
            "T": _tail_stages(xt_ref, tail_in, wout_ref, g2_ref, wq_ref, xgq_ref, kmem_ref.at[0],
                              vmem_ref.at[0], wo_ref, g3_ref, up_ref, down_ref, out_ref),
        }
        for name in STEP_ORDER:
            piece = next(streams[name], None)
            if callable(piece):
                piece()
        for stream in streams.values():
            assert next(stream, "done") == "done"

    @pl.when(s % 2 == 0)
    def _():
        step(slab_a, slab_b, mix_b, mix_a)

    @pl.when(s % 2 == 1)
    def _():
        step(slab_b, slab_a, mix_a, mix_b)


def _layer(x, mem_k, mem_v, ctab, stab, sinks, g1, w_in, hlb, hgn, gq, gk, w_out, g2, wq, xgq, wo,
           g3, up, down):
    B, S, D = x.shape
    M = mem_k.shape[1]
    tile = MIX_TILE
    tiles_per_seq = S // tile
    n_tiles = B * tiles_per_seq
    seg_mean = jnp.asarray(np.kron(np.eye(SW_HEADS), np.full((SW_HEAD_DIM, SW_HEAD_DIM),
                                                              1.0 / SW_HEAD_DIM)), BF16)
    masks = jnp.asarray(_hgrn2_level_masks())

    def tile_index(lag):
        return lambda s: jnp.clip(s - lag, 0, n_tiles - 1)

    def tok(width, lag):
        index = tile_index(lag)
        return pl.BlockSpec((tile, width), lambda s: (index(s), 0))

    def const(shape):
        return pl.BlockSpec(shape, lambda s: (0,) * len(shape), pipeline_mode=pl.Buffered(1))

    tail_index = tile_index(2)
    per_seq = pl.BlockSpec((1, M, XA_WIDTH), lambda s: (tail_index(s) // tiles_per_seq, 0, 0))
    slab = pltpu.VMEM((N_SLABS, tile, LANES), F32)
    mix_buf = pltpu.VMEM((tile, MIX_WIDTH), BF16)
    x2 = x.reshape(B * S, D)
    weights = (w_in, w_out, wq, wo, up, down)
    for w in weights:
        assert w.dtype == F32 and w.shape[0] % WEIGHT_CHUNK_ROWS == 0
        assert w.shape[1] % LANES == 0
    in_hbm = pl.BlockSpec(memory_space=pl.ANY)
    out = pl.pallas_call(
        functools.partial(_layer_kernel, tiles_per_seq=tiles_per_seq),
        grid=(n_tiles + 2,),
        in_specs=[pl.BlockSpec(memory_space=pltpu.SMEM),
                  tok(D, 0), tok(D, 2), tok(LANES, 1), tok(LANES, 1),
                  const((1, D)), in_hbm, const(hlb.shape), const((1, HG_DV)),
                  const((1, SW_WIDTH)), const((1, LANES)), const((SW_WIDTH, SW_WIDTH)),
                  const((N_LEVELS, HG_CHUNK, 2 * HG_CHUNK)),
                  in_hbm, const((1, D)), in_hbm, const((1, XA_HEAD_DIM)),
                  per_seq, per_seq, in_hbm, const((1, D)), in_hbm, in_hbm],
        out_specs=tok(D, 2),
        out_shape=jax.ShapeDtypeStruct((B * S, D), F32),
        scratch_shapes=[slab, slab, mix_buf, mix_buf,
                        pltpu.VMEM((HG_HEADS, tile, LANES), F32),
                        pltpu.VMEM((HG_HEADS, HG_DV, HG_DK), F32),
                        pltpu.VMEM((2 * SW_KV_HEADS, SW_BLOCK, LANES), BF16),
                        pltpu.VMEM((2 * SW_KV_HEADS, SW_BLOCK, LANES), BF16)]
                       + [pltpu.VMEM(w.shape, BF16) for w in weights]
                       + [pltpu.VMEM((WEIGHT_STAGE_SLOTS, WEIGHT_CHUNK_ROWS, WEIGHT_STAGE_WIDTH), F32),
                          pltpu.SemaphoreType.DMA((WEIGHT_STAGE_SLOTS,))],
        compiler_params=pltpu.CompilerParams(
            dimension_semantics=("arbitrary",),
            vmem_limit_bytes=VMEM_LIMIT_BYTES),
        name="layer",
    )(sinks, x2, x2, ctab.reshape(B * S, LANES), stab.reshape(B * S, LANES), g1, w_in, hlb, hgn,
      gq, gk, seg_mean, masks, w_out, g2, wq, xgq, mem_k, mem_v, wo, g3, up, down)
    return out.reshape(B, S, D)


def _tail_stages(x_ref, mix_ref, wout_ref, g2_ref, wq_ref, gq_ref, k_ref, v_ref, wo_ref,
                 g3_ref, up_ref, down_ref, out_ref):
    h1 = x_ref[...] + _dot(mix_ref[...], wout_ref[...])
    yield

    q = _dot(_rms_norm(h1, g2_ref[...]).astype(BF16), wq_ref[...])
    yield

    lanes = [slice(h * XA_HEAD_DIM, (h + 1) * XA_HEAD_DIM) for h in range(XA_HEADS)]
    qn = [_rms_norm(q[:, sl], gq_ref[...]) * (XA_HEAD_DIM ** -0.5) for sl in lanes]
    s = [_dot_nt(a.astype(BF16), k_ref[:, sl]) for a, sl in zip(qn, lanes)]
    p = [jnp.exp(a - jnp.max(a, axis=-1, keepdims=True)) for a in s]
    inv = [1.0 / jnp.sum(a, axis=-1, keepdims=True) for a in p]
    heads = [_dot(a.astype(BF16), v_ref[:, sl]) * b for a, b, sl in zip(p, inv, lanes)]
    yield

    h2 = h1 + _dot(jnp.concatenate(heads, axis=1).astype(BF16), wo_ref[...])
    out_ref[...] = h2
    hn = _rms_norm(h2, g3_ref[...]).astype(BF16)
    for c0 in range(0, up_ref.shape[1], FF_CHUNK):
        yield
        a = jnp.maximum(_dot(hn, up_ref[:, c0:c0 + FF_CHUNK]), 0.0)
        out_ref[...] += _dot((a * a).astype(BF16), down_ref[c0:c0 + FF_CHUNK, :])


def kernel(x, mem, positions, norm1_g, w_in, hg_lower_bounds, hg_norm_g, sw_q_norm_g, sw_k_norm_g,
           sw_sinks, w_out, norm2_g, mem_norm_g, xa_wq, xa_wkv, xa_q_norm_g, xa_k_norm_g, xa_wo,
           norm3_g, mlp_up, mlp_down):
    depth = norm1_g.shape[0]
    assert depth == 1 and x.shape[1] % MIX_TILE == 0
    ctab, stab = _rope_tables(positions)
    h = x
    for l in range(depth):
        kmem, vmem = _mem_kv(mem, mem_norm_g[l][None], xa_wkv[l].astype(BF16), xa_k_norm_g[l][None])
        h = _layer(h, kmem, vmem, ctab, stab, sw_sinks[l], norm1_g[l][None], w_in[l],
                   hg_lower_bounds, hg_norm_g[l][None], jnp.tile(sw_q_norm_g[l], SW_HEADS)[None],
                   jnp.tile(sw_k_norm_g[l], SW_KV_HEADS)[None], w_out[l],
                   norm2_g[l][None], xa_wq[l], xa_q_norm_g[l][None],
                   xa_wo[l], norm3_g[l][None], mlp_up[l], mlp_down[l])
    return h
```
